```python
import math
import jax
import jax.numpy as jnp
from jax import lax
import numpy as np

D_MODEL = 1024
BATCH = 2
SEQ = 8192
DEPTH = 2
DEC_BATCH = 128
DEC_SEQ = 8
PAST_LEN = 8192
PAGE_SIZE = 128

RMS_EPS = 1e-6
L2_EPS = 1e-6
NEG_INF = -1e30

GDN_HEADS = 6
GDN_DK = 128
GDN_DV = 128
GDN_CONV_K = 4
GDN_CHUNK = 64
GDN_QK_WIDTH = GDN_HEADS * GDN_DK
GDN_V_WIDTH = GDN_HEADS * GDN_DV
GDN_CONV_DIM = 2 * GDN_QK_WIDTH + GDN_V_WIDTH

SWA_HEADS = 12
SWA_KV_HEADS = 4
SWA_HEAD_DIM = 64
SWA_GROUP = SWA_HEADS // SWA_KV_HEADS
SWA_WINDOW = 128
SWA_Q_WIDTH = SWA_HEADS * SWA_HEAD_DIM
SWA_KV_WIDTH = SWA_KV_HEADS * SWA_HEAD_DIM

N_MEM = 256
MEM_HEADS = 4
MEM_HEAD_DIM = 64
MEM_WIDTH = MEM_HEADS * MEM_HEAD_DIM

MIX_WIDTH = GDN_V_WIDTH + MEM_WIDTH
GDN_IN_DIM = GDN_CONV_DIM + GDN_V_WIDTH + 2 * GDN_HEADS + MEM_WIDTH
SWA_IN_DIM = SWA_Q_WIDTH + 2 * SWA_KV_WIDTH + MEM_WIDTH

FFN_DIM = 2816
N_EXPERTS = 8
TOP_K = 2
EXPERT_DIM = 3584

N_GDN_LAYERS = (DEPTH + 1) // 2
N_SWA_LAYERS = DEPTH // 2

kernel_name = 'hybrid_gdn_swa_memory_decoder_step'


def rms_norm(x, g):
    xf = x.astype(jnp.float32)
    y = xf * lax.rsqrt(jnp.mean(xf * xf, axis=-1, keepdims=True) + RMS_EPS)
    return (y * g.astype(jnp.float32)).astype(x.dtype)


def l2_normalize(x):
    xf = x.astype(jnp.float32)
    return xf * lax.rsqrt(jnp.sum(xf * xf, axis=-1, keepdims=True) + L2_EPS)


def swiglu(h, w_gu, w_down):
    gate, up = jnp.split(h @ w_gu, 2, axis=-1)
    return (jax.nn.silu(gate) * up) @ w_down


def moe_swiglu(h, w_router, w_gu, w_down):
    logits = jnp.einsum('bld,de->ble', h, w_router).astype(jnp.float32)
    top_val, top_idx = lax.top_k(logits, TOP_K)
    top_w = jax.nn.softmax(top_val, axis=-1)
    gate = jnp.sum(jax.nn.one_hot(top_idx, N_EXPERTS, dtype=jnp.float32) * top_w[..., None], axis=-2)
    out = jnp.zeros_like(h)
    for e in range(N_EXPERTS):
        out = out + swiglu(h, w_gu[e], w_down[e]) * gate[..., e:e + 1].astype(h.dtype)
    return out


def memory_kv(mem, g, w):
    b, m = mem.shape[:2]
    k, v = jnp.split(rms_norm(mem, g) @ w, 2, axis=-1)
    return (k.reshape(b, m, MEM_HEADS, MEM_HEAD_DIM), v.reshape(b, m, MEM_HEADS, MEM_HEAD_DIM))


def memory_attention(xq, mem_k, mem_v):
    b, l = xq.shape[:2]
    q = xq.reshape(b, l, MEM_HEADS, MEM_HEAD_DIM)
    s = jnp.einsum('blhd,bmhd->bhlm', q, mem_k.astype(q.dtype)).astype(jnp.float32) * MEM_HEAD_DIM ** -0.5
    p = jax.nn.softmax(s, axis=-1).astype(xq.dtype)
    o = jnp.einsum('bhlm,bmhd->blhd', p, mem_v.astype(xq.dtype))
    return o.reshape(b, l, MEM_WIDTH)


def causal_short_conv(u, buf, w):
    l = u.shape[1]
    full = jnp.concatenate([buf.astype(u.dtype), u], axis=1)
    out = full[:, 0:l] * w[0]
    for j in range(1, GDN_CONV_K):
        out = out + full[:, j:j + l] * w[j]
    return jax.nn.silu(out), full[:, l:]


def gated_delta_rule(q, k, v, g, beta, s0):
    b, l, h, dk = q.shape
    dv = v.shape[-1]
    c = min(GDN_CHUNK, l)
    n = -(-l // c)
    pad = n * c - l

    def chunks(t):
        t = jnp.pad(t, [(0, 0), (0, pad)] + [(0, 0)] * (t.ndim - 2))
        t = t.reshape((b, n, c) + t.shape[2:])
        return t.transpose((1, 0, 3, 2) + tuple(range(4, t.ndim)))

    qc, kc, vc, bc = chunks(q), chunks(k), chunks(v), chunks(beta)
    gc = jnp.cumsum(chunks(g), axis=-1)
    tri = jnp.tril(jnp.ones((c, c), dtype=bool))
    strict = jnp.tril(jnp.ones((c, c), dtype=bool), -1)
    diff = gc[..., :, None] - gc[..., None, :]
    decay = jnp.where(tri, jnp.exp(jnp.where(tri, diff, 0.0)), 0.0)
    kb = kc * bc[..., None]
    lmat = jnp.where(strict, jnp.einsum('nbhid,nbhjd->nbhij', kb, kc) * decay, 0.0)
    a = lmat + jnp.eye(c, dtype=lmat.dtype)
    rhs = jnp.concatenate([vc * bc[..., None], kb * jnp.exp(gc)[..., None]], axis=-1)
    sol = lax.linalg.triangular_solve(a, rhs, left_side=True, lower=True, unit_diagonal=True)
    u, w = sol[..., :dv], sol[..., dv:]
    qk = jnp.einsum('nbhid,nbhjd->nbhij', qc, kc) * decay

    def step(s, xs):
        q_c, k_c, u_c, w_c, qk_c, g_c = xs
        v_new = u_c - jnp.einsum('bhcd,bhde->bhce', w_c, s)
        o = (jnp.einsum('bhcd,bhde->bhce', q_c * jnp.exp(g_c)[..., None], s)
             + jnp.einsum('bhij,bhje->bhie', qk_c, v_new))
        g_last = g_c[..., -1:]
        s = (s * jnp.exp(g_last)[..., None]
             + jnp.einsum('bhcd,bhce->bhde', k_c * jnp.exp(g_last - g_c)[..., None], v_new))
        return s, o

    s_final, o = lax.scan(step, s0, (qc, kc, u, w, qk, gc))
    o = o.transpose(1, 0, 3, 2, 4).reshape(b, n * c, h, dv)[:, :l]
    return o, s_final


def gdn_mixer(h, conv_buf, s0, w_in, conv_w, a_log, dt_bias, norm_g, w_out, mem_k, mem_v):
    b, l, _ = h.shape
    proj = h @ w_in
    o0 = GDN_CONV_DIM
    o1 = o0 + GDN_V_WIDTH
    o2 = o1 + GDN_HEADS
    o3 = o2 + GDN_HEADS
    qkv, z, b_logit, a_logit, mem_q = proj[..., :o0], proj[..., o0:o1], proj[..., o1:o2], proj[..., o2:o3], proj[..., o3:]
    qkv, new_conv = causal_short_conv(qkv, conv_buf, conv_w)
    q = qkv[..., :GDN_QK_WIDTH].reshape(b, l, GDN_HEADS, GDN_DK)
    k = qkv[..., GDN_QK_WIDTH:2 * GDN_QK_WIDTH].reshape(b, l, GDN_HEADS, GDN_DK)
    v = qkv[..., 2 * GDN_QK_WIDTH:].reshape(b, l, GDN_HEADS, GDN_DV).astype(jnp.float32)
    q = l2_normalize(q) * GDN_DK ** -0.5
    k = l2_normalize(k)
    beta = jax.nn.sigmoid(b_logit.astype(jnp.float32))
    g = -jnp.exp(a_log.astype(jnp.float32)) * jax.nn.softplus(a_logit.astype(jnp.float32) + dt_bias.astype(jnp.float32))
    o, s_new = gated_delta_rule(q, k, v, g, beta, s0.astype(jnp.float32))
    z = z.reshape(b, l, GDN_HEADS, GDN_DV).astype(jnp.float32)
    o = rms_norm(o, norm_g) * jax.nn.silu(z)
    o = o.reshape(b, l, GDN_V_WIDTH).astype(h.dtype)
    mem_o = memory_attention(mem_q, mem_k, mem_v)
    out = jnp.concatenate([o, mem_o], axis=-1) @ w_out
    return out, new_conv, s_new.astype(h.dtype)


def alibi_slopes():
    return 2.0 ** (-8.0 * jnp.arange(1, SWA_HEADS + 1, dtype=jnp.float32) / SWA_HEADS)


def banded_sink_attention(q, kk, vv, sinks, pos0, n_prev):
    b, l = q.shape[:2]
    bq = min(SWA_WINDOW, l)
    nb = -(-l // bq)
    pad = nb * bq - l
    q = jnp.pad(q, ((0, 0), (0, pad), (0, 0), (0, 0)))
    kk = jnp.pad(kk, ((0, 0), (0, pad), (0, 0), (0, 0)))
    vv = jnp.pad(vv, ((0, 0), (0, pad), (0, 0), (0, 0)))
    span = n_prev + bq
    starts = jnp.arange(nb) * bq
    idx = starts[:, None] + jnp.arange(span)[None, :]
    kb = kk[:, idx]
    vb = vv[:, idx]
    qb = q.reshape(b, nb, bq, SWA_KV_HEADS, SWA_GROUP, SWA_HEAD_DIM)
    s = jnp.einsum('bnqkgd,bnskd->bnkgqs', qb, kb).astype(jnp.float32) * SWA_HEAD_DIM ** -0.5
    tq = pos0 + starts[:, None] + jnp.arange(bq)[None, :]
    ts = pos0 - n_prev + idx
    dist = tq[:, :, None] - ts[:, None, :]
    valid = (dist >= 0) & (dist < SWA_WINDOW) & (ts[:, None, :] >= 0)
    slopes = alibi_slopes().reshape(SWA_KV_HEADS, SWA_GROUP)
    s = s - slopes[None, None, :, :, None, None] * dist.astype(jnp.float32)[None, :, None, None]
    s = jnp.where(valid[None, :, None, None], s, NEG_INF)
    sink = sinks.astype(jnp.float32).reshape(SWA_KV_HEADS, SWA_GROUP)[None, None, :, :, None]
    m = jnp.maximum(jnp.max(s, axis=-1), sink)
    p = jnp.exp(s - m[..., None])
    p = p / (jnp.sum(p, axis=-1) + jnp.exp(sink - m))[..., None]
    o = jnp.einsum('bnkgqs,bnskd->bnqkgd', p.astype(vb.dtype), vb)
    return o.reshape(b, nb * bq, SWA_HEADS, SWA_HEAD_DIM)[:, :l]


def swa_mixer(h, k_prev, v_prev, pos0, w_in, sinks, w_out, mem_k, mem_v):
    b, l, _ = h.shape
    proj = h @ w_in
    o0 = SWA_Q_WIDTH
    o1 = o0 + SWA_KV_WIDTH
    o2 = o1 + SWA_KV_WIDTH
    q = proj[..., :o0].reshape(b, l, SWA_HEADS, SWA_HEAD_DIM)
    k = proj[..., o0:o1].reshape(b, l, SWA_KV_HEADS, SWA_HEAD_DIM)
    v = proj[..., o1:o2].reshape(b, l, SWA_KV_HEADS, SWA_HEAD_DIM)
    mem_q = proj[..., o2:]
    n_prev = k_prev.shape[1]
    kk = jnp.concatenate([k_prev.astype(k.dtype), k], axis=1)
    vv = jnp.concatenate([v_prev.astype(v.dtype), v], axis=1)
    o = banded_sink_attention(q, kk, vv, sinks, pos0, n_prev).reshape(b, l, SWA_Q_WIDTH)
    mem_o = memory_attention(mem_q, mem_k, mem_v)
    out = jnp.concatenate([o, mem_o], axis=-1) @ w_out
    return out, kk[:, -n_prev:], vv[:, -n_prev:]


def setup_inputs(seed: int = 0) -> dict:
    key = jax.random.key(seed)
    ks = list(jax.random.split(key, 32))

    def nrm(i, shape, scale):
        return jax.random.normal(ks[i], shape, jnp.float32) * scale

    swa_buf = min(SWA_WINDOW, PAST_LEN)
    return {
        'x_prompt': nrm(0, (BATCH, SEQ, D_MODEL), 1.0),
        'x_sample': nrm(1, (DEC_BATCH, DEC_SEQ, D_MODEL), 1.0),
        'state_gdn_conv': nrm(2, (N_GDN_LAYERS, DEC_BATCH, GDN_CONV_K - 1, GDN_CONV_DIM), 1.0),
        'state_gdn_ssm': nrm(3, (N_GDN_LAYERS, DEC_BATCH, GDN_HEADS, GDN_DK, GDN_DV), 0.1),
        'cache_swa_k': nrm(4, (N_SWA_LAYERS, DEC_BATCH, swa_buf, SWA_KV_HEADS, SWA_HEAD_DIM), 1.0),
        'cache_swa_v': nrm(5, (N_SWA_LAYERS, DEC_BATCH, swa_buf, SWA_KV_HEADS, SWA_HEAD_DIM), 1.0),
        'cache_mem_k': nrm(6, (DEPTH, DEC_BATCH, N_MEM, MEM_HEADS, MEM_HEAD_DIM), 1.0),
        'cache_mem_v': nrm(7, (DEPTH, DEC_BATCH, N_MEM, MEM_HEADS, MEM_HEAD_DIM), 1.0),
        'mem_prompt': nrm(8, (BATCH, N_MEM, D_MODEL), 1.0),
        'attn_norm': 1.0 + nrm(9, (DEPTH, D_MODEL), 0.02),
        'ffn_norm': 1.0 + nrm(10, (DEPTH, D_MODEL), 0.02),
        'mem_norm': 1.0 + nrm(11, (DEPTH, D_MODEL), 0.02),
        'final_norm': 1.0 + nrm(12, (D_MODEL,), 0.02),
        'w_in_gdn': nrm(13, (N_GDN_LAYERS, D_MODEL, GDN_IN_DIM), D_MODEL ** -0.5),
        'gdn_conv_w': nrm(14, (N_GDN_LAYERS, GDN_CONV_K, GDN_CONV_DIM), GDN_CONV_K ** -0.5),
        'gdn_a_log': jnp.log(jax.random.uniform(ks[15], (N_GDN_LAYERS, GDN_HEADS), jnp.float32, 1.0, 16.0)),
        'gdn_dt_bias': nrm(16, (N_GDN_LAYERS, GDN_HEADS), 0.1),
        'gdn_norm': 1.0 + nrm(17, (N_GDN_LAYERS, GDN_DV), 0.02),
        'w_out_gdn': nrm(18, (N_GDN_LAYERS, MIX_WIDTH, D_MODEL), MIX_WIDTH ** -0.5),
        'w_in_swa': nrm(19, (N_SWA_LAYERS, D_MODEL, SWA_IN_DIM), D_MODEL ** -0.5),
        'swa_sinks': nrm(20, (N_SWA_LAYERS, SWA_HEADS), 0.5),
        'w_out_swa': nrm(21, (N_SWA_LAYERS, MIX_WIDTH, D_MODEL), MIX_WIDTH ** -0.5),
        'w_mem_kv': nrm(22, (DEPTH, D_MODEL, 2 * MEM_WIDTH), D_MODEL ** -0.5),
        'w_ffn_gu': nrm(23, (N_GDN_LAYERS, D_MODEL, 2 * FFN_DIM), D_MODEL ** -0.5),
        'w_ffn_down': nrm(24, (N_GDN_LAYERS, FFN_DIM, D_MODEL), FFN_DIM ** -0.5),
        'w_router': nrm(25, (N_SWA_LAYERS, D_MODEL, N_EXPERTS), D_MODEL ** -0.5),
        'w_exp_gu': nrm(26, (N_SWA_LAYERS, N_EXPERTS, D_MODEL, 2 * EXPERT_DIM), D_MODEL ** -0.5),
        'w_exp_down': nrm(27, (N_SWA_LAYERS, N_EXPERTS, EXPERT_DIM, D_MODEL), EXPERT_DIM ** -0.5),
    }


def reference(x_prompt, x_sample, state_gdn_conv, state_gdn_ssm, cache_swa_k, cache_swa_v, cache_mem_k, cache_mem_v,
              mem_prompt, attn_norm, ffn_norm, mem_norm, final_norm, w_in_gdn, gdn_conv_w, gdn_a_log, gdn_dt_bias,
              gdn_norm, w_out_gdn, w_in_swa, swa_sinks, w_out_swa, w_mem_kv, w_ffn_gu, w_ffn_down, w_router,
              w_exp_gu, w_exp_down):
    xp, xs = x_prompt, x_sample
    bp = xp.shape[0]
    conv_p, ssm_p, conv_s, ssm_s = [], [], [], []
    swk_p, swv_p, swk_s, swv_s = [], [], [], []
    memk_p, memv_p = [], []
    for i in range(DEPTH):
        j = i // 2
        hp = rms_norm(xp, attn_norm[i])
        hs = rms_norm(xs, attn_norm[i])
        mk_p, mv_p = memory_kv(mem_prompt, mem_norm[i], w_mem_kv[i])
        memk_p.append(mk_p)
        memv_p.append(mv_p)
        if i % 2 == 0:
            conv0 = jnp.zeros((bp, GDN_CONV_K - 1, GDN_CONV_DIM), xp.dtype)
            s0 = jnp.zeros((bp, GDN_HEADS, GDN_DK, GDN_DV), jnp.float32)
            dp, cp, sp = gdn_mixer(hp, conv0, s0, w_in_gdn[j], gdn_conv_w[j], gdn_a_log[j], gdn_dt_bias[j],
                                   gdn_norm[j], w_out_gdn[j], mk_p, mv_p)
            ds, cs, ss = gdn_mixer(hs, state_gdn_conv[j], state_gdn_ssm[j], w_in_gdn[j], gdn_conv_w[j], gdn_a_log[j],
                                   gdn_dt_bias[j], gdn_norm[j], w_out_gdn[j], cache_mem_k[i], cache_mem_v[i])
            conv_p.append(cp)
            ssm_p.append(sp)
            conv_s.append(cs)
            ssm_s.append(ss)
        else:
            prev0 = jnp.zeros((bp, SWA_WINDOW, SWA_KV_HEADS, SWA_HEAD_DIM), xp.dtype)
            dp, kp, vp = swa_mixer(hp, prev0, prev0, 0, w_in_swa[j], swa_sinks[j], w_out_swa[j], mk_p, mv_p)
            ds, k_s, v_s = swa_mixer(hs, cache_swa_k[j], cache_swa_v[j], PAST_LEN, w_in_swa[j], swa_sinks[j],
                                     w_out_swa[j], cache_mem_k[i], cache_mem_v[i])
            swk_p.append(kp)
            swv_p.append(vp)
            swk_s.append(k_s)
            swv_s.append(v_s)
        xp = xp + dp
        xs = xs + ds
        hp = rms_norm(xp, ffn_norm[i])
        hs = rms_norm(xs, ffn_norm[i])
        if i % 2 == 0:
            xp = xp + swiglu(hp, w_ffn_gu[j], w_ffn_down[j])
            xs = xs + swiglu(hs, w_ffn_gu[j], w_ffn_down[j])
        else:
            xp = xp + moe_swiglu(hp, w_router[j], w_exp_gu[j], w_exp_down[j])
            xs = xs + moe_swiglu(hs, w_router[j], w_exp_gu[j], w_exp_down[j])
    y_prompt = rms_norm(xp, final_norm)
    y_sample = rms_norm(xs, final_norm)
    new_gdn_conv_prompt = jnp.stack(conv_p)
    new_gdn_ssm_prompt = jnp.stack(ssm_p)
    new_swa_k_prompt = jnp.stack(swk_p)
    new_swa_v_prompt = jnp.stack(swv_p)
    new_mem_k_prompt = jnp.stack(memk_p)
    new_mem_v_prompt = jnp.stack(memv_p)
    new_gdn_conv_sample = jnp.stack(conv_s)
    new_gdn_ssm_sample = jnp.stack(ssm_s)
    new_swa_k_sample = jnp.stack(swk_s)
    new_swa_v_sample = jnp.stack(swv_s)
    return (y_prompt, y_sample, new_gdn_conv_prompt, new_gdn_ssm_prompt, new_swa_k_prompt, new_swa_v_prompt,
            new_mem_k_prompt, new_mem_v_prompt, new_gdn_conv_sample, new_gdn_ssm_sample, new_swa_k_sample,
            new_swa_v_sample)
```

```python
import functools

import jax
import jax.numpy as jnp
from jax import lax
from jax.experimental import pallas as pl
from jax.experimental.pallas import tpu as pltpu

F32 = jnp.float32
BF16 = jnp.bfloat16
HIGHEST = lax.Precision.HIGHEST

D_MODEL = 1024
RMS_EPS = 1e-6
L2_EPS = 1e-6
NEG_INF = -1e30

GDN_HEADS = 6
GDN_DK = 128
GDN_DV = 128
GDN_CONV_K = 4
GDN_QK_WIDTH = GDN_HEADS * GDN_DK
GDN_V_WIDTH = GDN_HEADS * GDN_DV
GDN_CONV_DIM = 2 * GDN_QK_WIDTH + GDN_V_WIDTH
GDN_ROWS = 64

SWA_HEADS = 12
SWA_KV_HEADS = 4
SWA_HEAD_DIM = 64
SWA_GROUP = SWA_HEADS // SWA_KV_HEADS
SWA_WINDOW = 128
SWA_Q_WIDTH = SWA_HEADS * SWA_HEAD_DIM
SWA_KV_WIDTH = SWA_KV_HEADS * SWA_HEAD_DIM

N_MEM = 256
MEM_HEADS = 4
MEM_HEAD_DIM = 64
MEM_WIDTH = MEM_HEADS * MEM_HEAD_DIM

FFN_DIM = 2816
N_EXPERTS = 8
EXPERT_DIM = 3584

LANES = 128
VMEM_LIMIT = 56 * 1024 * 1024

TOKEN_TILE = 512
MOE_TILE = 512
FFN_COLS = 1408
EXPERT_COLS = 896
SEQ_BATCH = 8


def _params(sem):
    return pltpu.CompilerParams(dimension_semantics=sem, vmem_limit_bytes=VMEM_LIMIT)


def _rms(x, g):
    return x * lax.rsqrt(jnp.mean(x * x, axis=-1, keepdims=True) + RMS_EPS) * g


def _dot(a, b):
    return jnp.dot(a, b, preferred_element_type=F32)


def _div(x, d):
    assert d & (d - 1) == 0
    return lax.shift_right_logical(x, d.bit_length() - 1)


def _dot_nt(a, b):
    return lax.dot_general(a, b, (((1,), (1,)), ((), ())), preferred_element_type=F32)


def _norm_matmul_kernel(x_ref, g_ref, w_ref, o_ref, h_ref):
    @pl.when(pl.program_id(1) == 0)
    def _():
        h_ref[...] = _rms(x_ref[...], g_ref[...]).astype(BF16)

    o_ref[...] = _dot(h_ref[...], w_ref[...])


def _norm_matmul(x, g, w, tn):
    t, k = x.shape
    n = w.shape[1]
    tm = min(TOKEN_TILE, t)
    return pl.pallas_call(
        _norm_matmul_kernel,
        grid=(t // tm, n // tn),
        in_specs=[
            pl.BlockSpec((tm, k), lambda i, j: (i, 0)),
            pl.BlockSpec((1, k), lambda i, j: (0, 0)),
            pl.BlockSpec((k, tn), lambda i, j: (0, j)),
        ],
        out_specs=pl.BlockSpec((tm, tn), lambda i, j: (i, j)),
        out_shape=jax.ShapeDtypeStruct((t, n), F32),
        scratch_shapes=[pltpu.VMEM((tm, k), BF16)],
        compiler_params=_params(("parallel", "arbitrary")),
        name="norm_matmul",
    )(x, g.reshape(1, k), w)


def _out_proj_kernel(x_ref, a_ref, m_ref, w_ref, o_ref):
    na = a_ref.shape[1]
    y = _dot(a_ref[...].astype(BF16), w_ref[0:na, :])
    y = y + _dot(m_ref[...].astype(BF16), w_ref[na:, :])
    o_ref[...] = x_ref[...] + y


def _out_proj(x, a, m, w):
    t, d = x.shape
    tm = min(TOKEN_TILE, t)
    return pl.pallas_call(
        _out_proj_kernel,
        grid=(t // tm,),
        in_specs=[
            pl.BlockSpec((tm, d), lambda i: (i, 0)),
            pl.BlockSpec((tm, a.shape[1]), lambda i: (i, 0)),
            pl.BlockSpec((tm, m.shape[1]), lambda i: (i, 0)),
            pl.BlockSpec(w.shape, lambda i: (0, 0)),
        ],
        out_specs=pl.BlockSpec((tm, d), lambda i: (i, 0)),
        out_shape=jax.ShapeDtypeStruct((t, d), F32),
        compiler_params=_params(("parallel",)),
        name="out_proj",
    )(x, a, m, w)


def _ffn_kernel(x_ref, g_ref, wg_ref, wu_ref, wd_ref, o_ref, h_ref, acc_ref):
    f = pl.program_id(1)

    @pl.when(f == 0)
    def _():
        h_ref[...] = _rms(x_ref[...], g_ref[...]).astype(BF16)
        acc_ref[...] = jnp.zeros_like(acc_ref)

    h = h_ref[...]
    a = jax.nn.silu(_dot(h, wg_ref[...])) * _dot(h, wu_ref[...])
    acc_ref[...] += _dot(a.astype(BF16), wd_ref[...])

    @pl.when(f == pl.num_programs(1) - 1)
    def _():
        o_ref[...] = x_ref[...] + acc_ref[...]


def _ffn(x, g, w_gu, w_down):
    t, d = x.shape
    tm = min(TOKEN_TILE, t)
    tf = FFN_COLS
    nf = FFN_DIM // tf
    return pl.pallas_call(
        _ffn_kernel,
        grid=(t // tm, nf),
        in_specs=[
            pl.BlockSpec((tm, d), lambda i, f: (i, 0)),
            pl.BlockSpec((1, d), lambda i, f: (0, 0)),
            pl.BlockSpec((d, tf), lambda i, f: (0, f)),
            pl.BlockSpec((d, tf), lambda i, f: (0, nf + f)),
            pl.BlockSpec((tf, d), lambda i, f: (f, 0)),
        ],
        out_specs=pl.BlockSpec((tm, d), lambda i, f: (i, 0)),
        out_shape=jax.ShapeDtypeStruct((t, d), F32),
        scratch_shapes=[pltpu.VMEM((tm, d), BF16), pltpu.VMEM((tm, d), F32)],
        compiler_params=_params(("parallel", "arbitrary")),
        name="ffn",
    )(x, g.reshape(1, d), w_gu, w_gu, w_down)


def _router_kernel(x_ref, g_ref, wr_ref, h_ref, r_ref):
    h = _rms(x_ref[...], g_ref[...])
    h_ref[...] = h.astype(BF16)
    logits = jnp.dot(h, wr_ref[...], preferred_element_type=F32, precision=HIGHEST)
    lane = lax.broadcasted_iota(jnp.int32, logits.shape, 1).astype(F32)
    logits = jnp.where(lane < N_EXPERTS, logits, -jnp.inf)
    m1 = jnp.max(logits, axis=-1, keepdims=True)
    i1 = jnp.min(jnp.where(logits == m1, lane, float(LANES)), axis=-1, keepdims=True)
    rest = jnp.where(lane == i1, -jnp.inf, logits)
    m2 = jnp.max(rest, axis=-1, keepdims=True)
    i2 = jnp.min(jnp.where(rest == m2, lane, float(LANES)), axis=-1, keepdims=True)
    e2 = jnp.exp(m2 - m1)
    w1 = 1.0 / (1.0 + e2)
    w2 = e2 / (1.0 + e2)
    r = jnp.where(lane == 0, i1, 0.0)
    r = jnp.where(lane == 1, i2, r)
    r = jnp.where(lane == 2, w1, r)
    r_ref[...] = jnp.where(lane == 3, w2, r)


def _router(x, g, w_router_pad):
    t, d = x.shape
    tm = min(TOKEN_TILE, t)
    return pl.pallas_call(
        _router_kernel,
        grid=(t // tm,),
        in_specs=[
            pl.BlockSpec((tm, d), lambda i: (i, 0)),
            pl.BlockSpec((1, d), lambda i: (0, 0)),
            pl.BlockSpec((d, LANES), lambda i: (0, 0)),
        ],
        out_specs=[
            pl.BlockSpec((tm, d), lambda i: (i, 0)),
            pl.BlockSpec((tm, LANES), lambda i: (i, 0)),
        ],
        out_shape=[
            jax.ShapeDtypeStruct((t, d), BF16),
            jax.ShapeDtypeStruct((t, LANES), F32),
        ],
        compiler_params=_params(("parallel",)),
        name="router",
    )(x, g.reshape(1, d), w_router_pad)


def _moe_kernel(te_ref, nt_ref, x_ref, rw_ref, wg_ref, wu_ref, wd_ref, o_ref, acc_ref):
    i = pl.program_id(0)
    f = pl.program_id(1)

    @pl.when(i < nt_ref[0])
    def _():
        @pl.when(f == 0)
        def _():
            acc_ref[...] = jnp.zeros_like(acc_ref)

        h = x_ref[...]
        a = jax.nn.silu(_dot(h, wg_ref[...])) * _dot(h, wu_ref[...])
        acc_ref[...] += _dot(a.astype(BF16), wd_ref[...])

        @pl.when(f == pl.num_programs(1) - 1)
        def _():
            o_ref[...] = acc_ref[...] * rw_ref[...]


def _moe_ffn(xs, roww, tile_expert, n_tiles, w_gu, w_down):
    p, d = xs.shape
    tm = MOE_TILE
    tf = EXPERT_COLS
    nf = EXPERT_DIM // tf

    def row(i, f, te, nt):
        return (jnp.minimum(i, nt[0] - 1), 0)

    def col(i, f, nt):
        return jnp.where(i < nt[0], f, nf - 1)

    grid_spec = pltpu.PrefetchScalarGridSpec(
        num_scalar_prefetch=2,
        grid=(p // tm, nf),
        in_specs=[
            pl.BlockSpec((tm, d), row),
            pl.BlockSpec((tm, 1), row),
            pl.BlockSpec((None, d, tf), lambda i, f, te, nt: (te[i], 0, col(i, f, nt))),
            pl.BlockSpec((None, d, tf), lambda i, f, te, nt: (te[i], 0, nf + col(i, f, nt))),
            pl.BlockSpec((None, tf, d), lambda i, f, te, nt: (te[i], col(i, f, nt), 0)),
        ],
        out_specs=pl.BlockSpec((tm, d), row),
        scratch_shapes=[pltpu.VMEM((tm, d), F32)],
    )
    return pl.pallas_call(
        _moe_kernel,
        grid_spec=grid_spec,
        out_shape=jax.ShapeDtypeStruct((p, d), F32),
        compiler_params=_params(("arbitrary", "arbitrary")),
        name="moe_ffn",
    )(tile_expert, n_tiles, xs, roww, w_gu, w_gu, w_down)


def _combine_norm_kernel(x_ref, a_ref, b_ref, g_ref, o_ref):
    o_ref[...] = _rms(x_ref[...] + (a_ref[...] + b_ref[...]), g_ref[...])


def _combine_norm(x, a, b, g):
    t, d = x.shape
    tm = min(TOKEN_TILE, t)
    spec = pl.BlockSpec((tm, d), lambda i: (i, 0))
    return pl.pallas_call(
        _combine_norm_kernel,
        grid=(t // tm,),
        in_specs=[spec, spec, spec, pl.BlockSpec((1, d), lambda i: (0, 0))],
        out_specs=spec,
        out_shape=jax.ShapeDtypeStruct((t, d), F32),
        compiler_params=_params(("parallel",)),
        name="combine_norm",
    )(x, a, b, g.reshape(1, d))


def _head_softmax_pv(s, v):
    m = jnp.max(s, axis=-1, keepdims=True)
    p = jnp.exp(s - m)
    return _dot(p.astype(BF16), v) / jnp.sum(p, axis=-1, keepdims=True)


def _mem_attn_shared_kernel(q_ref, kv_ref, o_ref):
    q = q_ref[...]
    k = kv_ref[0, :, 0:MEM_WIDTH].astype(BF16)
    v = kv_ref[0, :, MEM_WIDTH:].astype(BF16)
    col = lax.broadcasted_iota(jnp.int32, (1, MEM_WIDTH), 1)
    acc = jnp.zeros(q.shape, F32)
    for h in range(MEM_HEADS):
        in_head = (col >= h * MEM_HEAD_DIM) & (col < (h + 1) * MEM_HEAD_DIM)
        qh = jnp.where(in_head, q, 0.0).astype(BF16)
        s = _dot_nt(qh, k) * MEM_HEAD_DIM ** -0.5
        acc = acc + jnp.where(in_head, _head_softmax_pv(s, v), 0.0)
    o_ref[...] = acc


def _mem_attn_shared(proj, q_col, mem_kv, seq_len):
    t = proj.shape[0]
    tq = min(TOKEN_TILE, seq_len)
    per_seq = seq_len // tq
    return pl.pallas_call(
        _mem_attn_shared_kernel,
        grid=(t // tq,),
        in_specs=[
            pl.BlockSpec((tq, MEM_WIDTH), lambda i: (i, q_col)),
            pl.BlockSpec((1, N_MEM, 2 * MEM_WIDTH), lambda i: (i // per_seq, 0, 0)),
        ],
        out_specs=pl.BlockSpec((tq, MEM_WIDTH), lambda i: (i, 0)),
        out_shape=jax.ShapeDtypeStruct((t, MEM_WIDTH), F32),
        compiler_params=_params(("parallel",)),
        name="mem_attn_shared",
    )(proj, mem_kv)


def _mem_attn_decode_kernel(q_ref, k_ref, v_ref, o_ref, *, seq_len):
    rows = MEM_HEADS * seq_len
    row_head = _div(lax.broadcasted_iota(jnp.int32, (rows, MEM_WIDTH), 0), seq_len)
    col_head = _div(lax.broadcasted_iota(jnp.int32, (rows, MEM_WIDTH), 1), MEM_HEAD_DIM)
    diag = row_head == col_head
    out_head = _div(lax.broadcasted_iota(jnp.int32, (seq_len, MEM_WIDTH), 1), MEM_HEAD_DIM)
    for b in range(k_ref.shape[0]):
        q = q_ref[b * seq_len:(b + 1) * seq_len, :]
        qd = jnp.where(diag, jnp.concatenate([q] * MEM_HEADS, axis=0), 0.0).astype(BF16)
        s = _dot_nt(qd, k_ref[b].astype(BF16)) * MEM_HEAD_DIM ** -0.5
        o = _head_softmax_pv(s, v_ref[b].astype(BF16))
        acc = jnp.zeros((seq_len, MEM_WIDTH), F32)
        for h in range(MEM_HEADS):
            acc = acc + jnp.where(out_head == h, o[h * seq_len:(h + 1) * seq_len], 0.0)
        o_ref[b * seq_len:(b + 1) * seq_len, :] = acc


def _mem_attn_decode(proj, q_col, mem_k, mem_v, seq_len):
    t = proj.shape[0]
    nb = mem_k.shape[0]
    bb = min(SEQ_BATCH, nb)
    return pl.pallas_call(
        functools.partial(_mem_attn_decode_kernel, seq_len=seq_len),
        grid=(nb // bb,),
        in_specs=[
            pl.BlockSpec((bb * seq_len, MEM_WIDTH), lambda i: (i, q_col)),
            pl.BlockSpec((bb, N_MEM, MEM_WIDTH), lambda i: (i, 0, 0)),
            pl.BlockSpec((bb, N_MEM, MEM_WIDTH), lambda i: (i, 0, 0)),
        ],
        out_specs=pl.BlockSpec((bb * seq_len, MEM_WIDTH), lambda i: (i, 0)),
        out_shape=jax.ShapeDtypeStruct((t, MEM_WIDTH), F32),
        compiler_params=_params(("parallel",)),
        name="mem_attn_decode",
    )(proj, mem_k, mem_v)


def _sink_softmax_pv(s, sink, v):
    m = jnp.maximum(jnp.max(s, axis=-1, keepdims=True), sink)
    p = jnp.exp(s - m)
    denom = jnp.sum(p, axis=-1, keepdims=True) + jnp.exp(sink - m)
    return _dot(p.astype(BF16), v) / denom


def _swa_prefill_kernel(slope_ref, sink_ref, q_ref, kp_ref, kc_ref, vp_ref, vc_ref, o_ref, *, blocks_per_seq):
    w = SWA_WINDOW
    first = (pl.program_id(0) % blocks_per_seq) == 0
    q = q_ref[...]
    k = jnp.concatenate([kp_ref[...], kc_ref[...]], axis=0)
    v = jnp.concatenate([vp_ref[...], vc_ref[...]], axis=0)
    qi = lax.broadcasted_iota(jnp.int32, (w, 2 * w), 0)
    kj = lax.broadcasted_iota(jnp.int32, (w, 2 * w), 1)
    dist = qi + w - kj
    valid = (dist >= 0) & (dist < w) & ((kj >= w) | jnp.logical_not(first))
    distf = dist.astype(F32)
    outs = []
    for kh in range(SWA_KV_HEADS):
        kk = k[:, kh * SWA_HEAD_DIM:(kh + 1) * SWA_HEAD_DIM].astype(BF16)
        vv = v[:, kh * SWA_HEAD_DIM:(kh + 1) * SWA_HEAD_DIM].astype(BF16)
        heads = [kh * SWA_GROUP + g for g in range(SWA_GROUP)]
        qg = jnp.concatenate([q[:, h * SWA_HEAD_DIM:(h + 1) * SWA_HEAD_DIM] for h in heads], axis=0).astype(BF16)
        s3 = _dot_nt(qg, kk) * SWA_HEAD_DIM ** -0.5
        for g, h in enumerate(heads):
            s = s3[g * w:(g + 1) * w] - slope_ref[h] * distf
            s = jnp.where(valid, s, NEG_INF)
            outs.append(_sink_softmax_pv(s, sink_ref[h], vv))
    o_ref[...] = jnp.concatenate(outs, axis=1)


def _swa_prefill(proj, slopes, sinks, seq_len):
    t = proj.shape[0]
    w = SWA_WINDOW
    per_seq = seq_len // w
    kcol = SWA_Q_WIDTH // SWA_KV_WIDTH
    vcol = kcol + 1

    def prev(i):
        return jnp.where(i % per_seq == 0, i, i - 1)

    smem = pl.BlockSpec(memory_space=pltpu.SMEM)
    return pl.pallas_call(
        functools.partial(_swa_prefill_kernel, blocks_per_seq=per_seq),
        grid=(t // w,),
        in_specs=[
            smem,
            smem,
            pl.BlockSpec((w, SWA_Q_WIDTH), lambda i: (i, 0)),
            pl.BlockSpec((w, SWA_KV_WIDTH), lambda i: (prev(i), kcol)),
            pl.BlockSpec((w, SWA_KV_WIDTH), lambda i: (i, kcol)),
            pl.BlockSpec((w, SWA_KV_WIDTH), lambda i: (prev(i), vcol)),
            pl.BlockSpec((w, SWA_KV_WIDTH), lambda i: (i, vcol)),
        ],
        out_specs=pl.BlockSpec((w, SWA_Q_WIDTH), lambda i: (i, 0)),
        out_shape=jax.ShapeDtypeStruct((t, SWA_Q_WIDTH), F32),
        compiler_params=_params(("parallel",)),
        name="swa_prefill",
    )(slopes, sinks, proj, proj, proj, proj, proj)


def _swa_decode_kernel(q_ref, kn_ref, vn_ref, kc_ref, vc_ref, rep_ref, slope_ref, sink_ref, o_ref, *, seq_len):
    w = SWA_WINDOW
    rows = SWA_HEADS * seq_len
    span = 2 * w
    row = lax.broadcasted_iota(jnp.int32, (rows, SWA_Q_WIDTH), 0)
    col = lax.broadcasted_iota(jnp.int32, (rows, SWA_Q_WIDTH), 1)
    diag = _div(row, seq_len) == _div(col, SWA_HEAD_DIM)
    ql = lax.broadcasted_iota(jnp.int32, (rows, span), 0) & (seq_len - 1)
    kj = lax.broadcasted_iota(jnp.int32, (rows, span), 1)
    dist = w + ql - kj
    valid = (dist >= 0) & (dist < w)
    bias = slope_ref[...] * dist.astype(F32)
    sink = sink_ref[...]
    out_head = _div(lax.broadcasted_iota(jnp.int32, (seq_len, SWA_Q_WIDTH), 1), SWA_HEAD_DIM)
    rep = rep_ref[...]
    tail = jnp.zeros((w - seq_len, SWA_KV_WIDTH), F32)
    for b in range(kc_ref.shape[0]):
        rs = slice(b * seq_len, (b + 1) * seq_len)
        k = jnp.concatenate([kc_ref[b], kn_ref[rs, :], tail], axis=0).astype(BF16)
        v = jnp.concatenate([vc_ref[b], vn_ref[rs, :], tail], axis=0).astype(BF16)
        k_rep = _dot(k, rep).astype(BF16)
        v_rep = _dot(v, rep).astype(BF16)
        qd = jnp.where(diag, jnp.concatenate([q_ref[rs, :]] * SWA_HEADS, axis=0), 0.0).astype(BF16)
        s = _dot_nt(qd, k_rep) * SWA_HEAD_DIM ** -0.5 - bias
        s = jnp.where(valid, s, NEG_INF)
        o = _sink_softmax_pv(s, sink, v_rep)
        acc = jnp.zeros((seq_len, SWA_Q_WIDTH), F32)
        for h in range(SWA_HEADS):
            acc = acc + jnp.where(out_head == h, o[h * seq_len:(h + 1) * seq_len], 0.0)
        o_ref[rs, :] = acc


def _swa_decode(proj, cache_k, cache_v, slopes, sinks, seq_len):
    t = proj.shape[0]
    nb = cache_k.shape[0]
    bb = min(SEQ_BATCH, nb)
    w = SWA_WINDOW
    kcol = SWA_Q_WIDTH // SWA_KV_WIDTH
    src = (jnp.arange(SWA_Q_WIDTH) // SWA_HEAD_DIM // SWA_GROUP) * SWA_HEAD_DIM + jnp.arange(SWA_Q_WIDTH) % SWA_HEAD_DIM
    rep = (jnp.arange(SWA_KV_WIDTH)[:, None] == src[None, :]).astype(BF16)
    slope_rows = jnp.repeat(slopes, seq_len).reshape(-1, 1)
    sink_rows = jnp.repeat(sinks, seq_len).reshape(-1, 1)
    rows = SWA_HEADS * seq_len
    return pl.pallas_call(
        functools.partial(_swa_decode_kernel, seq_len=seq_len),
        grid=(nb // bb,),
        in_specs=[
            pl.BlockSpec((bb * seq_len, SWA_Q_WIDTH), lambda i: (i, 0)),
            pl.BlockSpec((bb * seq_len, SWA_KV_WIDTH), lambda i: (i, kcol)),
            pl.BlockSpec((bb * seq_len, SWA_KV_WIDTH), lambda i: (i, kcol + 1)),
            pl.BlockSpec((bb, w, SWA_KV_WIDTH), lambda i: (i, 0, 0)),
            pl.BlockSpec((bb, w, SWA_KV_WIDTH), lambda i: (i, 0, 0)),
            pl.BlockSpec((SWA_KV_WIDTH, SWA_Q_WIDTH), lambda i: (0, 0)),
            pl.BlockSpec((rows, 1), lambda i: (0, 0)),
            pl.BlockSpec((rows, 1), lambda i: (0, 0)),
        ],
        out_specs=pl.BlockSpec((bb * seq_len, SWA_Q_WIDTH), lambda i: (i, 0)),
        out_shape=jax.ShapeDtypeStruct((t, SWA_Q_WIDTH), F32),
        compiler_params=_params(("parallel",)),
        name="swa_decode",
    )(proj, proj, proj, cache_k, cache_v, rep, slope_rows, sink_rows)


def _gdn_kernel(*refs, nblk, spb):
    r = GDN_ROWS
    c = r // spb
    qkv_refs = refs[0:nblk]
    z_refs = refs[nblk:2 * nblk]
    ba_refs = refs[2 * nblk:3 * nblk]
    conv0_ref, s0_ref, cw_ref, alog_ref, dtb_ref, gn_ref, o_ref, sfin_ref, s_scr, fbuf = refs[3 * nblk:]
    step = pl.program_id(1)
    nseq = nblk * spb
    pad = 8
    hist = GDN_CONV_K - 1

    @pl.when(step == 0)
    def _():
        s_scr[...] = s0_ref[...]
        for s in range(nseq):
            fbuf[s, pad - hist:pad, :] = conv0_ref[s]

    ri = lax.broadcasted_iota(jnp.int32, (r, r), 0)
    ci = lax.broadcasted_iota(jnp.int32, (r, r), 1)
    same = _div(ri, c) == _div(ci, c)
    tri = same & (ri >= ci)
    strict = same & (ri > ci)
    eye = (ri == ci).astype(F32)
    cum_mat = jnp.concatenate([tri.astype(F32), same.astype(F32)], axis=0)
    row_seq = _div(lax.broadcasted_iota(jnp.int32, (r, 1), 0), c)
    zeros_rr = jnp.zeros((r, LANES), F32)
    cw = cw_ref[...]
    neg_a = -jnp.exp(alog_ref[...])
    gn = gn_ref[...]

    for n in range(nblk):
        u = qkv_refs[n][...]
        pieces = []
        for s in range(spb):
            idx = n * spb + s
            fbuf[idx, pad:pad + c, :] = u[s * c:(s + 1) * c]
            acc = fbuf[idx, pad - hist:pad - hist + c, :] * cw[0:1]
            for j in range(1, GDN_CONV_K):
                acc = acc + fbuf[idx, pad - hist + j:pad - hist + j + c, :] * cw[j:j + 1]
            fbuf[idx, pad - hist:pad, :] = fbuf[idx, pad + c - hist:pad + c, :]
            pieces.append(acc)
        conv = pieces[0] if spb == 1 else jnp.concatenate(pieces, axis=0)
        qkv = jax.nn.silu(conv)

        ba = ba_refs[n][...]
        beta_all = jax.nn.sigmoid(ba)
        xg = ba + dtb_ref[...]
        g_all = neg_a * (jnp.maximum(xg, 0.0) + jnp.log(1.0 + jnp.exp(-jnp.abs(xg))))
        gsum = jnp.dot(cum_mat, g_all, preferred_element_type=F32, precision=HIGHEST)
        gcum = gsum[0:r]
        gtot = gsum[r:2 * r]
        gcum_t = jnp.transpose(jnp.concatenate([gcum, zeros_rr], axis=0))

        z = z_refs[n][...]
        for h in range(GDN_HEADS):
            lo = h * GDN_DK
            q = qkv[:, lo:lo + GDN_DK]
            k = qkv[:, GDN_QK_WIDTH + lo:GDN_QK_WIDTH + lo + GDN_DK]
            v = qkv[:, 2 * GDN_QK_WIDTH + lo:2 * GDN_QK_WIDTH + lo + GDN_DV]
            q = q * lax.rsqrt(jnp.sum(q * q, axis=-1, keepdims=True) + L2_EPS) * GDN_DK ** -0.5
            k = k * lax.rsqrt(jnp.sum(k * k, axis=-1, keepdims=True) + L2_EPS)
            beta = beta_all[:, h:h + 1]
            gcol = gcum[:, GDN_HEADS + h:GDN_HEADS + h + 1]
            grow = gcum_t[GDN_HEADS + h:GDN_HEADS + h + 1, 0:r]
            glast = gtot[:, GDN_HEADS + h:GDN_HEADS + h + 1]
            decay = jnp.where(tri, jnp.exp(jnp.where(tri, gcol - grow, 0.0)), 0.0)
            eg = jnp.exp(gcol)
            kb = k * beta
            kq = _dot_nt(jnp.concatenate([kb, q], axis=0).astype(BF16), k.astype(BF16))
            lmat = jnp.where(strict, kq[0:r] * decay, 0.0)
            qk = kq[r:2 * r] * decay
            inv = eye - lmat
            power = lmat
            span = 2
            while span < c:
                power = _dot(power.astype(BF16), power.astype(BF16))
                inv = _dot(inv.astype(BF16), (eye + power).astype(BF16))
                span *= 2
            tq = jnp.concatenate([inv, _dot(qk.astype(BF16), inv.astype(BF16))], axis=0).astype(BF16)
            vb = v * beta
            kq_dec = jnp.concatenate([kb * eg, q * eg], axis=0).astype(BF16)
            kdec_t = jnp.transpose(jnp.concatenate([k * jnp.exp(glast - gcol), zeros_rr], axis=0))[:, 0:r]
            kdec_t = kdec_t.astype(BF16)

            if spb == 1:
                state = s_scr[n, h]
                ks_qs = _dot(kq_dec, state.astype(BF16))
                resid = vb - ks_qs[0:r]
                qs = ks_qs[r:2 * r]
            else:
                ks_parts, qs_parts = [], []
                for s in range(spb):
                    state = s_scr[n * spb + s, h].astype(BF16)
                    rows = jnp.concatenate([kq_dec[s * c:(s + 1) * c], kq_dec[r + s * c:r + (s + 1) * c]], axis=0)
                    both = _dot(rows, state)
                    ks_parts.append(both[0:c])
                    qs_parts.append(both[c:2 * c])
                resid = vb - jnp.concatenate(ks_parts, axis=0)
                qs = jnp.concatenate(qs_parts, axis=0)

            vo = _dot(tq, resid.astype(BF16))
            v_new = vo[0:r]
            o = qs + vo[r:2 * r]
            for s in range(spb):
                idx = n * spb + s
                vs = v_new if spb == 1 else jnp.where(row_seq == s, v_new, 0.0)
                carry = jnp.exp(glast[s * c:s * c + 1, :])
                s_scr[idx, h] = s_scr[idx, h] * carry + _dot(kdec_t, vs.astype(BF16))

            on = _rms(o, gn)
            zh = z[:, h * GDN_DV:(h + 1) * GDN_DV]
            o_ref[n, :, h * GDN_DV:(h + 1) * GDN_DV] = on * jax.nn.silu(zh)

    @pl.when(step == pl.num_programs(1) - 1)
    def _():
        sfin_ref[...] = s_scr[...]


def _gdn(proj, conv_buf, s0, conv_w, a_log, dt_bias, norm_g, *, nblk, spb, steps):
    t = proj.shape[0]
    r = GDN_ROWS
    groups = t // (r * nblk * steps)
    nseq = nblk * spb
    c = r // spb
    z_col = GDN_CONV_DIM // GDN_V_WIDTH
    ba_col = (GDN_CONV_DIM + GDN_V_WIDTH + MEM_WIDTH) // LANES

    def rows(n, col):
        return lambda g, l: ((g * nblk + n) * steps + l, col)

    lane6 = jnp.zeros((1, LANES), F32)
    alog = lane6.at[0, GDN_HEADS:2 * GDN_HEADS].set(a_log)
    dtb = lane6.at[0, GDN_HEADS:2 * GDN_HEADS].set(dt_bias)
    const = lambda g, l: (0, 0)
    in_specs = (
        [pl.BlockSpec((r, GDN_CONV_DIM), rows(n, 0)) for n in range(nblk)]
        + [pl.BlockSpec((r, GDN_V_WIDTH), rows(n, z_col)) for n in range(nblk)]
        + [pl.BlockSpec((r, LANES), rows(n, ba_col)) for n in range(nblk)]
        + [
            pl.BlockSpec((nseq, GDN_CONV_K - 1, GDN_CONV_DIM), lambda g, l: (g, 0, 0)),
            pl.BlockSpec((nseq, GDN_HEADS, GDN_DK, GDN_DV), lambda g, l: (g, 0, 0, 0)),
            pl.BlockSpec((GDN_CONV_K, GDN_CONV_DIM), const),
            pl.BlockSpec((1, LANES), const),
            pl.BlockSpec((1, LANES), const),
            pl.BlockSpec((1, GDN_DV), const),
        ]
    )
    out, s_fin = pl.pallas_call(
        functools.partial(_gdn_kernel, nblk=nblk, spb=spb),
        grid=(groups, steps),
        in_specs=in_specs,
        out_specs=[
            pl.BlockSpec((nblk, r, GDN_V_WIDTH), lambda g, l: (0, g * steps + l, 0)),
            pl.BlockSpec((nseq, GDN_HEADS, GDN_DK, GDN_DV), lambda g, l: (g, 0, 0, 0)),
        ],
        out_shape=[
            jax.ShapeDtypeStruct((nblk, groups * steps * r, GDN_V_WIDTH), F32),
            jax.ShapeDtypeStruct((groups * nseq, GDN_HEADS, GDN_DK, GDN_DV), F32),
        ],
        scratch_shapes=[
            pltpu.VMEM((nseq, GDN_HEADS, GDN_DK, GDN_DV), F32),
            pltpu.VMEM((nseq, 8 + c, GDN_CONV_DIM), F32),
        ],
        compiler_params=_params(("parallel", "arbitrary")),
        name="gdn",
    )(*([proj] * (3 * nblk)), conv_buf, s0, conv_w, alog, dtb, norm_g.reshape(1, GDN_DV))
    return out.reshape(t, GDN_V_WIDTH), s_fin


def _dispatch(route, tm, p):
    t = route.shape[0]
    experts = route[:, 0:2].astype(jnp.int32).reshape(-1)
    weights = route[:, 2:4].reshape(-1)
    onehot = (experts[:, None] == jnp.arange(N_EXPERTS)[None, :]).astype(jnp.int32)
    csum = jnp.cumsum(onehot, axis=0)
    rank = jnp.take_along_axis(csum, experts[:, None], axis=1)[:, 0] - 1
    tiles = (csum[-1] + tm - 1) // tm
    tile_end = jnp.cumsum(tiles)
    start = (tile_end - tiles) * tm
    dest = start[experts] + rank
    n_tiles = tile_end[-1:].astype(jnp.int32)
    tile_expert = jnp.searchsorted(tile_end, jnp.arange(p // tm), side="right")
    tile_expert = jnp.minimum(tile_expert, N_EXPERTS - 1).astype(jnp.int32)
    src = jnp.zeros((p,), jnp.int32).at[dest].set(jnp.arange(2 * t, dtype=jnp.int32) // 2)
    roww = jnp.zeros((p,), F32).at[dest].set(weights)
    return dest.reshape(t, 2), src, roww.reshape(p, 1), tile_expert, n_tiles


def kernel(x_prompt, x_sample, state_gdn_conv, state_gdn_ssm, cache_swa_k, cache_swa_v, cache_mem_k, cache_mem_v, mem_prompt, attn_norm, ffn_norm, mem_norm, final_norm, w_in_gdn, gdn_conv_w, gdn_a_log, gdn_dt_bias, gdn_norm, w_out_gdn, w_in_swa, swa_sinks, w_out_swa, w_mem_kv, w_ffn_gu, w_ffn_down, w_router, w_exp_gu, w_exp_down):
    bp, lp, d = x_prompt.shape
    bs, ls, _ = x_sample.shape
    tp, ts = bp * lp, bs * ls
    xp = x_prompt.reshape(tp, d)
    xs = x_sample.reshape(ts, d)

    mem = mem_prompt.reshape(bp * N_MEM, d)
    mem_kv = [
        _norm_matmul(mem, mem_norm[i], w_mem_kv[i].astype(BF16), 2 * MEM_WIDTH).reshape(bp, N_MEM, 2 * MEM_WIDTH)
        for i in range(2)
    ]
    new_mem_k = jnp.stack([kv[..., :MEM_WIDTH].reshape(bp, N_MEM, MEM_HEADS, MEM_HEAD_DIM) for kv in mem_kv])
    new_mem_v = jnp.stack([kv[..., MEM_WIDTH:].reshape(bp, N_MEM, MEM_HEADS, MEM_HEAD_DIM) for kv in mem_kv])

    w_in = w_in_gdn[0]
    o_z = GDN_CONV_DIM + GDN_V_WIDTH
    o_mem = o_z + 2 * GDN_HEADS
    w0 = jnp.concatenate(
        [w_in[:, :o_z], w_in[:, o_mem:], w_in[:, o_z:o_mem], jnp.zeros((d, LANES - 2 * GDN_HEADS), F32)], axis=1
    ).astype(BF16)
    mq_col0 = o_z // MEM_WIDTH
    w_out0 = w_out_gdn[0].astype(BF16)
    w_gu0 = w_ffn_gu[0].astype(BF16)
    w_dn0 = w_ffn_down[0].astype(BF16)

    proj_p = _norm_matmul(xp, attn_norm[0], w0, w0.shape[1] // 3)
    proj_s = _norm_matmul(xs, attn_norm[0], w0, w0.shape[1] // 3)

    zero_conv = jnp.zeros((bp, GDN_CONV_K - 1, GDN_CONV_DIM), F32)
    zero_state = jnp.zeros((bp, GDN_HEADS, GDN_DK, GDN_DV), F32)
    gdn_p, ssm_p = _gdn(proj_p, zero_conv, zero_state, gdn_conv_w[0], gdn_a_log[0], gdn_dt_bias[0], gdn_norm[0],
                        nblk=bp, spb=1, steps=lp // GDN_ROWS)
    gdn_s, ssm_s = _gdn(proj_s, state_gdn_conv[0], state_gdn_ssm[0], gdn_conv_w[0], gdn_a_log[0], gdn_dt_bias[0],
                        gdn_norm[0], nblk=1, spb=GDN_ROWS // ls, steps=1)
    hist = GDN_CONV_K - 1
    conv_p = proj_p.reshape(bp, lp, -1)[:, lp - hist:, :GDN_CONV_DIM]
    conv_s = proj_s.reshape(bs, ls, -1)[:, ls - hist:, :GDN_CONV_DIM]

    memo_p = _mem_attn_shared(proj_p, mq_col0, mem_kv[0], lp)
    memo_s = _mem_attn_decode(proj_s, mq_col0, cache_mem_k[0].reshape(bs, N_MEM, MEM_WIDTH),
                              cache_mem_v[0].reshape(bs, N_MEM, MEM_WIDTH), ls)
    xp = _out_proj(xp, gdn_p, memo_p, w_out0)
    xs = _out_proj(xs, gdn_s, memo_s, w_out0)
    xp = _ffn(xp, ffn_norm[0], w_gu0, w_dn0)
    xs = _ffn(xs, ffn_norm[0], w_gu0, w_dn0)

    w1 = w_in_swa[0].astype(BF16)
    mq_col1 = (SWA_Q_WIDTH + 2 * SWA_KV_WIDTH) // MEM_WIDTH
    w_out1 = w_out_swa[0].astype(BF16)
    slopes = 2.0 ** (-8.0 * jnp.arange(1, SWA_HEADS + 1, dtype=F32) / SWA_HEADS)
    sinks = swa_sinks[0].astype(F32)

    proj_p = _norm_matmul(xp, attn_norm[1], w1, w1.shape[1] // 2)
    proj_s = _norm_matmul(xs, attn_norm[1], w1, w1.shape[1] // 2)
    swa_p = _swa_prefill(proj_p, slopes, sinks, lp)
    cache_k = cache_swa_k[0].reshape(bs, SWA_WINDOW, SWA_KV_WIDTH)
    cache_v = cache_swa_v[0].reshape(bs, SWA_WINDOW, SWA_KV_WIDTH)
    swa_s = _swa_decode(proj_s, cache_k, cache_v, slopes, sinks, ls)

    k0, v0 = SWA_Q_WIDTH, SWA_Q_WIDTH + SWA_KV_WIDTH
    pp = proj_p.reshape(bp, lp, -1)
    ps = proj_s.reshape(bs, ls, -1)
    kv_shape = (SWA_KV_HEADS, SWA_HEAD_DIM)
    swk_p = pp[:, lp - SWA_WINDOW:, k0:k0 + SWA_KV_WIDTH].reshape(bp, SWA_WINDOW, *kv_shape)
    swv_p = pp[:, lp - SWA_WINDOW:, v0:v0 + SWA_KV_WIDTH].reshape(bp, SWA_WINDOW, *kv_shape)
    swk_s = jnp.concatenate([cache_k[:, ls:], ps[:, :, k0:k0 + SWA_KV_WIDTH]], axis=1).reshape(bs, SWA_WINDOW, *kv_shape)
    swv_s = jnp.concatenate([cache_v[:, ls:], ps[:, :, v0:v0 + SWA_KV_WIDTH]], axis=1).reshape(bs, SWA_WINDOW, *kv_shape)

    memo_p = _mem_attn_shared(proj_p, mq_col1, mem_kv[1], lp)
    memo_s = _mem_attn_decode(proj_s, mq_col1, cache_mem_k[1].reshape(bs, N_MEM, MEM_WIDTH),
                              cache_mem_v[1].reshape(bs, N_MEM, MEM_WIDTH), ls)
    xp = _out_proj(xp, swa_p, memo_p, w_out1)
    xs = _out_proj(xs, swa_s, memo_s, w_out1)

    w_r = jnp.concatenate([w_router[0], jnp.zeros((d, LANES - N_EXPERTS), F32)], axis=1)
    hp, route_p = _router(xp, ffn_norm[1], w_r)
    hs, route_s = _router(xs, ffn_norm[1], w_r)
    t_all = tp + ts
    tm = MOE_TILE
    p_rows = -(-(2 * t_all + N_EXPERTS * (tm - 1)) // tm) * tm
    dest, src, roww, tile_expert, n_tiles = _dispatch(jnp.concatenate([route_p, route_s], axis=0), tm, p_rows)
    rows_in = jnp.concatenate([hp, hs], axis=0)[src]
    rows_out = _moe_ffn(rows_in, roww, tile_expert, n_tiles, w_exp_gu[0].astype(BF16), w_exp_down[0].astype(BF16))
    first = rows_out[dest[:, 0]]
    second = rows_out[dest[:, 1]]
    y_p = _combine_norm(xp, first[:tp], second[:tp], final_norm)
    y_s = _combine_norm(xs, first[tp:], second[tp:], final_norm)

    return (
        y_p.reshape(bp, lp, d),
        y_s.reshape(bs, ls, d),
        conv_p[None],
        ssm_p[None],
        swk_p[None],
        swv_p[None],
        new_mem_k,
        new_mem_v,
        conv_s[None],
        ssm_s[None],
        swk_s[None],
        swv_s[None],
    )
```

```python
import functools

import jax
import jax.numpy as jnp
from jax import lax
from jax.experimental import pallas as pl
from jax.experimental.pallas import tpu as pltpu

F32 = jnp.float32
BF16 = jnp.bfloat16
HIGHEST = lax.Precision.HIGHEST

D_MODEL = 1024
RMS_EPS = 1e-6
L2_EPS = 1e-6
NEG_INF = -1e30

GDN_HEADS = 6
GDN_DK = 128
GDN_DV = 128
GDN_CONV_K = 4
GDN_QK_WIDTH = GDN_HEADS * GDN_DK
GDN_V_WIDTH = GDN_HEADS * GDN_DV
GDN_CONV_DIM = 2 * GDN_QK_WIDTH + GDN_V_WIDTH
GDN_ROWS = 64

SWA_HEADS = 12
SWA_KV_HEADS = 4
SWA_HEAD_DIM = 64
SWA_GROUP = SWA_HEADS // SWA_KV_HEADS
SWA_WINDOW = 128
SWA_Q_WIDTH = SWA_HEADS * SWA_HEAD_DIM
SWA_KV_WIDTH = SWA_KV_HEADS * SWA_HEAD_DIM

N_MEM = 256
MEM_HEADS = 4
MEM_HEAD_DIM = 64
MEM_WIDTH = MEM_HEADS * MEM_HEAD_DIM

FFN_DIM = 2816
N_EXPERTS = 8
EXPERT_DIM = 3584

LANES = 128
VMEM_LIMIT = 56 * 1024 * 1024

TOKEN_TILE = 512
MOE_TILE = 512
FFN_COLS = 1408
EXPERT_COLS = 896
SEQ_BATCH = 8


def _params(sem):
    return pltpu.CompilerParams(dimension_semantics=sem, vmem_limit_bytes=VMEM_LIMIT)


def _rms(x, g):
    return x * lax.rsqrt(jnp.mean(x * x, axis=-1, keepdims=True) + RMS_EPS) * g


def _dot(a, b):
    return jnp.dot(a, b, preferred_element_type=F32)


def _div(x, d):
    assert d & (d - 1) == 0
    return lax.shift_right_logical(x, d.bit_length() - 1)


def _dot_nt(a, b):
    return lax.dot_general(a, b, (((1,), (1,)), ((), ())), preferred_element_type=F32)


def _norm_matmul_kernel(x_ref, g_ref, w_ref, o_ref, h_ref):
    @pl.when(pl.program_id(1) == 0)
    def _():
        h_ref[...] = _rms(x_ref[...], g_ref[...]).astype(BF16)

    o_ref[...] = _dot(h_ref[...], w_ref[...])


def _norm_matmul(x, g, w, tn):
    t, k = x.shape
    n = w.shape[1]
    tm = min(TOKEN_TILE, t)
    return pl.pallas_call(
        _norm_matmul_kernel,
        grid=(t // tm, n // tn),
        in_specs=[
            pl.BlockSpec((tm, k), lambda i, j: (i, 0)),
            pl.BlockSpec((1, k), lambda i, j: (0, 0)),
            pl.BlockSpec((k, tn), lambda i, j: (0, j)),
        ],
        out_specs=pl.BlockSpec((tm, tn), lambda i, j: (i, j)),
        out_shape=jax.ShapeDtypeStruct((t, n), F32),
        scratch_shapes=[pltpu.VMEM((tm, k), BF16)],
        compiler_params=_params(("parallel", "arbitrary")),
        name="norm_matmul",
    )(x, g.reshape(1, k), w)


def _out_proj_block(x_ref, a_ref, m_ref, w_ref, o_ref):
    na = a_ref.shape[1]
    y = _dot(a_ref[...].astype(BF16), w_ref[0:na, :])
    y = y + _dot(m_ref[...].astype(BF16), w_ref[na:, :])
    o_ref[...] = x_ref[...] + y


def _out_proj_kernel(x_ref, a_ref, m_ref, w_ref, o_ref):
    _out_proj_block(x_ref, a_ref, m_ref, w_ref, o_ref)


def _out_proj(x, a, m, w):
    t, d = x.shape
    tm = min(TOKEN_TILE, t)
    return pl.pallas_call(
        _out_proj_kernel,
        grid=(t // tm,),
        in_specs=[
            pl.BlockSpec((tm, d), lambda i: (i, 0)),
            pl.BlockSpec((tm, a.shape[1]), lambda i: (i, 0)),
            pl.BlockSpec((tm, m.shape[1]), lambda i: (i, 0)),
            pl.BlockSpec(w.shape, lambda i: (0, 0)),
        ],
        out_specs=pl.BlockSpec((tm, d), lambda i: (i, 0)),
        out_shape=jax.ShapeDtypeStruct((t, d), F32),
        compiler_params=_params(("parallel",)),
        name="out_proj",
    )(x, a, m, w)


def _out_proj_joined_kernel(x0_ref, a0_ref, m0_ref, x1_ref, a1_ref, m1_ref, w_ref, o_ref, *, n0):
    i = pl.program_id(0)

    @pl.when(i < n0)
    def _():
        _out_proj_block(x0_ref, a0_ref, m0_ref, w_ref, o_ref)

    @pl.when(i >= n0)
    def _():
        _out_proj_block(x1_ref, a1_ref, m1_ref, w_ref, o_ref)


def _out_proj_joined(first, second, w):
    (x0, a0, m0), (x1, a1, m1) = first, second
    d = x0.shape[1]
    tm = min(TOKEN_TILE, x0.shape[0], x1.shape[0])
    n0, n1 = x0.shape[0] // tm, x1.shape[0] // tm

    def group0(width):
        return pl.BlockSpec((tm, width), lambda i: (jnp.minimum(i, n0 - 1), 0))

    def group1(width):
        return pl.BlockSpec((tm, width), lambda i: (jnp.maximum(i - n0, 0), 0))

    return pl.pallas_call(
        functools.partial(_out_proj_joined_kernel, n0=n0),
        grid=(n0 + n1,),
        in_specs=[
            group0(d), group0(a0.shape[1]), group0(m0.shape[1]),
            group1(d), group1(a1.shape[1]), group1(m1.shape[1]),
            pl.BlockSpec(w.shape, lambda i: (0, 0)),
        ],
        out_specs=pl.BlockSpec((tm, d), lambda i: (i, 0)),
        out_shape=jax.ShapeDtypeStruct(((n0 + n1) * tm, d), F32),
        compiler_params=_params(("parallel",)),
        name="out_proj_joined",
    )(x0, a0, m0, x1, a1, m1, w)


def _ffn_kernel(x_ref, g_ref, wg_ref, wu_ref, wd_ref, o_ref, h_ref, acc_ref):
    f = pl.program_id(1)

    @pl.when(f == 0)
    def _():
        h_ref[...] = _rms(x_ref[...], g_ref[...]).astype(BF16)
        acc_ref[...] = jnp.zeros_like(acc_ref)

    h = h_ref[...]
    a = jax.nn.silu(_dot(h, wg_ref[...])) * _dot(h, wu_ref[...])
    acc_ref[...] += _dot(a.astype(BF16), wd_ref[...])

    @pl.when(f == pl.num_programs(1) - 1)
    def _():
        o_ref[...] = x_ref[...] + acc_ref[...]


def _ffn(x, g, w_gu, w_down):
    t, d = x.shape
    tm = min(TOKEN_TILE, t)
    tf = FFN_COLS
    nf = FFN_DIM // tf
    return pl.pallas_call(
        _ffn_kernel,
        grid=(t // tm, nf),
        in_specs=[
            pl.BlockSpec((tm, d), lambda i, f: (i, 0)),
            pl.BlockSpec((1, d), lambda i, f: (0, 0)),
            pl.BlockSpec((d, tf), lambda i, f: (0, f)),
            pl.BlockSpec((d, tf), lambda i, f: (0, nf + f)),
            pl.BlockSpec((tf, d), lambda i, f: (f, 0)),
        ],
        out_specs=pl.BlockSpec((tm, d), lambda i, f: (i, 0)),
        out_shape=jax.ShapeDtypeStruct((t, d), F32),
        scratch_shapes=[pltpu.VMEM((tm, d), BF16), pltpu.VMEM((tm, d), F32)],
        compiler_params=_params(("parallel", "arbitrary")),
        name="ffn",
    )(x, g.reshape(1, d), w_gu, w_gu, w_down)


def _router_kernel(x_ref, g_ref, wr_ref, r_ref):
    h = _rms(x_ref[...], g_ref[...])
    logits = jnp.dot(h, wr_ref[...], preferred_element_type=F32, precision=HIGHEST)
    lane = lax.broadcasted_iota(jnp.int32, logits.shape, 1).astype(F32)
    logits = jnp.where(lane < N_EXPERTS, logits, -jnp.inf)
    m1 = jnp.max(logits, axis=-1, keepdims=True)
    i1 = jnp.min(jnp.where(logits == m1, lane, float(LANES)), axis=-1, keepdims=True)
    rest = jnp.where(lane == i1, -jnp.inf, logits)
    m2 = jnp.max(rest, axis=-1, keepdims=True)
    i2 = jnp.min(jnp.where(rest == m2, lane, float(LANES)), axis=-1, keepdims=True)
    e2 = jnp.exp(m2 - m1)
    w1 = 1.0 / (1.0 + e2)
    w2 = e2 / (1.0 + e2)
    r = jnp.where(lane == 0, i1, 0.0)
    r = jnp.where(lane == 1, i2, r)
    r = jnp.where(lane == 2, w1, r)
    r_ref[...] = jnp.where(lane == 3, w2, r)


def _router(x, g, w_router_pad):
    t, d = x.shape
    tm = min(TOKEN_TILE, t)
    return pl.pallas_call(
        _router_kernel,
        grid=(t // tm,),
        in_specs=[
            pl.BlockSpec((tm, d), lambda i: (i, 0)),
            pl.BlockSpec((1, d), lambda i: (0, 0)),
            pl.BlockSpec((d, LANES), lambda i: (0, 0)),
        ],
        out_specs=pl.BlockSpec((tm, LANES), lambda i: (i, 0)),
        out_shape=jax.ShapeDtypeStruct((t, LANES), F32),
        compiler_params=_params(("parallel",)),
        name="router",
    )(x, g.reshape(1, d), w_router_pad)


def _start_row_gather(src_hbm, idx_ref, idx0, stride, dst, sem):
    def body(r, carry):
        row = idx_ref[idx0 + stride * r]
        pltpu.make_async_copy(src_hbm.at[pl.ds(row, 1)], dst.at[pl.ds(r, 1)], sem).start()
        return carry

    lax.fori_loop(0, dst.shape[0], body, 0, unroll=8)


def _wait_row_gather(src_hbm, dst, sem):
    pltpu.make_async_copy(src_hbm.at[pl.ds(0, dst.shape[0])], dst, sem).wait()


def _moe_kernel(te_ref, nt_ref, src_ref, x_hbm, g_ref, rw_ref, wg_ref, wu_ref, wd_ref, o_ref,
                xbuf, h_ref, acc_ref, sem):
    i = pl.program_id(0)
    f = pl.program_id(1)
    tm = h_ref.shape[0]
    nt = nt_ref[0]
    slot = i % 2

    @pl.when((i == 0) & (f == 0))
    def _():
        _start_row_gather(x_hbm, src_ref, 0, 1, xbuf.at[0], sem.at[0])

    @pl.when(i < nt)
    def _():
        @pl.when(f == 0)
        def _():
            @pl.when(i + 1 < nt)
            def _():
                _start_row_gather(x_hbm, src_ref, (i + 1) * tm, 1, xbuf.at[1 - slot], sem.at[1 - slot])

            _wait_row_gather(x_hbm, xbuf.at[slot], sem.at[slot])
            h_ref[...] = _rms(xbuf[slot], g_ref[...]).astype(BF16)
            acc_ref[...] = jnp.zeros_like(acc_ref)

        h = h_ref[...]
        a = jax.nn.silu(_dot(h, wg_ref[...])) * _dot(h, wu_ref[...])
        acc_ref[...] += _dot(a.astype(BF16), wd_ref[...])

        @pl.when(f == pl.num_programs(1) - 1)
        def _():
            o_ref[...] = acc_ref[...] * rw_ref[...]

    @pl.when((i >= nt) & (f == 0))
    def _():
        o_ref[...] = jnp.zeros_like(o_ref)


def _moe_ffn(x, g, src, roww, tile_expert, n_tiles, w_gu, w_down):
    d = x.shape[1]
    p = src.shape[0]
    tm = MOE_TILE
    tf = EXPERT_COLS
    nf = EXPERT_DIM // tf

    def row(i, f, te, nt, src):
        return (jnp.minimum(i, nt[0] - 1), 0)

    def col(i, f, nt):
        return jnp.where(i < nt[0], f, nf - 1)

    grid_spec = pltpu.PrefetchScalarGridSpec(
        num_scalar_prefetch=3,
        grid=(p // tm, nf),
        in_specs=[
            pl.BlockSpec(memory_space=pl.ANY),
            pl.BlockSpec((1, d), lambda i, f, te, nt, src: (0, 0)),
            pl.BlockSpec((tm, 1), row),
            pl.BlockSpec((None, d, tf), lambda i, f, te, nt, src: (te[i], 0, col(i, f, nt))),
            pl.BlockSpec((None, d, tf), lambda i, f, te, nt, src: (te[i], 0, nf + col(i, f, nt))),
            pl.BlockSpec((None, tf, d), lambda i, f, te, nt, src: (te[i], col(i, f, nt), 0)),
        ],
        out_specs=pl.BlockSpec((tm, d), lambda i, f, te, nt, src: (i, 0)),
        scratch_shapes=[
            pltpu.VMEM((2, tm, d), F32),
            pltpu.VMEM((tm, d), BF16),
            pltpu.VMEM((tm, d), F32),
            pltpu.SemaphoreType.DMA((2,)),
        ],
    )
    return pl.pallas_call(
        _moe_kernel,
        grid_spec=grid_spec,
        out_shape=jax.ShapeDtypeStruct((p, d), F32),
        compiler_params=_params(("arbitrary", "arbitrary")),
        name="moe_ffn",
    )(tile_expert, n_tiles, src, x, g.reshape(1, d), roww, w_gu, w_gu, w_down)


def _combine_norm_kernel(dest_ref, x_ref, y_hbm, g_ref, o_ref, buf, sem, *, tok0):
    i = pl.program_id(0)
    tm = x_ref.shape[0]
    slot = i % 2

    def fetch(tile, s):
        for k in range(2):
            _start_row_gather(y_hbm, dest_ref, 2 * (tok0 + tile * tm) + k, 2, buf.at[s, k], sem.at[s, k])

    @pl.when(i == 0)
    def _():
        fetch(0, 0)

    @pl.when(i + 1 < pl.num_programs(0))
    def _():
        fetch(i + 1, 1 - slot)

    for k in range(2):
        _wait_row_gather(y_hbm, buf.at[slot, k], sem.at[slot, k])
    o_ref[...] = _rms(x_ref[...] + (buf[slot, 0] + buf[slot, 1]), g_ref[...])


def _combine_norm(x, rows_out, dest, g, tok0, rows):
    d = x.shape[1]
    tm = min(TOKEN_TILE, rows)
    blk0 = tok0 // tm
    grid_spec = pltpu.PrefetchScalarGridSpec(
        num_scalar_prefetch=1,
        grid=(rows // tm,),
        in_specs=[
            pl.BlockSpec((tm, d), lambda i, dest: (i + blk0, 0)),
            pl.BlockSpec(memory_space=pl.ANY),
            pl.BlockSpec((1, d), lambda i, dest: (0, 0)),
        ],
        out_specs=pl.BlockSpec((tm, d), lambda i, dest: (i, 0)),
        scratch_shapes=[pltpu.VMEM((2, 2, tm, d), F32), pltpu.SemaphoreType.DMA((2, 2))],
    )
    return pl.pallas_call(
        functools.partial(_combine_norm_kernel, tok0=tok0),
        grid_spec=grid_spec,
        out_shape=jax.ShapeDtypeStruct((rows, d), F32),
        compiler_params=_params(("arbitrary",)),
        name="combine_norm",
    )(dest, x, rows_out, g.reshape(1, d))


def _head_softmax_pv(s, v):
    m = jnp.max(s, axis=-1, keepdims=True)
    p = jnp.exp(s - m)
    return _dot(p.astype(BF16), v) / jnp.sum(p, axis=-1, keepdims=True)


def _mem_attn_shared_kernel(q_ref, kv_ref, o_ref):
    q = q_ref[...]
    k = kv_ref[0, :, 0:MEM_WIDTH].astype(BF16)
    v = kv_ref[0, :, MEM_WIDTH:].astype(BF16)
    col = lax.broadcasted_iota(jnp.int32, (1, MEM_WIDTH), 1)
    acc = jnp.zeros(q.shape, F32)
    for h in range(MEM_HEADS):
        in_head = (col >= h * MEM_HEAD_DIM) & (col < (h + 1) * MEM_HEAD_DIM)
        qh = jnp.where(in_head, q, 0.0).astype(BF16)
        s = _dot_nt(qh, k) * MEM_HEAD_DIM ** -0.5
        acc = acc + jnp.where(in_head, _head_softmax_pv(s, v), 0.0)
    o_ref[...] = acc


def _mem_attn_shared(proj, q_col, mem_kv, seq_len):
    t = proj.shape[0]
    tq = min(TOKEN_TILE, seq_len)
    per_seq = seq_len // tq
    return pl.pallas_call(
        _mem_attn_shared_kernel,
        grid=(t // tq,),
        in_specs=[
            pl.BlockSpec((tq, MEM_WIDTH), lambda i: (i, q_col)),
            pl.BlockSpec((1, N_MEM, 2 * MEM_WIDTH), lambda i: (i // per_seq, 0, 0)),
        ],
        out_specs=pl.BlockSpec((tq, MEM_WIDTH), lambda i: (i, 0)),
        out_shape=jax.ShapeDtypeStruct((t, MEM_WIDTH), F32),
        compiler_params=_params(("parallel",)),
        name="mem_attn_shared",
    )(proj, mem_kv)


def _mem_attn_decode_kernel(q_ref, k_ref, v_ref, o_ref, *, seq_len):
    rows = MEM_HEADS * seq_len
    row_head = _div(lax.broadcasted_iota(jnp.int32, (rows, MEM_WIDTH), 0), seq_len)
    col_head = _div(lax.broadcasted_iota(jnp.int32, (rows, MEM_WIDTH), 1), MEM_HEAD_DIM)
    diag = row_head == col_head
    out_head = _div(lax.broadcasted_iota(jnp.int32, (seq_len, MEM_WIDTH), 1), MEM_HEAD_DIM)
    for b in range(k_ref.shape[0]):
        q = q_ref[b * seq_len:(b + 1) * seq_len, :]
        qd = jnp.where(diag, jnp.concatenate([q] * MEM_HEADS, axis=0), 0.0).astype(BF16)
        s = _dot_nt(qd, k_ref[b].astype(BF16)) * MEM_HEAD_DIM ** -0.5
        o = _head_softmax_pv(s, v_ref[b].astype(BF16))
        acc = jnp.zeros((seq_len, MEM_WIDTH), F32)
        for h in range(MEM_HEADS):
            acc = acc + jnp.where(out_head == h, o[h * seq_len:(h + 1) * seq_len], 0.0)
        o_ref[b * seq_len:(b + 1) * seq_len, :] = acc


def _mem_attn_decode(proj, q_col, mem_k, mem_v, seq_len):
    t = proj.shape[0]
    nb = mem_k.shape[0]
    bb = min(SEQ_BATCH, nb)
    return pl.pallas_call(
        functools.partial(_mem_attn_decode_kernel, seq_len=seq_len),
        grid=(nb // bb,),
        in_specs=[
            pl.BlockSpec((bb * seq_len, MEM_WIDTH), lambda i: (i, q_col)),
            pl.BlockSpec((bb, N_MEM, MEM_WIDTH), lambda i: (i, 0, 0)),
            pl.BlockSpec((bb, N_MEM, MEM_WIDTH), lambda i: (i, 0, 0)),
        ],
        out_specs=pl.BlockSpec((bb * seq_len, MEM_WIDTH), lambda i: (i, 0)),
        out_shape=jax.ShapeDtypeStruct((t, MEM_WIDTH), F32),
        compiler_params=_params(("parallel",)),
        name="mem_attn_decode",
    )(proj, mem_k, mem_v)


def _sink_softmax_pv(s, sink, v):
    m = jnp.maximum(jnp.max(s, axis=-1, keepdims=True), sink)
    p = jnp.exp(s - m)
    denom = jnp.sum(p, axis=-1, keepdims=True) + jnp.exp(sink - m)
    return _dot(p.astype(BF16), v) / denom


def _swa_prefill_kernel(slope_ref, sink_ref, q_ref, kp_ref, kc_ref, vp_ref, vc_ref, o_ref, *, blocks_per_seq):
    w = SWA_WINDOW
    first = (pl.program_id(0) % blocks_per_seq) == 0
    q = q_ref[...]
    k = jnp.concatenate([kp_ref[...], kc_ref[...]], axis=0)
    v = jnp.concatenate([vp_ref[...], vc_ref[...]], axis=0)
    qi = lax.broadcasted_iota(jnp.int32, (w, 2 * w), 0)
    kj = lax.broadcasted_iota(jnp.int32, (w, 2 * w), 1)
    dist = qi + w - kj
    valid = (dist >= 0) & (dist < w) & ((kj >= w) | jnp.logical_not(first))
    distf = dist.astype(F32)
    outs = []
    for kh in range(SWA_KV_HEADS):
        kk = k[:, kh * SWA_HEAD_DIM:(kh + 1) * SWA_HEAD_DIM].astype(BF16)
        vv = v[:, kh * SWA_HEAD_DIM:(kh + 1) * SWA_HEAD_DIM].astype(BF16)
        heads = [kh * SWA_GROUP + g for g in range(SWA_GROUP)]
        qg = jnp.concatenate([q[:, h * SWA_HEAD_DIM:(h + 1) * SWA_HEAD_DIM] for h in heads], axis=0).astype(BF16)
        s3 = _dot_nt(qg, kk) * SWA_HEAD_DIM ** -0.5
        for g, h in enumerate(heads):
            s = s3[g * w:(g + 1) * w] - slope_ref[h] * distf
            s = jnp.where(valid, s, NEG_INF)
            outs.append(_sink_softmax_pv(s, sink_ref[h], vv))
    o_ref[...] = jnp.concatenate(outs, axis=1)


def _swa_prefill(proj, slopes, sinks, seq_len):
    t = proj.shape[0]
    w = SWA_WINDOW
    per_seq = seq_len // w
    kcol = SWA_Q_WIDTH // SWA_KV_WIDTH
    vcol = kcol + 1

    def prev(i):
        return jnp.where(i % per_seq == 0, i, i - 1)

    smem = pl.BlockSpec(memory_space=pltpu.SMEM)
    return pl.pallas_call(
        functools.partial(_swa_prefill_kernel, blocks_per_seq=per_seq),
        grid=(t // w,),
        in_specs=[
            smem,
            smem,
            pl.BlockSpec((w, SWA_Q_WIDTH), lambda i: (i, 0)),
            pl.BlockSpec((w, SWA_KV_WIDTH), lambda i: (prev(i), kcol)),
            pl.BlockSpec((w, SWA_KV_WIDTH), lambda i: (i, kcol)),
            pl.BlockSpec((w, SWA_KV_WIDTH), lambda i: (prev(i), vcol)),
            pl.BlockSpec((w, SWA_KV_WIDTH), lambda i: (i, vcol)),
        ],
        out_specs=pl.BlockSpec((w, SWA_Q_WIDTH), lambda i: (i, 0)),
        out_shape=jax.ShapeDtypeStruct((t, SWA_Q_WIDTH), F32),
        compiler_params=_params(("parallel",)),
        name="swa_prefill",
    )(slopes, sinks, proj, proj, proj, proj, proj)


def _swa_decode_kernel(q_ref, kn_ref, vn_ref, kc_ref, vc_ref, rep_ref, slope_ref, sink_ref, o_ref, *, seq_len):
    w = SWA_WINDOW
    rows = SWA_HEADS * seq_len
    span = 2 * w
    row = lax.broadcasted_iota(jnp.int32, (rows, SWA_Q_WIDTH), 0)
    col = lax.broadcasted_iota(jnp.int32, (rows, SWA_Q_WIDTH), 1)
    diag = _div(row, seq_len) == _div(col, SWA_HEAD_DIM)
    ql = lax.broadcasted_iota(jnp.int32, (rows, span), 0) & (seq_len - 1)
    kj = lax.broadcasted_iota(jnp.int32, (rows, span), 1)
    dist = w + ql - kj
    valid = (dist >= 0) & (dist < w)
    bias = slope_ref[...] * dist.astype(F32)
    sink = sink_ref[...]
    out_head = _div(lax.broadcasted_iota(jnp.int32, (seq_len, SWA_Q_WIDTH), 1), SWA_HEAD_DIM)
    rep = rep_ref[...]
    tail = jnp.zeros((w - seq_len, SWA_KV_WIDTH), F32)
    for b in range(kc_ref.shape[0]):
        rs = slice(b * seq_len, (b + 1) * seq_len)
        k = jnp.concatenate([kc_ref[b], kn_ref[rs, :], tail], axis=0).astype(BF16)
        v = jnp.concatenate([vc_ref[b], vn_ref[rs, :], tail], axis=0).astype(BF16)
        k_rep = _dot(k, rep).astype(BF16)
        v_rep = _dot(v, rep).astype(BF16)
        qd = jnp.where(diag, jnp.concatenate([q_ref[rs, :]] * SWA_HEADS, axis=0), 0.0).astype(BF16)
        s = _dot_nt(qd, k_rep) * SWA_HEAD_DIM ** -0.5 - bias
        s = jnp.where(valid, s, NEG_INF)
        o = _sink_softmax_pv(s, sink, v_rep)
        acc = jnp.zeros((seq_len, SWA_Q_WIDTH), F32)
        for h in range(SWA_HEADS):
            acc = acc + jnp.where(out_head == h, o[h * seq_len:(h + 1) * seq_len], 0.0)
        o_ref[rs, :] = acc


def _swa_decode(proj, cache_k, cache_v, slopes, sinks, seq_len):
    t = proj.shape[0]
    nb = cache_k.shape[0]
    bb = min(SEQ_BATCH, nb)
    w = SWA_WINDOW
    kcol = SWA_Q_WIDTH // SWA_KV_WIDTH
    src = (jnp.arange(SWA_Q_WIDTH) // SWA_HEAD_DIM // SWA_GROUP) * SWA_HEAD_DIM + jnp.arange(SWA_Q_WIDTH) % SWA_HEAD_DIM
    rep = (jnp.arange(SWA_KV_WIDTH)[:, None] == src[None, :]).astype(BF16)
    slope_rows = jnp.repeat(slopes, seq_len).reshape(-1, 1)
    sink_rows = jnp.repeat(sinks, seq_len).reshape(-1, 1)
    rows = SWA_HEADS * seq_len
    return pl.pallas_call(
        functools.partial(_swa_decode_kernel, seq_len=seq_len),
        grid=(nb // bb,),
        in_specs=[
            pl.BlockSpec((bb * seq_len, SWA_Q_WIDTH), lambda i: (i, 0)),
            pl.BlockSpec((bb * seq_len, SWA_KV_WIDTH), lambda i: (i, kcol)),
            pl.BlockSpec((bb * seq_len, SWA_KV_WIDTH), lambda i: (i, kcol + 1)),
            pl.BlockSpec((bb, w, SWA_KV_WIDTH), lambda i: (i, 0, 0)),
            pl.BlockSpec((bb, w, SWA_KV_WIDTH), lambda i: (i, 0, 0)),
            pl.BlockSpec((SWA_KV_WIDTH, SWA_Q_WIDTH), lambda i: (0, 0)),
            pl.BlockSpec((rows, 1), lambda i: (0, 0)),
            pl.BlockSpec((rows, 1), lambda i: (0, 0)),
        ],
        out_specs=pl.BlockSpec((bb * seq_len, SWA_Q_WIDTH), lambda i: (i, 0)),
        out_shape=jax.ShapeDtypeStruct((t, SWA_Q_WIDTH), F32),
        compiler_params=_params(("parallel",)),
        name="swa_decode",
    )(proj, proj, proj, cache_k, cache_v, rep, slope_rows, sink_rows)


def _gdn_kernel(*refs, nblk, spb):
    r = GDN_ROWS
    c = r // spb
    qkv_refs = refs[0:nblk]
    z_refs = refs[nblk:2 * nblk]
    ba_refs = refs[2 * nblk:3 * nblk]
    conv0_ref, s0_ref, cw_ref, alog_ref, dtb_ref, gn_ref, o_ref, sfin_ref, s_scr, fbuf = refs[3 * nblk:]
    step = pl.program_id(1)
    nseq = nblk * spb
    pad = 8
    hist = GDN_CONV_K - 1

    @pl.when(step == 0)
    def _():
        s_scr[...] = s0_ref[...]
        for s in range(nseq):
            fbuf[s, pad - hist:pad, :] = conv0_ref[s]

    ri = lax.broadcasted_iota(jnp.int32, (r, r), 0)
    ci = lax.broadcasted_iota(jnp.int32, (r, r), 1)
    same = _div(ri, c) == _div(ci, c)
    tri = same & (ri >= ci)
    strict = same & (ri > ci)
    eye = (ri == ci).astype(F32)
    cum_mat = jnp.concatenate([tri.astype(F32), same.astype(F32)], axis=0)
    row_seq = _div(lax.broadcasted_iota(jnp.int32, (r, 1), 0), c)
    zeros_rr = jnp.zeros((r, LANES), F32)
    cw = cw_ref[...]
    neg_a = -jnp.exp(alog_ref[...])
    gn = gn_ref[...]

    for n in range(nblk):
        u = qkv_refs[n][...]
        pieces = []
        for s in range(spb):
            idx = n * spb + s
            fbuf[idx, pad:pad + c, :] = u[s * c:(s + 1) * c]
            acc = fbuf[idx, pad - hist:pad - hist + c, :] * cw[0:1]
            for j in range(1, GDN_CONV_K):
                acc = acc + fbuf[idx, pad - hist + j:pad - hist + j + c, :] * cw[j:j + 1]
            fbuf[idx, pad - hist:pad, :] = fbuf[idx, pad + c - hist:pad + c, :]
            pieces.append(acc)
        conv = pieces[0] if spb == 1 else jnp.concatenate(pieces, axis=0)
        qkv = jax.nn.silu(conv)

        ba = ba_refs[n][...]
        beta_all = jax.nn.sigmoid(ba)
        xg = ba + dtb_ref[...]
        g_all = neg_a * (jnp.maximum(xg, 0.0) + jnp.log(1.0 + jnp.exp(-jnp.abs(xg))))
        gsum = jnp.dot(cum_mat, g_all, preferred_element_type=F32, precision=HIGHEST)
        gcum = gsum[0:r]
        gtot = gsum[r:2 * r]
        gcum_t = jnp.transpose(jnp.concatenate([gcum, zeros_rr], axis=0))

        z = z_refs[n][...]
        for h in range(GDN_HEADS):
            lo = h * GDN_DK
            q = qkv[:, lo:lo + GDN_DK]
            k = qkv[:, GDN_QK_WIDTH + lo:GDN_QK_WIDTH + lo + GDN_DK]
            v = qkv[:, 2 * GDN_QK_WIDTH + lo:2 * GDN_QK_WIDTH + lo + GDN_DV]
            q = q * lax.rsqrt(jnp.sum(q * q, axis=-1, keepdims=True) + L2_EPS) * GDN_DK ** -0.5
            k = k * lax.rsqrt(jnp.sum(k * k, axis=-1, keepdims=True) + L2_EPS)
            beta = beta_all[:, h:h + 1]
            gcol = gcum[:, GDN_HEADS + h:GDN_HEADS + h + 1]
            grow = gcum_t[GDN_HEADS + h:GDN_HEADS + h + 1, 0:r]
            glast = gtot[:, GDN_HEADS + h:GDN_HEADS + h + 1]
            decay = jnp.where(tri, jnp.exp(jnp.where(tri, gcol - grow, 0.0)), 0.0)
            eg = jnp.exp(gcol)
            kb = k * beta
            kq = _dot_nt(jnp.concatenate([kb, q], axis=0).astype(BF16), k.astype(BF16))
            lmat = jnp.where(strict, kq[0:r] * decay, 0.0)
            qk = kq[r:2 * r] * decay
            inv = eye - lmat
            power = lmat
            span = 2
            while span < c:
                power = _dot(power.astype(BF16), power.astype(BF16))
                inv = _dot(inv.astype(BF16), (eye + power).astype(BF16))
                span *= 2
            tq = jnp.concatenate([inv, _dot(qk.astype(BF16), inv.astype(BF16))], axis=0).astype(BF16)
            vb = v * beta
            kq_dec = jnp.concatenate([kb * eg, q * eg], axis=0).astype(BF16)
            kdec_t = jnp.transpose(jnp.concatenate([k * jnp.exp(glast - gcol), zeros_rr], axis=0))[:, 0:r]
            kdec_t = kdec_t.astype(BF16)

            if spb == 1:
                state = s_scr[n, h]
                ks_qs = _dot(kq_dec, state.astype(BF16))
                resid = vb - ks_qs[0:r]
                qs = ks_qs[r:2 * r]
            else:
                ks_parts, qs_parts = [], []
                for s in range(spb):
                    state = s_scr[n * spb + s, h].astype(BF16)
                    rows = jnp.concatenate([kq_dec[s * c:(s + 1) * c], kq_dec[r + s * c:r + (s + 1) * c]], axis=0)
                    both = _dot(rows, state)
                    ks_parts.append(both[0:c])
                    qs_parts.append(both[c:2 * c])
                resid = vb - jnp.concatenate(ks_parts, axis=0)
                qs = jnp.concatenate(qs_parts, axis=0)

            vo = _dot(tq, resid.astype(BF16))
            v_new = vo[0:r]
            o = qs + vo[r:2 * r]
            for s in range(spb):
                idx = n * spb + s
                vs = v_new if spb == 1 else jnp.where(row_seq == s, v_new, 0.0)
                carry = jnp.exp(glast[s * c:s * c + 1, :])
                s_scr[idx, h] = s_scr[idx, h] * carry + _dot(kdec_t, vs.astype(BF16))

            on = _rms(o, gn)
            zh = z[:, h * GDN_DV:(h + 1) * GDN_DV]
            o_ref[n, :, h * GDN_DV:(h + 1) * GDN_DV] = on * jax.nn.silu(zh)

    @pl.when(step == pl.num_programs(1) - 1)
    def _():
        sfin_ref[...] = s_scr[...]


def _gdn(proj, conv_buf, s0, conv_w, a_log, dt_bias, norm_g, *, nblk, spb, steps):
    t = proj.shape[0]
    r = GDN_ROWS
    groups = t // (r * nblk * steps)
    nseq = nblk * spb
    c = r // spb
    z_col = GDN_CONV_DIM // GDN_V_WIDTH
    ba_col = (GDN_CONV_DIM + GDN_V_WIDTH + MEM_WIDTH) // LANES

    def rows(n, col):
        return lambda g, l: ((g * nblk + n) * steps + l, col)

    lane6 = jnp.zeros((1, LANES), F32)
    alog = lane6.at[0, GDN_HEADS:2 * GDN_HEADS].set(a_log)
    dtb = lane6.at[0, GDN_HEADS:2 * GDN_HEADS].set(dt_bias)
    const = lambda g, l: (0, 0)
    in_specs = (
        [pl.BlockSpec((r, GDN_CONV_DIM), rows(n, 0)) for n in range(nblk)]
        + [pl.BlockSpec((r, GDN_V_WIDTH), rows(n, z_col)) for n in range(nblk)]
        + [pl.BlockSpec((r, LANES), rows(n, ba_col)) for n in range(nblk)]
        + [
            pl.BlockSpec((nseq, GDN_CONV_K - 1, GDN_CONV_DIM), lambda g, l: (g, 0, 0)),
            pl.BlockSpec((nseq, GDN_HEADS, GDN_DK, GDN_DV), lambda g, l: (g, 0, 0, 0)),
            pl.BlockSpec((GDN_CONV_K, GDN_CONV_DIM), const),
            pl.BlockSpec((1, LANES), const),
            pl.BlockSpec((1, LANES), const),
            pl.BlockSpec((1, GDN_DV), const),
        ]
    )
    out, s_fin = pl.pallas_call(
        functools.partial(_gdn_kernel, nblk=nblk, spb=spb),
        grid=(groups, steps),
        in_specs=in_specs,
        out_specs=[
            pl.BlockSpec((nblk, r, GDN_V_WIDTH), lambda g, l: (0, g * steps + l, 0)),
            pl.BlockSpec((nseq, GDN_HEADS, GDN_DK, GDN_DV), lambda g, l: (g, 0, 0, 0)),
        ],
        out_shape=[
            jax.ShapeDtypeStruct((nblk, groups * steps * r, GDN_V_WIDTH), F32),
            jax.ShapeDtypeStruct((groups * nseq, GDN_HEADS, GDN_DK, GDN_DV), F32),
        ],
        scratch_shapes=[
            pltpu.VMEM((nseq, GDN_HEADS, GDN_DK, GDN_DV), F32),
            pltpu.VMEM((nseq, 8 + c, GDN_CONV_DIM), F32),
        ],
        compiler_params=_params(("parallel", "arbitrary")),
        name="gdn",
    )(*([proj] * (3 * nblk)), conv_buf, s0, conv_w, alog, dtb, norm_g.reshape(1, GDN_DV))
    return out.reshape(t, GDN_V_WIDTH), s_fin


def _dispatch(route, tm, p):
    t = route.shape[0]
    experts = route[:, 0:2].astype(jnp.int32).reshape(-1)
    weights = route[:, 2:4].reshape(-1)
    onehot = (experts[:, None] == jnp.arange(N_EXPERTS)[None, :]).astype(jnp.int32)
    csum = jnp.cumsum(onehot, axis=0)
    rank = jnp.take_along_axis(csum, experts[:, None], axis=1)[:, 0] - 1
    tiles = (csum[-1] + tm - 1) // tm
    tile_end = jnp.cumsum(tiles)
    start = (tile_end - tiles) * tm
    dest = start[experts] + rank
    n_tiles = tile_end[-1:].astype(jnp.int32)
    tile_expert = jnp.sum(tile_end[None, :] <= jnp.arange(p // tm)[:, None], axis=1)
    tile_expert = jnp.minimum(tile_expert, N_EXPERTS - 1).astype(jnp.int32)
    src = jnp.zeros((p,), jnp.int32).at[dest].set(jnp.arange(2 * t, dtype=jnp.int32) // 2)
    roww = jnp.zeros((p,), F32).at[dest].set(weights)
    return dest, src, roww.reshape(p, 1), tile_expert, n_tiles


def kernel(x_prompt, x_sample, state_gdn_conv, state_gdn_ssm, cache_swa_k, cache_swa_v, cache_mem_k, cache_mem_v, mem_prompt, attn_norm, ffn_norm, mem_norm, final_norm, w_in_gdn, gdn_conv_w, gdn_a_log, gdn_dt_bias, gdn_norm, w_out_gdn, w_in_swa, swa_sinks, w_out_swa, w_mem_kv, w_ffn_gu, w_ffn_down, w_router, w_exp_gu, w_exp_down):
    bp, lp, d = x_prompt.shape
    bs, ls, _ = x_sample.shape
    tp, ts = bp * lp, bs * ls
    xp = x_prompt.reshape(tp, d)
    xs = x_sample.reshape(ts, d)

    mem = mem_prompt.reshape(bp * N_MEM, d)
    mem_kv = [
        _norm_matmul(mem, mem_norm[i], w_mem_kv[i].astype(BF16), 2 * MEM_WIDTH).reshape(bp, N_MEM, 2 * MEM_WIDTH)
        for i in range(2)
    ]
    new_mem_k = jnp.stack([kv[..., :MEM_WIDTH].reshape(bp, N_MEM, MEM_HEADS, MEM_HEAD_DIM) for kv in mem_kv])
    new_mem_v = jnp.stack([kv[..., MEM_WIDTH:].reshape(bp, N_MEM, MEM_HEADS, MEM_HEAD_DIM) for kv in mem_kv])

    w_in = w_in_gdn[0]
    o_z = GDN_CONV_DIM + GDN_V_WIDTH
    o_mem = o_z + 2 * GDN_HEADS
    w0 = jnp.concatenate(
        [w_in[:, :o_z], w_in[:, o_mem:], w_in[:, o_z:o_mem], jnp.zeros((d, LANES - 2 * GDN_HEADS), F32)], axis=1
    ).astype(BF16)
    mq_col0 = o_z // MEM_WIDTH
    w_out0 = w_out_gdn[0].astype(BF16)
    w_gu0 = w_ffn_gu[0].astype(BF16)
    w_dn0 = w_ffn_down[0].astype(BF16)

    proj_p = _norm_matmul(xp, attn_norm[0], w0, w0.shape[1] // 3)
    proj_s = _norm_matmul(xs, attn_norm[0], w0, w0.shape[1] // 3)

    zero_conv = jnp.zeros((bp, GDN_CONV_K - 1, GDN_CONV_DIM), F32)
    zero_state = jnp.zeros((bp, GDN_HEADS, GDN_DK, GDN_DV), F32)
    gdn_p, ssm_p = _gdn(proj_p, zero_conv, zero_state, gdn_conv_w[0], gdn_a_log[0], gdn_dt_bias[0], gdn_norm[0],
                        nblk=bp, spb=1, steps=lp // GDN_ROWS)
    gdn_s, ssm_s = _gdn(proj_s, state_gdn_conv[0], state_gdn_ssm[0], gdn_conv_w[0], gdn_a_log[0], gdn_dt_bias[0],
                        gdn_norm[0], nblk=1, spb=GDN_ROWS // ls, steps=1)
    hist = GDN_CONV_K - 1
    conv_p = proj_p.reshape(bp, lp, -1)[:, lp - hist:, :GDN_CONV_DIM]
    conv_s = proj_s.reshape(bs, ls, -1)[:, ls - hist:, :GDN_CONV_DIM]

    memo_p = _mem_attn_shared(proj_p, mq_col0, mem_kv[0], lp)
    memo_s = _mem_attn_decode(proj_s, mq_col0, cache_mem_k[0].reshape(bs, N_MEM, MEM_WIDTH),
                              cache_mem_v[0].reshape(bs, N_MEM, MEM_WIDTH), ls)
    xp = _out_proj(xp, gdn_p, memo_p, w_out0)
    xs = _out_proj(xs, gdn_s, memo_s, w_out0)
    xp = _ffn(xp, ffn_norm[0], w_gu0, w_dn0)
    xs = _ffn(xs, ffn_norm[0], w_gu0, w_dn0)

    w1 = w_in_swa[0].astype(BF16)
    mq_col1 = (SWA_Q_WIDTH + 2 * SWA_KV_WIDTH) // MEM_WIDTH
    w_out1 = w_out_swa[0].astype(BF16)
    slopes = 2.0 ** (-8.0 * jnp.arange(1, SWA_HEADS + 1, dtype=F32) / SWA_HEADS)
    sinks = swa_sinks[0].astype(F32)

    proj_p = _norm_matmul(xp, attn_norm[1], w1, w1.shape[1] // 2)
    proj_s = _norm_matmul(xs, attn_norm[1], w1, w1.shape[1] // 2)
    swa_p = _swa_prefill(proj_p, slopes, sinks, lp)
    cache_k = cache_swa_k[0].reshape(bs, SWA_WINDOW, SWA_KV_WIDTH)
    cache_v = cache_swa_v[0].reshape(bs, SWA_WINDOW, SWA_KV_WIDTH)
    swa_s = _swa_decode(proj_s, cache_k, cache_v, slopes, sinks, ls)

    k0, v0 = SWA_Q_WIDTH, SWA_Q_WIDTH + SWA_KV_WIDTH
    pp = proj_p.reshape(bp, lp, -1)
    ps = proj_s.reshape(bs, ls, -1)
    kv_shape = (SWA_KV_HEADS, SWA_HEAD_DIM)
    swk_p = pp[:, lp - SWA_WINDOW:, k0:k0 + SWA_KV_WIDTH].reshape(bp, SWA_WINDOW, *kv_shape)
    swv_p = pp[:, lp - SWA_WINDOW:, v0:v0 + SWA_KV_WIDTH].reshape(bp, SWA_WINDOW, *kv_shape)
    swk_s = jnp.concatenate([cache_k[:, ls:], ps[:, :, k0:k0 + SWA_KV_WIDTH]], axis=1).reshape(bs, SWA_WINDOW, *kv_shape)
    swv_s = jnp.concatenate([cache_v[:, ls:], ps[:, :, v0:v0 + SWA_KV_WIDTH]], axis=1).reshape(bs, SWA_WINDOW, *kv_shape)

    memo_p = _mem_attn_shared(proj_p, mq_col1, mem_kv[1], lp)
    memo_s = _mem_attn_decode(proj_s, mq_col1, cache_mem_k[1].reshape(bs, N_MEM, MEM_WIDTH),
                              cache_mem_v[1].reshape(bs, N_MEM, MEM_WIDTH), ls)
    t_all = tp + ts
    x_all = _out_proj_joined((xp, swa_p, memo_p), (xs, swa_s, memo_s), w_out1)

    w_r = jnp.concatenate([w_router[0], jnp.zeros((d, LANES - N_EXPERTS), F32)], axis=1)
    route = _router(x_all, ffn_norm[1], w_r)
    tm = MOE_TILE
    p_rows = -(-(2 * t_all + N_EXPERTS * (tm - 1)) // tm) * tm
    dest, src, roww, tile_expert, n_tiles = _dispatch(route, tm, p_rows)
    rows_out = _moe_ffn(x_all, ffn_norm[1], src, roww, tile_expert, n_tiles,
                        w_exp_gu[0].astype(BF16), w_exp_down[0].astype(BF16))
    y_p = _combine_norm(x_all, rows_out, dest, final_norm, 0, tp)
    y_s = _combine_norm(x_all, rows_out, dest, final_norm, tp, ts)

    return (
        y_p.reshape(bp, lp, d),
        y_s.reshape(bs, ls, d),
        conv_p[None],
        ssm_p[None],
        swk_p[None],
        swv_p[None],
        new_mem_k,
        new_mem_v,
        conv_s[None],
        ssm_s[None],
        swk_s[None],
        swv_s[None],
    )
```

```python
import functools

import jax
import jax.numpy as jnp
from jax import lax
from jax.experimental import pallas as pl
from jax.experimental.pallas import tpu as pltpu

F32 = jnp.float32
BF16 = jnp.bfloat16
HIGHEST = lax.Precision.HIGHEST

D_MODEL = 1024
RMS_EPS = 1e-6
L2_EPS = 1e-6
NEG_INF = -1e30

GDN_HEADS = 6
GDN_DK = 128
GDN_DV = 128
GDN_CONV_K = 4
GDN_QK_WIDTH = GDN_HEADS * GDN_DK
GDN_V_WIDTH = GDN_HEADS * GDN_DV
GDN_CONV_DIM = 2 * GDN_QK_WIDTH + GDN_V_WIDTH
GDN_ROWS = 64

SWA_HEADS = 12
SWA_KV_HEADS = 4
SWA_HEAD_DIM = 64
SWA_GROUP = SWA_HEADS // SWA_KV_HEADS
SWA_WINDOW = 128
SWA_Q_WIDTH = SWA_HEADS * SWA_HEAD_DIM
SWA_KV_WIDTH = SWA_KV_HEADS * SWA_HEAD_DIM

N_MEM = 256
MEM_HEADS = 4
MEM_HEAD_DIM = 64
MEM_WIDTH = MEM_HEADS * MEM_HEAD_DIM

FFN_DIM = 2816
N_EXPERTS = 8
EXPERT_DIM = 3584

LANES = 128
VMEM_LIMIT = 56 * 1024 * 1024

TOKEN_TILE = 512
MOE_TILE = 512
FFN_COLS = 1408
EXPERT_STEPS = 4
EXPERT_COLS = EXPERT_DIM // EXPERT_STEPS
SEQ_BATCH = 8


def _params(sem):
    return pltpu.CompilerParams(dimension_semantics=sem, vmem_limit_bytes=VMEM_LIMIT)


def _rms(x, g):
    return x * lax.rsqrt(jnp.mean(x * x, axis=-1, keepdims=True) + RMS_EPS) * g


def _dot(a, b):
    return jnp.dot(a, b, preferred_element_type=F32)


def _div(x, d):
    assert d & (d - 1) == 0
    return lax.shift_right_logical(x, d.bit_length() - 1)


def _dot_nt(a, b):
    return lax.dot_general(a, b, (((1,), (1,)), ((), ())), preferred_element_type=F32)


def _norm_matmul_kernel(x_ref, g_ref, w_ref, o_ref, h_ref):
    @pl.when(pl.program_id(1) == 0)
    def _():
        h_ref[...] = _rms(x_ref[...], g_ref[...]).astype(BF16)

    o_ref[...] = _dot(h_ref[...], w_ref[...])


def _norm_matmul(x, g, w, tn):
    t, k = x.shape
    n = w.shape[1]
    tm = min(TOKEN_TILE, t)
    return pl.pallas_call(
        _norm_matmul_kernel,
        grid=(t // tm, n // tn),
        in_specs=[
            pl.BlockSpec((tm, k), lambda i, j: (i, 0)),
            pl.BlockSpec((1, k), lambda i, j: (0, 0)),
            pl.BlockSpec((k, tn), lambda i, j: (0, j)),
        ],
        out_specs=pl.BlockSpec((tm, tn), lambda i, j: (i, j)),
        out_shape=jax.ShapeDtypeStruct((t, n), F32),
        scratch_shapes=[pltpu.VMEM((tm, k), BF16)],
        compiler_params=_params(("parallel", "arbitrary")),
        name="norm_matmul",
    )(x, g.reshape(1, k), w)


def _out_proj_block(x_ref, a_ref, m_ref, w_ref, o_ref):
    na = a_ref.shape[1]
    y = _dot(a_ref[...].astype(BF16), w_ref[0:na, :])
    y = y + _dot(m_ref[...].astype(BF16), w_ref[na:, :])
    o_ref[...] = x_ref[...] + y


def _out_proj_kernel(x_ref, a_ref, m_ref, w_ref, o_ref):
    _out_proj_block(x_ref, a_ref, m_ref, w_ref, o_ref)


def _out_proj(x, a, m, w):
    t, d = x.shape
    tm = min(TOKEN_TILE, t)
    return pl.pallas_call(
        _out_proj_kernel,
        grid=(t // tm,),
        in_specs=[
            pl.BlockSpec((tm, d), lambda i: (i, 0)),
            pl.BlockSpec((tm, a.shape[1]), lambda i: (i, 0)),
            pl.BlockSpec((tm, m.shape[1]), lambda i: (i, 0)),
            pl.BlockSpec(w.shape, lambda i: (0, 0)),
        ],
        out_specs=pl.BlockSpec((tm, d), lambda i: (i, 0)),
        out_shape=jax.ShapeDtypeStruct((t, d), F32),
        compiler_params=_params(("parallel",)),
        name="out_proj",
    )(x, a, m, w)


def _out_proj_joined_kernel(x0_ref, a0_ref, m0_ref, x1_ref, a1_ref, m1_ref, w_ref, o_ref, *, n0):
    i = pl.program_id(0)

    @pl.when(i < n0)
    def _():
        _out_proj_block(x0_ref, a0_ref, m0_ref, w_ref, o_ref)

    @pl.when(i >= n0)
    def _():
        _out_proj_block(x1_ref, a1_ref, m1_ref, w_ref, o_ref)


def _out_proj_joined(first, second, w):
    (x0, a0, m0), (x1, a1, m1) = first, second
    d = x0.shape[1]
    tm = min(TOKEN_TILE, x0.shape[0], x1.shape[0])
    n0, n1 = x0.shape[0] // tm, x1.shape[0] // tm

    def group0(width):
        return pl.BlockSpec((tm, width), lambda i: (jnp.minimum(i, n0 - 1), 0))

    def group1(width):
        return pl.BlockSpec((tm, width), lambda i: (jnp.maximum(i - n0, 0), 0))

    return pl.pallas_call(
        functools.partial(_out_proj_joined_kernel, n0=n0),
        grid=(n0 + n1,),
        in_specs=[
            group0(d), group0(a0.shape[1]), group0(m0.shape[1]),
            group1(d), group1(a1.shape[1]), group1(m1.shape[1]),
            pl.BlockSpec(w.shape, lambda i: (0, 0)),
        ],
        out_specs=pl.BlockSpec((tm, d), lambda i: (i, 0)),
        out_shape=jax.ShapeDtypeStruct(((n0 + n1) * tm, d), F32),
        compiler_params=_params(("parallel",)),
        name="out_proj_joined",
    )(x0, a0, m0, x1, a1, m1, w)


def _ffn_kernel(x_ref, g_ref, wg_ref, wu_ref, wd_ref, o_ref, h_ref, acc_ref):
    f = pl.program_id(1)

    @pl.when(f == 0)
    def _():
        h_ref[...] = _rms(x_ref[...], g_ref[...]).astype(BF16)
        acc_ref[...] = jnp.zeros_like(acc_ref)

    h = h_ref[...]
    a = jax.nn.silu(_dot(h, wg_ref[...])) * _dot(h, wu_ref[...])
    acc_ref[...] += _dot(a.astype(BF16), wd_ref[...])

    @pl.when(f == pl.num_programs(1) - 1)
    def _():
        o_ref[...] = x_ref[...] + acc_ref[...]


def _ffn(x, g, w_gu, w_down):
    t, d = x.shape
    tm = min(TOKEN_TILE, t)
    tf = FFN_COLS
    nf = FFN_DIM // tf
    return pl.pallas_call(
        _ffn_kernel,
        grid=(t // tm, nf),
        in_specs=[
            pl.BlockSpec((tm, d), lambda i, f: (i, 0)),
            pl.BlockSpec((1, d), lambda i, f: (0, 0)),
            pl.BlockSpec((d, tf), lambda i, f: (0, f)),
            pl.BlockSpec((d, tf), lambda i, f: (0, nf + f)),
            pl.BlockSpec((tf, d), lambda i, f: (f, 0)),
        ],
        out_specs=pl.BlockSpec((tm, d), lambda i, f: (i, 0)),
        out_shape=jax.ShapeDtypeStruct((t, d), F32),
        scratch_shapes=[pltpu.VMEM((tm, d), BF16), pltpu.VMEM((tm, d), F32)],
        compiler_params=_params(("parallel", "arbitrary")),
        name="ffn",
    )(x, g.reshape(1, d), w_gu, w_gu, w_down)


def _router_kernel(x_ref, g_ref, wr_ref, r_ref):
    h = _rms(x_ref[...], g_ref[...])
    logits = jnp.dot(h, wr_ref[...], preferred_element_type=F32, precision=HIGHEST)
    lane = lax.broadcasted_iota(jnp.int32, logits.shape, 1).astype(F32)
    logits = jnp.where(lane < N_EXPERTS, logits, -jnp.inf)
    m1 = jnp.max(logits, axis=-1, keepdims=True)
    i1 = jnp.min(jnp.where(logits == m1, lane, float(LANES)), axis=-1, keepdims=True)
    rest = jnp.where(lane == i1, -jnp.inf, logits)
    m2 = jnp.max(rest, axis=-1, keepdims=True)
    i2 = jnp.min(jnp.where(rest == m2, lane, float(LANES)), axis=-1, keepdims=True)
    e2 = jnp.exp(m2 - m1)
    w1 = 1.0 / (1.0 + e2)
    w2 = e2 / (1.0 + e2)
    r = jnp.where(lane == 0, i1, 0.0)
    r = jnp.where(lane == 1, i2, r)
    r = jnp.where(lane == 2, w1, r)
    r_ref[...] = jnp.where(lane == 3, w2, r)


def _router(x, g, w_router_pad):
    t, d = x.shape
    tm = min(TOKEN_TILE, t)
    return pl.pallas_call(
        _router_kernel,
        grid=(t // tm,),
        in_specs=[
            pl.BlockSpec((tm, d), lambda i: (i, 0)),
            pl.BlockSpec((1, d), lambda i: (0, 0)),
            pl.BlockSpec((d, LANES), lambda i: (0, 0)),
        ],
        out_specs=pl.BlockSpec((tm, LANES), lambda i: (i, 0)),
        out_shape=jax.ShapeDtypeStruct((t, LANES), F32),
        compiler_params=_params(("parallel",)),
        name="router",
    )(x, g.reshape(1, d), w_router_pad)


def _start_row_gather(src_hbm, idx_ref, idx0, stride, dst, sem):
    def body(r, carry):
        row = idx_ref[idx0 + stride * r]
        pltpu.make_async_copy(src_hbm.at[pl.ds(row, 1)], dst.at[pl.ds(r, 1)], sem).start()
        return carry

    lax.fori_loop(0, dst.shape[0], body, 0, unroll=8)


def _wait_row_gather(src_hbm, dst, sem):
    pltpu.make_async_copy(src_hbm.at[pl.ds(0, dst.shape[0])], dst, sem).wait()


def _moe_kernel(te_ref, nt_ref, src_ref, x_hbm, g_ref, rw_ref, wg_ref, wu_ref, wd_ref, o_ref,
                xbuf, h_ref, acc_ref, sem):
    i = pl.program_id(0)
    f = pl.program_id(1)
    tm = h_ref.shape[0]
    nt = nt_ref[0]
    slot = i % 2

    @pl.when((i == 0) & (f == 0))
    def _():
        _start_row_gather(x_hbm, src_ref, 0, 1, xbuf.at[0], sem.at[0])

    nf = pl.num_programs(1)
    share = tm // EXPERT_STEPS

    def column_step(prefetch):
        if prefetch:
            for r in range(share):
                row = src_ref[(i + 1) * tm + f * share + r]
                pltpu.make_async_copy(
                    x_hbm.at[pl.ds(row, 1)], xbuf.at[1 - slot, pl.ds(f * share + r, 1)], sem.at[1 - slot]
                ).start()
        h = h_ref[...]
        a = jax.nn.silu(_dot(h, wg_ref[...])) * _dot(h, wu_ref[...])
        acc_ref[...] += _dot(a.astype(BF16), wd_ref[...])

    @pl.when(i < nt)
    def _():
        @pl.when(f == 0)
        def _():
            _wait_row_gather(x_hbm, xbuf.at[slot], sem.at[slot])
            h_ref[...] = _rms(xbuf[slot], g_ref[...]).astype(BF16)
            acc_ref[...] = jnp.zeros_like(acc_ref)

        @pl.when(i + 1 < nt)
        def _():
            column_step(True)

        @pl.when(i + 1 >= nt)
        def _():
            column_step(False)

        @pl.when(f == nf - 1)
        def _():
            o_ref[...] = acc_ref[...] * rw_ref[...]

    @pl.when((i >= nt) & (f == 0))
    def _():
        o_ref[...] = jnp.zeros_like(o_ref)


def _moe_ffn(x, g, src, roww, tile_expert, n_tiles, w_gu, w_down):
    d = x.shape[1]
    p = src.shape[0]
    tm = MOE_TILE
    tf = EXPERT_COLS
    nf = EXPERT_DIM // tf

    def row(i, f, te, nt, src):
        return (jnp.minimum(i, nt[0] - 1), 0)

    def col(i, f, nt):
        return jnp.where(i < nt[0], f, nf - 1)

    grid_spec = pltpu.PrefetchScalarGridSpec(
        num_scalar_prefetch=3,
        grid=(p // tm, nf),
        in_specs=[
            pl.BlockSpec(memory_space=pl.ANY),
            pl.BlockSpec((1, d), lambda i, f, te, nt, src: (0, 0)),
            pl.BlockSpec((tm, 1), row),
            pl.BlockSpec((None, d, tf), lambda i, f, te, nt, src: (te[i], 0, col(i, f, nt))),
            pl.BlockSpec((None, d, tf), lambda i, f, te, nt, src: (te[i], 0, nf + col(i, f, nt))),
            pl.BlockSpec((None, tf, d), lambda i, f, te, nt, src: (te[i], col(i, f, nt), 0)),
        ],
        out_specs=pl.BlockSpec((tm, d), lambda i, f, te, nt, src: (i, 0)),
        scratch_shapes=[
            pltpu.VMEM((2, tm, d), F32),
            pltpu.VMEM((tm, d), BF16),
            pltpu.VMEM((tm, d), F32),
            pltpu.SemaphoreType.DMA((2,)),
        ],
    )
    return pl.pallas_call(
        _moe_kernel,
        grid_spec=grid_spec,
        out_shape=jax.ShapeDtypeStruct((p, d), F32),
        compiler_params=_params(("arbitrary", "arbitrary")),
        name="moe_ffn",
    )(tile_expert, n_tiles, src, x, g.reshape(1, d), roww, w_gu, w_gu, w_down)


def _combine_norm_kernel(dest_ref, x_ref, y_hbm, g_ref, o_ref, buf, sem, *, tok0):
    i = pl.program_id(0)
    tm = x_ref.shape[0]
    slot = i % 2

    def fetch(tile, s):
        for k in range(2):
            _start_row_gather(y_hbm, dest_ref, 2 * (tok0 + tile * tm) + k, 2, buf.at[s, k], sem.at[s, k])

    @pl.when(i == 0)
    def _():
        fetch(0, 0)

    @pl.when(i + 1 < pl.num_programs(0))
    def _():
        fetch(i + 1, 1 - slot)

    for k in range(2):
        _wait_row_gather(y_hbm, buf.at[slot, k], sem.at[slot, k])
    o_ref[...] = _rms(x_ref[...] + (buf[slot, 0] + buf[slot, 1]), g_ref[...])


def _combine_norm(x, rows_out, dest, g, tok0, rows):
    d = x.shape[1]
    tm = min(TOKEN_TILE, rows)
    blk0 = tok0 // tm
    grid_spec = pltpu.PrefetchScalarGridSpec(
        num_scalar_prefetch=1,
        grid=(rows // tm,),
        in_specs=[
            pl.BlockSpec((tm, d), lambda i, dest: (i + blk0, 0)),
            pl.BlockSpec(memory_space=pl.ANY),
            pl.BlockSpec((1, d), lambda i, dest: (0, 0)),
        ],
        out_specs=pl.BlockSpec((tm, d), lambda i, dest: (i, 0)),
        scratch_shapes=[pltpu.VMEM((2, 2, tm, d), F32), pltpu.SemaphoreType.DMA((2, 2))],
    )
    return pl.pallas_call(
        functools.partial(_combine_norm_kernel, tok0=tok0),
        grid_spec=grid_spec,
        out_shape=jax.ShapeDtypeStruct((rows, d), F32),
        compiler_params=_params(("arbitrary",)),
        name="combine_norm",
    )(dest, x, rows_out, g.reshape(1, d))


def _head_softmax_pv(s, v):
    m = jnp.max(s, axis=-1, keepdims=True)
    p = jnp.exp(s - m)
    return _dot(p.astype(BF16), v) / jnp.sum(p, axis=-1, keepdims=True)


def _mem_attn_shared_kernel(q_ref, kv_ref, o_ref):
    q = q_ref[...]
    k = kv_ref[0, :, 0:MEM_WIDTH].astype(BF16)
    v = kv_ref[0, :, MEM_WIDTH:].astype(BF16)
    col = lax.broadcasted_iota(jnp.int32, (1, MEM_WIDTH), 1)
    acc = jnp.zeros(q.shape, F32)
    for h in range(MEM_HEADS):
        in_head = (col >= h * MEM_HEAD_DIM) & (col < (h + 1) * MEM_HEAD_DIM)
        qh = jnp.where(in_head, q, 0.0).astype(BF16)
        s = _dot_nt(qh, k) * MEM_HEAD_DIM ** -0.5
        acc = acc + jnp.where(in_head, _head_softmax_pv(s, v), 0.0)
    o_ref[...] = acc


def _mem_attn_shared(proj, q_col, mem_kv, seq_len):
    t = proj.shape[0]
    tq = min(TOKEN_TILE, seq_len)
    per_seq = seq_len // tq
    return pl.pallas_call(
        _mem_attn_shared_kernel,
        grid=(t // tq,),
        in_specs=[
            pl.BlockSpec((tq, MEM_WIDTH), lambda i: (i, q_col)),
            pl.BlockSpec((1, N_MEM, 2 * MEM_WIDTH), lambda i: (i // per_seq, 0, 0)),
        ],
        out_specs=pl.BlockSpec((tq, MEM_WIDTH), lambda i: (i, 0)),
        out_shape=jax.ShapeDtypeStruct((t, MEM_WIDTH), F32),
        compiler_params=_params(("parallel",)),
        name="mem_attn_shared",
    )(proj, mem_kv)


def _mem_attn_decode_kernel(q_ref, k_ref, v_ref, o_ref, *, seq_len):
    rows = MEM_HEADS * seq_len
    row_head = _div(lax.broadcasted_iota(jnp.int32, (rows, MEM_WIDTH), 0), seq_len)
    col_head = _div(lax.broadcasted_iota(jnp.int32, (rows, MEM_WIDTH), 1), MEM_HEAD_DIM)
    diag = row_head == col_head
    out_head = _div(lax.broadcasted_iota(jnp.int32, (seq_len, MEM_WIDTH), 1), MEM_HEAD_DIM)
    for b in range(k_ref.shape[0]):
        q = q_ref[b * seq_len:(b + 1) * seq_len, :]
        qd = jnp.where(diag, jnp.concatenate([q] * MEM_HEADS, axis=0), 0.0).astype(BF16)
        s = _dot_nt(qd, k_ref[b].astype(BF16)) * MEM_HEAD_DIM ** -0.5
        o = _head_softmax_pv(s, v_ref[b].astype(BF16))
        acc = jnp.zeros((seq_len, MEM_WIDTH), F32)
        for h in range(MEM_HEADS):
            acc = acc + jnp.where(out_head == h, o[h * seq_len:(h + 1) * seq_len], 0.0)
        o_ref[b * seq_len:(b + 1) * seq_len, :] = acc


def _mem_attn_decode(proj, q_col, mem_k, mem_v, seq_len):
    t = proj.shape[0]
    nb = mem_k.shape[0]
    bb = min(SEQ_BATCH, nb)
    return pl.pallas_call(
        functools.partial(_mem_attn_decode_kernel, seq_len=seq_len),
        grid=(nb // bb,),
        in_specs=[
            pl.BlockSpec((bb * seq_len, MEM_WIDTH), lambda i: (i, q_col)),
            pl.BlockSpec((bb, N_MEM, MEM_WIDTH), lambda i: (i, 0, 0)),
            pl.BlockSpec((bb, N_MEM, MEM_WIDTH), lambda i: (i, 0, 0)),
        ],
        out_specs=pl.BlockSpec((bb * seq_len, MEM_WIDTH), lambda i: (i, 0)),
        out_shape=jax.ShapeDtypeStruct((t, MEM_WIDTH), F32),
        compiler_params=_params(("parallel",)),
        name="mem_attn_decode",
    )(proj, mem_k, mem_v)


def _sink_softmax_pv(s, sink, v):
    m = jnp.maximum(jnp.max(s, axis=-1, keepdims=True), sink)
    p = jnp.exp(s - m)
    denom = jnp.sum(p, axis=-1, keepdims=True) + jnp.exp(sink - m)
    return _dot(p.astype(BF16), v) / denom


def _swa_prefill_kernel(slope_ref, sink_ref, q_ref, kp_ref, kc_ref, vp_ref, vc_ref, o_ref, *, blocks_per_seq):
    w = SWA_WINDOW
    first = (pl.program_id(0) % blocks_per_seq) == 0
    q = q_ref[...]
    k = jnp.concatenate([kp_ref[...], kc_ref[...]], axis=0)
    v = jnp.concatenate([vp_ref[...], vc_ref[...]], axis=0)
    qi = lax.broadcasted_iota(jnp.int32, (w, 2 * w), 0)
    kj = lax.broadcasted_iota(jnp.int32, (w, 2 * w), 1)
    dist = qi + w - kj
    valid = (dist >= 0) & (dist < w) & ((kj >= w) | jnp.logical_not(first))
    distf = dist.astype(F32)
    outs = []
    for kh in range(SWA_KV_HEADS):
        kk = k[:, kh * SWA_HEAD_DIM:(kh + 1) * SWA_HEAD_DIM].astype(BF16)
        vv = v[:, kh * SWA_HEAD_DIM:(kh + 1) * SWA_HEAD_DIM].astype(BF16)
        heads = [kh * SWA_GROUP + g for g in range(SWA_GROUP)]
        qg = jnp.concatenate([q[:, h * SWA_HEAD_DIM:(h + 1) * SWA_HEAD_DIM] for h in heads], axis=0).astype(BF16)
        s3 = _dot_nt(qg, kk) * SWA_HEAD_DIM ** -0.5
        for g, h in enumerate(heads):
            s = s3[g * w:(g + 1) * w] - slope_ref[h] * distf
            s = jnp.where(valid, s, NEG_INF)
            outs.append(_sink_softmax_pv(s, sink_ref[h], vv))
    o_ref[...] = jnp.concatenate(outs, axis=1)


def _swa_prefill(proj, slopes, sinks, seq_len):
    t = proj.shape[0]
    w = SWA_WINDOW
    per_seq = seq_len // w
    kcol = SWA_Q_WIDTH // SWA_KV_WIDTH
    vcol = kcol + 1

    def prev(i):
        return jnp.where(i % per_seq == 0, i, i - 1)

    smem = pl.BlockSpec(memory_space=pltpu.SMEM)
    return pl.pallas_call(
        functools.partial(_swa_prefill_kernel, blocks_per_seq=per_seq),
        grid=(t // w,),
        in_specs=[
            smem,
            smem,
            pl.BlockSpec((w, SWA_Q_WIDTH), lambda i: (i, 0)),
            pl.BlockSpec((w, SWA_KV_WIDTH), lambda i: (prev(i), kcol)),
            pl.BlockSpec((w, SWA_KV_WIDTH), lambda i: (i, kcol)),
            pl.BlockSpec((w, SWA_KV_WIDTH), lambda i: (prev(i), vcol)),
            pl.BlockSpec((w, SWA_KV_WIDTH), lambda i: (i, vcol)),
        ],
        out_specs=pl.BlockSpec((w, SWA_Q_WIDTH), lambda i: (i, 0)),
        out_shape=jax.ShapeDtypeStruct((t, SWA_Q_WIDTH), F32),
        compiler_params=_params(("parallel",)),
        name="swa_prefill",
    )(slopes, sinks, proj, proj, proj, proj, proj)


def _swa_decode_kernel(q_ref, kn_ref, vn_ref, kc_ref, vc_ref, rep_ref, slope_ref, sink_ref, o_ref, *, seq_len):
    w = SWA_WINDOW
    rows = SWA_HEADS * seq_len
    span = 2 * w
    row = lax.broadcasted_iota(jnp.int32, (rows, SWA_Q_WIDTH), 0)
    col = lax.broadcasted_iota(jnp.int32, (rows, SWA_Q_WIDTH), 1)
    diag = _div(row, seq_len) == _div(col, SWA_HEAD_DIM)
    ql = lax.broadcasted_iota(jnp.int32, (rows, span), 0) & (seq_len - 1)
    kj = lax.broadcasted_iota(jnp.int32, (rows, span), 1)
    dist = w + ql - kj
    valid = (dist >= 0) & (dist < w)
    bias = slope_ref[...] * dist.astype(F32)
    sink = sink_ref[...]
    out_head = _div(lax.broadcasted_iota(jnp.int32, (seq_len, SWA_Q_WIDTH), 1), SWA_HEAD_DIM)
    rep = rep_ref[...]
    tail = jnp.zeros((w - seq_len, SWA_KV_WIDTH), F32)
    for b in range(kc_ref.shape[0]):
        rs = slice(b * seq_len, (b + 1) * seq_len)
        k = jnp.concatenate([kc_ref[b], kn_ref[rs, :], tail], axis=0).astype(BF16)
        v = jnp.concatenate([vc_ref[b], vn_ref[rs, :], tail], axis=0).astype(BF16)
        k_rep = _dot(k, rep).astype(BF16)
        v_rep = _dot(v, rep).astype(BF16)
        qd = jnp.where(diag, jnp.concatenate([q_ref[rs, :]] * SWA_HEADS, axis=0), 0.0).astype(BF16)
        s = _dot_nt(qd, k_rep) * SWA_HEAD_DIM ** -0.5 - bias
        s = jnp.where(valid, s, NEG_INF)
        o = _sink_softmax_pv(s, sink, v_rep)
        acc = jnp.zeros((seq_len, SWA_Q_WIDTH), F32)
        for h in range(SWA_HEADS):
            acc = acc + jnp.where(out_head == h, o[h * seq_len:(h + 1) * seq_len], 0.0)
        o_ref[rs, :] = acc


def _swa_decode(proj, cache_k, cache_v, slopes, sinks, seq_len):
    t = proj.shape[0]
    nb = cache_k.shape[0]
    bb = min(SEQ_BATCH, nb)
    w = SWA_WINDOW
    kcol = SWA_Q_WIDTH // SWA_KV_WIDTH
    src = (jnp.arange(SWA_Q_WIDTH) // SWA_HEAD_DIM // SWA_GROUP) * SWA_HEAD_DIM + jnp.arange(SWA_Q_WIDTH) % SWA_HEAD_DIM
    rep = (jnp.arange(SWA_KV_WIDTH)[:, None] == src[None, :]).astype(BF16)
    slope_rows = jnp.repeat(slopes, seq_len).reshape(-1, 1)
    sink_rows = jnp.repeat(sinks, seq_len).reshape(-1, 1)
    rows = SWA_HEADS * seq_len
    return pl.pallas_call(
        functools.partial(_swa_decode_kernel, seq_len=seq_len),
        grid=(nb // bb,),
        in_specs=[
            pl.BlockSpec((bb * seq_len, SWA_Q_WIDTH), lambda i: (i, 0)),
            pl.BlockSpec((bb * seq_len, SWA_KV_WIDTH), lambda i: (i, kcol)),
            pl.BlockSpec((bb * seq_len, SWA_KV_WIDTH), lambda i: (i, kcol + 1)),
            pl.BlockSpec((bb, w, SWA_KV_WIDTH), lambda i: (i, 0, 0)),
            pl.BlockSpec((bb, w, SWA_KV_WIDTH), lambda i: (i, 0, 0)),
            pl.BlockSpec((SWA_KV_WIDTH, SWA_Q_WIDTH), lambda i: (0, 0)),
            pl.BlockSpec((rows, 1), lambda i: (0, 0)),
            pl.BlockSpec((rows, 1), lambda i: (0, 0)),
        ],
        out_specs=pl.BlockSpec((bb * seq_len, SWA_Q_WIDTH), lambda i: (i, 0)),
        out_shape=jax.ShapeDtypeStruct((t, SWA_Q_WIDTH), F32),
        compiler_params=_params(("parallel",)),
        name="swa_decode",
    )(proj, proj, proj, cache_k, cache_v, rep, slope_rows, sink_rows)


def _gdn_kernel(*refs, nblk, spb):
    r = GDN_ROWS
    c = r // spb
    qkv_refs = refs[0:nblk]
    z_refs = refs[nblk:2 * nblk]
    ba_refs = refs[2 * nblk:3 * nblk]
    conv0_ref, s0_ref, cw_ref, alog_ref, dtb_ref, gn_ref, o_ref, sfin_ref, s_scr, fbuf = refs[3 * nblk:]
    step = pl.program_id(1)
    nseq = nblk * spb
    pad = 8
    hist = GDN_CONV_K - 1

    @pl.when(step == 0)
    def _():
        s_scr[...] = s0_ref[...]
        for s in range(nseq):
            fbuf[s, pad - hist:pad, :] = conv0_ref[s]

    ri = lax.broadcasted_iota(jnp.int32, (r, r), 0)
    ci = lax.broadcasted_iota(jnp.int32, (r, r), 1)
    same = _div(ri, c) == _div(ci, c)
    tri = same & (ri >= ci)
    strict = same & (ri > ci)
    eye = (ri == ci).astype(F32)
    cum_mat = jnp.concatenate([tri.astype(F32), same.astype(F32)], axis=0)
    row_seq = _div(lax.broadcasted_iota(jnp.int32, (r, 1), 0), c)
    zeros_rr = jnp.zeros((r, LANES), F32)
    cw = cw_ref[...]
    neg_a = -jnp.exp(alog_ref[...])
    gn = gn_ref[...]

    blocks = []
    for n in range(nblk):
        u = qkv_refs[n][...]
        pieces = []
        for s in range(spb):
            idx = n * spb + s
            fbuf[idx, pad:pad + c, :] = u[s * c:(s + 1) * c]
            acc = fbuf[idx, pad - hist:pad - hist + c, :] * cw[0:1]
            for j in range(1, GDN_CONV_K):
                acc = acc + fbuf[idx, pad - hist + j:pad - hist + j + c, :] * cw[j:j + 1]
            fbuf[idx, pad - hist:pad, :] = fbuf[idx, pad + c - hist:pad + c, :]
            pieces.append(acc)
        conv = pieces[0] if spb == 1 else jnp.concatenate(pieces, axis=0)
        qkv = jax.nn.silu(conv)

        ba = ba_refs[n][...]
        beta_all = jax.nn.sigmoid(ba)
        xg = ba + dtb_ref[...]
        g_all = neg_a * (jnp.maximum(xg, 0.0) + jnp.log(1.0 + jnp.exp(-jnp.abs(xg))))
        gsum = jnp.dot(cum_mat, g_all, preferred_element_type=F32, precision=HIGHEST)
        gcum = gsum[0:r]
        gtot = gsum[r:2 * r]
        gcum_t = jnp.transpose(jnp.concatenate([gcum, zeros_rr], axis=0))

        blocks.append((qkv, beta_all, gcum, gtot, gcum_t))

    items = [(n, h) for n in range(nblk) for h in range(GDN_HEADS)]
    bf = lambda x: x.astype(BF16)
    k_n, kb_n, q_n, qk_dec, vb_n, kdec_t, decay_n, glast_n = [], [], [], [], [], [], [], []
    for n, h in items:
        qkv, beta_all, gcum, gtot, gcum_t = blocks[n]
        lo = h * GDN_DK
        q = qkv[:, lo:lo + GDN_DK]
        k = qkv[:, GDN_QK_WIDTH + lo:GDN_QK_WIDTH + lo + GDN_DK]
        v = qkv[:, 2 * GDN_QK_WIDTH + lo:2 * GDN_QK_WIDTH + lo + GDN_DV]
        q = q * lax.rsqrt(jnp.sum(q * q, axis=-1, keepdims=True) + L2_EPS) * GDN_DK ** -0.5
        k = k * lax.rsqrt(jnp.sum(k * k, axis=-1, keepdims=True) + L2_EPS)
        beta = beta_all[:, h:h + 1]
        gcol = gcum[:, GDN_HEADS + h:GDN_HEADS + h + 1]
        grow = gcum_t[GDN_HEADS + h:GDN_HEADS + h + 1, 0:r]
        glast = gtot[:, GDN_HEADS + h:GDN_HEADS + h + 1]
        eg = jnp.exp(gcol)
        kb = k * beta
        decay_n.append(jnp.where(tri, jnp.exp(jnp.where(tri, gcol - grow, 0.0)), 0.0))
        k_n.append(bf(k))
        kb_n.append(kb)
        q_n.append(q)
        vb_n.append(v * beta)
        qk_dec.append(bf(jnp.concatenate([kb * eg, q * eg], axis=0)))
        kdec = jnp.concatenate([k * jnp.exp(glast - gcol), zeros_rr], axis=0)
        kdec_t.append(bf(jnp.transpose(kdec)[:, 0:r]))
        glast_n.append(glast)

    kq = [_dot_nt(bf(jnp.concatenate([kb_n[i], q_n[i]], axis=0)), k_n[i]) for i in range(len(items))]
    power = [jnp.where(strict, kq[i][0:r] * decay_n[i], 0.0) for i in range(len(items))]
    qk = [bf(kq[i][r:2 * r] * decay_n[i]) for i in range(len(items))]
    inv = [eye - p for p in power]
    span = 2
    while span < c:
        power = [_dot(bf(p), bf(p)) for p in power]
        inv = [_dot(bf(a), bf(eye + p)) for a, p in zip(inv, power)]
        span *= 2
    inv = [bf(a) for a in inv]
    tq = [jnp.concatenate([a, bf(_dot(b, a))], axis=0) for a, b in zip(inv, qk)]

    if spb == 1:
        ks_qs = [_dot(qk_dec[i], bf(s_scr[n, h])) for i, (n, h) in enumerate(items)]
        resid = [vb_n[i] - ks_qs[i][0:r] for i in range(len(items))]
        qs = [x[r:2 * r] for x in ks_qs]
    else:
        resid, qs = [], []
        for i, (n, h) in enumerate(items):
            both = []
            for s in range(spb):
                rows = jnp.concatenate([qk_dec[i][s * c:(s + 1) * c], qk_dec[i][r + s * c:r + (s + 1) * c]], axis=0)
                both.append(_dot(rows, bf(s_scr[n * spb + s, h])))
            resid.append(vb_n[i] - jnp.concatenate([x[0:c] for x in both], axis=0))
            qs.append(jnp.concatenate([x[c:2 * c] for x in both], axis=0))

    vo = [_dot(tq[i], bf(resid[i])) for i in range(len(items))]
    for i, (n, h) in enumerate(items):
        v_new = vo[i][0:r]
        for s in range(spb):
            idx = n * spb + s
            vs = v_new if spb == 1 else jnp.where(row_seq == s, v_new, 0.0)
            carry = jnp.exp(glast_n[i][s * c:s * c + 1, :])
            s_scr[idx, h] = s_scr[idx, h] * carry + _dot(kdec_t[i], bf(vs))
    for i, (n, h) in enumerate(items):
        on = _rms(qs[i] + vo[i][r:2 * r], gn)
        zh = z_refs[n][:, h * GDN_DV:(h + 1) * GDN_DV]
        o_ref[n, :, h * GDN_DV:(h + 1) * GDN_DV] = on * jax.nn.silu(zh)

    @pl.when(step == pl.num_programs(1) - 1)
    def _():
        sfin_ref[...] = s_scr[...]


def _gdn(proj, conv_buf, s0, conv_w, a_log, dt_bias, norm_g, *, nblk, spb, steps):
    t = proj.shape[0]
    r = GDN_ROWS
    groups = t // (r * nblk * steps)
    nseq = nblk * spb
    c = r // spb
    z_col = GDN_CONV_DIM // GDN_V_WIDTH
    ba_col = (GDN_CONV_DIM + GDN_V_WIDTH + MEM_WIDTH) // LANES

    def rows(n, col):
        return lambda g, l: ((g * nblk + n) * steps + l, col)

    lane6 = jnp.zeros((1, LANES), F32)
    alog = lane6.at[0, GDN_HEADS:2 * GDN_HEADS].set(a_log)
    dtb = lane6.at[0, GDN_HEADS:2 * GDN_HEADS].set(dt_bias)
    const = lambda g, l: (0, 0)
    in_specs = (
        [pl.BlockSpec((r, GDN_CONV_DIM), rows(n, 0)) for n in range(nblk)]
        + [pl.BlockSpec((r, GDN_V_WIDTH), rows(n, z_col)) for n in range(nblk)]
        + [pl.BlockSpec((r, LANES), rows(n, ba_col)) for n in range(nblk)]
        + [
            pl.BlockSpec((nseq, GDN_CONV_K - 1, GDN_CONV_DIM), lambda g, l: (g, 0, 0)),
            pl.BlockSpec((nseq, GDN_HEADS, GDN_DK, GDN_DV), lambda g, l: (g, 0, 0, 0)),
            pl.BlockSpec((GDN_CONV_K, GDN_CONV_DIM), const),
            pl.BlockSpec((1, LANES), const),
            pl.BlockSpec((1, LANES), const),
            pl.BlockSpec((1, GDN_DV), const),
        ]
    )
    out, s_fin = pl.pallas_call(
        functools.partial(_gdn_kernel, nblk=nblk, spb=spb),
        grid=(groups, steps),
        in_specs=in_specs,
        out_specs=[
            pl.BlockSpec((nblk, r, GDN_V_WIDTH), lambda g, l: (0, g * steps + l, 0)),
            pl.BlockSpec((nseq, GDN_HEADS, GDN_DK, GDN_DV), lambda g, l: (g, 0, 0, 0)),
        ],
        out_shape=[
            jax.ShapeDtypeStruct((nblk, groups * steps * r, GDN_V_WIDTH), F32),
            jax.ShapeDtypeStruct((groups * nseq, GDN_HEADS, GDN_DK, GDN_DV), F32),
        ],
        scratch_shapes=[
            pltpu.VMEM((nseq, GDN_HEADS, GDN_DK, GDN_DV), F32),
            pltpu.VMEM((nseq, 8 + c, GDN_CONV_DIM), F32),
        ],
        compiler_params=_params(("parallel", "arbitrary")),
        name="gdn",
    )(*([proj] * (3 * nblk)), conv_buf, s0, conv_w, alog, dtb, norm_g.reshape(1, GDN_DV))
    return out.reshape(t, GDN_V_WIDTH), s_fin


def _dispatch(route, tm, p):
    t = route.shape[0]
    experts = route[:, 0:2].astype(jnp.int32).reshape(-1)
    weights = route[:, 2:4].reshape(-1)
    onehot = (experts[:, None] == jnp.arange(N_EXPERTS)[None, :]).astype(jnp.int32)
    csum = jnp.cumsum(onehot, axis=0)
    rank = jnp.take_along_axis(csum, experts[:, None], axis=1)[:, 0] - 1
    tiles = (csum[-1] + tm - 1) // tm
    tile_end = jnp.cumsum(tiles)
    start = (tile_end - tiles) * tm
    dest = start[experts] + rank
    n_tiles = tile_end[-1:].astype(jnp.int32)
    tile_expert = jnp.sum(tile_end[None, :] <= jnp.arange(p // tm)[:, None], axis=1)
    tile_expert = jnp.minimum(tile_expert, N_EXPERTS - 1).astype(jnp.int32)
    src = jnp.zeros((p,), jnp.int32).at[dest].set(jnp.arange(2 * t, dtype=jnp.int32) // 2)
    roww = jnp.zeros((p,), F32).at[dest].set(weights)
    return dest, src, roww.reshape(p, 1), tile_expert, n_tiles


def kernel(x_prompt, x_sample, state_gdn_conv, state_gdn_ssm, cache_swa_k, cache_swa_v, cache_mem_k, cache_mem_v, mem_prompt, attn_norm, ffn_norm, mem_norm, final_norm, w_in_gdn, gdn_conv_w, gdn_a_log, gdn_dt_bias, gdn_norm, w_out_gdn, w_in_swa, swa_sinks, w_out_swa, w_mem_kv, w_ffn_gu, w_ffn_down, w_router, w_exp_gu, w_exp_down):
    bp, lp, d = x_prompt.shape
    bs, ls, _ = x_sample.shape
    tp, ts = bp * lp, bs * ls
    xp = x_prompt.reshape(tp, d)
    xs = x_sample.reshape(ts, d)

    mem = mem_prompt.reshape(bp * N_MEM, d)
    mem_kv = [
        _norm_matmul(mem, mem_norm[i], w_mem_kv[i].astype(BF16), 2 * MEM_WIDTH).reshape(bp, N_MEM, 2 * MEM_WIDTH)
        for i in range(2)
    ]
    new_mem_k = jnp.stack([kv[..., :MEM_WIDTH].reshape(bp, N_MEM, MEM_HEADS, MEM_HEAD_DIM) for kv in mem_kv])
    new_mem_v = jnp.stack([kv[..., MEM_WIDTH:].reshape(bp, N_MEM, MEM_HEADS, MEM_HEAD_DIM) for kv in mem_kv])

    w_in = w_in_gdn[0]
    o_z = GDN_CONV_DIM + GDN_V_WIDTH
    o_mem = o_z + 2 * GDN_HEADS
    w0 = jnp.concatenate(
        [w_in[:, :o_z], w_in[:, o_mem:], w_in[:, o_z:o_mem], jnp.zeros((d, LANES - 2 * GDN_HEADS), F32)], axis=1
    ).astype(BF16)
    mq_col0 = o_z // MEM_WIDTH
    w_out0 = w_out_gdn[0].astype(BF16)
    w_gu0 = w_ffn_gu[0].astype(BF16)
    w_dn0 = w_ffn_down[0].astype(BF16)

    proj_p = _norm_matmul(xp, attn_norm[0], w0, w0.shape[1] // 3)
    proj_s = _norm_matmul(xs, attn_norm[0], w0, w0.shape[1] // 3)

    zero_conv = jnp.zeros((bp, GDN_CONV_K - 1, GDN_CONV_DIM), F32)
    zero_state = jnp.zeros((bp, GDN_HEADS, GDN_DK, GDN_DV), F32)
    gdn_p, ssm_p = _gdn(proj_p, zero_conv, zero_state, gdn_conv_w[0], gdn_a_log[0], gdn_dt_bias[0], gdn_norm[0],
                        nblk=bp, spb=1, steps=lp // GDN_ROWS)
    gdn_s, ssm_s = _gdn(proj_s, state_gdn_conv[0], state_gdn_ssm[0], gdn_conv_w[0], gdn_a_log[0], gdn_dt_bias[0],
                        gdn_norm[0], nblk=1, spb=GDN_ROWS // ls, steps=1)
    hist = GDN_CONV_K - 1
    conv_p = proj_p.reshape(bp, lp, -1)[:, lp - hist:, :GDN_CONV_DIM]
    conv_s = proj_s.reshape(bs, ls, -1)[:, ls - hist:, :GDN_CONV_DIM]

    memo_p = _mem_attn_shared(proj_p, mq_col0, mem_kv[0], lp)
    memo_s = _mem_attn_decode(proj_s, mq_col0, cache_mem_k[0].reshape(bs, N_MEM, MEM_WIDTH),
                              cache_mem_v[0].reshape(bs, N_MEM, MEM_WIDTH), ls)
    xp = _out_proj(xp, gdn_p, memo_p, w_out0)
    xs = _out_proj(xs, gdn_s, memo_s, w_out0)
    xp = _ffn(xp, ffn_norm[0], w_gu0, w_dn0)
    xs = _ffn(xs, ffn_norm[0], w_gu0, w_dn0)

    w1 = w_in_swa[0].astype(BF16)
    mq_col1 = (SWA_Q_WIDTH + 2 * SWA_KV_WIDTH) // MEM_WIDTH
    w_out1 = w_out_swa[0].astype(BF16)
    slopes = 2.0 ** (-8.0 * jnp.arange(1, SWA_HEADS + 1, dtype=F32) / SWA_HEADS)
    sinks = swa_sinks[0].astype(F32)

    proj_p = _norm_matmul(xp, attn_norm[1], w1, w1.shape[1] // 2)
    proj_s = _norm_matmul(xs, attn_norm[1], w1, w1.shape[1] // 2)
    swa_p = _swa_prefill(proj_p, slopes, sinks, lp)
    cache_k = cache_swa_k[0].reshape(bs, SWA_WINDOW, SWA_KV_WIDTH)
    cache_v = cache_swa_v[0].reshape(bs, SWA_WINDOW, SWA_KV_WIDTH)
    swa_s = _swa_decode(proj_s, cache_k, cache_v, slopes, sinks, ls)

    k0, v0 = SWA_Q_WIDTH, SWA_Q_WIDTH + SWA_KV_WIDTH
    pp = proj_p.reshape(bp, lp, -1)
    ps = proj_s.reshape(bs, ls, -1)
    kv_shape = (SWA_KV_HEADS, SWA_HEAD_DIM)
    swk_p = pp[:, lp - SWA_WINDOW:, k0:k0 + SWA_KV_WIDTH].reshape(bp, SWA_WINDOW, *kv_shape)
    swv_p = pp[:, lp - SWA_WINDOW:, v0:v0 + SWA_KV_WIDTH].reshape(bp, SWA_WINDOW, *kv_shape)
    swk_s = jnp.concatenate([cache_k[:, ls:], ps[:, :, k0:k0 + SWA_KV_WIDTH]], axis=1).reshape(bs, SWA_WINDOW, *kv_shape)
    swv_s = jnp.concatenate([cache_v[:, ls:], ps[:, :, v0:v0 + SWA_KV_WIDTH]], axis=1).reshape(bs, SWA_WINDOW, *kv_shape)

    memo_p = _mem_attn_shared(proj_p, mq_col1, mem_kv[1], lp)
    memo_s = _mem_attn_decode(proj_s, mq_col1, cache_mem_k[1].reshape(bs, N_MEM, MEM_WIDTH),
                              cache_mem_v[1].reshape(bs, N_MEM, MEM_WIDTH), ls)
    t_all = tp + ts
    x_all = _out_proj_joined((xp, swa_p, memo_p), (xs, swa_s, memo_s), w_out1)

    w_r = jnp.concatenate([w_router[0], jnp.zeros((d, LANES - N_EXPERTS), F32)], axis=1)
    route = _router(x_all, ffn_norm[1], w_r)
    tm = MOE_TILE
    p_rows = -(-(2 * t_all + N_EXPERTS * (tm - 1)) // tm) * tm
    dest, src, roww, tile_expert, n_tiles = _dispatch(route, tm, p_rows)
    rows_out = _moe_ffn(x_all, ffn_norm[1], src, roww, tile_expert, n_tiles,
                        w_exp_gu[0].astype(BF16), w_exp_down[0].astype(BF16))
    y_p = _combine_norm(x_all, rows_out, dest, final_norm, 0, tp)
    y_s = _combine_norm(x_all, rows_out, dest, final_norm, tp, ts)

    return (
        y_p.reshape(bp, lp, d),
        y_s.reshape(bs, ls, d),
        conv_p[None],
        ssm_p[None],
        swk_p[None],
        swv_p[None],
        new_mem_k,
        new_mem_v,
        conv_s[None],
        ssm_s[None],
        swk_s[None],
        swv_s[None],
    )
```

```python
import functools

import jax
import jax.numpy as jnp
from jax import lax
from jax.experimental import pallas as pl
from jax.experimental.pallas import tpu as pltpu

F32 = jnp.float32
BF16 = jnp.bfloat16
HIGHEST = lax.Precision.HIGHEST

D_MODEL = 1024
RMS_EPS = 1e-6
L2_EPS = 1e-6
NEG_INF = -1e30

GDN_HEADS = 6
GDN_DK = 128
GDN_DV = 128
GDN_CONV_K = 4
GDN_QK_WIDTH = GDN_HEADS * GDN_DK
GDN_V_WIDTH = GDN_HEADS * GDN_DV
GDN_CONV_DIM = 2 * GDN_QK_WIDTH + GDN_V_WIDTH
GDN_ROWS = 64

SWA_HEADS = 12
SWA_KV_HEADS = 4
SWA_HEAD_DIM = 64
SWA_GROUP = SWA_HEADS // SWA_KV_HEADS
SWA_WINDOW = 128
SWA_Q_WIDTH = SWA_HEADS * SWA_HEAD_DIM
SWA_KV_WIDTH = SWA_KV_HEADS * SWA_HEAD_DIM

N_MEM = 256
MEM_HEADS = 4
MEM_HEAD_DIM = 64
MEM_WIDTH = MEM_HEADS * MEM_HEAD_DIM

FFN_DIM = 2816
N_EXPERTS = 8
EXPERT_DIM = 3584

LANES = 128
VMEM_LIMIT = 56 * 1024 * 1024

TOKEN_TILE = 512
MOE_TILE = 512
FFN_COLS = 1408
EXPERT_STEPS = 2
EXPERT_COLS = EXPERT_DIM // EXPERT_STEPS
SEQ_BATCH = 8


def _params(sem):
    return pltpu.CompilerParams(dimension_semantics=sem, vmem_limit_bytes=VMEM_LIMIT)


def _rms(x, g):
    return x * lax.rsqrt(jnp.mean(x * x, axis=-1, keepdims=True) + RMS_EPS) * g


def _dot(a, b):
    return jnp.dot(a, b, preferred_element_type=F32)


def _div(x, d):
    assert d & (d - 1) == 0
    return lax.shift_right_logical(x, d.bit_length() - 1)


def _dot_nt(a, b):
    return lax.dot_general(a, b, (((1,), (1,)), ((), ())), preferred_element_type=F32)


def _norm_matmul_kernel(x_ref, g_ref, w_ref, o_ref):
    o_ref[...] = _dot(_rms(x_ref[...], g_ref[...]).astype(BF16), w_ref[...])


def _norm_matmul(x, g, w):
    t, k = x.shape
    n = w.shape[1]
    tm = min(TOKEN_TILE, t)
    return pl.pallas_call(
        _norm_matmul_kernel,
        grid=(t // tm,),
        in_specs=[
            pl.BlockSpec((tm, k), lambda i: (i, 0)),
            pl.BlockSpec((1, k), lambda i: (0, 0)),
            pl.BlockSpec((k, n), lambda i: (0, 0)),
        ],
        out_specs=pl.BlockSpec((tm, n), lambda i: (i, 0)),
        out_shape=jax.ShapeDtypeStruct((t, n), F32),
        compiler_params=_params(("parallel",)),
        name="norm_matmul",
    )(x, g.reshape(1, k), w)


def _out_proj_block(x_ref, a_ref, m_ref, w_ref, o_ref):
    na = a_ref.shape[1]
    y = _dot(a_ref[...].astype(BF16), w_ref[0:na, :])
    y = y + _dot(m_ref[...].astype(BF16), w_ref[na:, :])
    o_ref[...] = x_ref[...] + y


def _out_proj_kernel(x_ref, a_ref, m_ref, w_ref, o_ref):
    _out_proj_block(x_ref, a_ref, m_ref, w_ref, o_ref)


def _out_proj(x, a, m, w):
    t, d = x.shape
    tm = min(TOKEN_TILE, t)
    return pl.pallas_call(
        _out_proj_kernel,
        grid=(t // tm,),
        in_specs=[
            pl.BlockSpec((tm, d), lambda i: (i, 0)),
            pl.BlockSpec((tm, a.shape[1]), lambda i: (i, 0)),
            pl.BlockSpec((tm, m.shape[1]), lambda i: (i, 0)),
            pl.BlockSpec(w.shape, lambda i: (0, 0)),
        ],
        out_specs=pl.BlockSpec((tm, d), lambda i: (i, 0)),
        out_shape=jax.ShapeDtypeStruct((t, d), F32),
        compiler_params=_params(("parallel",)),
        name="out_proj",
    )(x, a, m, w)


def _out_proj_joined_kernel(x0_ref, a0_ref, m0_ref, x1_ref, a1_ref, m1_ref, w_ref, o_ref, *, n0):
    i = pl.program_id(0)

    @pl.when(i < n0)
    def _():
        _out_proj_block(x0_ref, a0_ref, m0_ref, w_ref, o_ref)

    @pl.when(i >= n0)
    def _():
        _out_proj_block(x1_ref, a1_ref, m1_ref, w_ref, o_ref)


def _out_proj_joined(first, second, w):
    (x0, a0, m0), (x1, a1, m1) = first, second
    d = x0.shape[1]
    tm = min(TOKEN_TILE, x0.shape[0], x1.shape[0])
    n0, n1 = x0.shape[0] // tm, x1.shape[0] // tm

    def group0(width):
        return pl.BlockSpec((tm, width), lambda i: (jnp.minimum(i, n0 - 1), 0))

    def group1(width):
        return pl.BlockSpec((tm, width), lambda i: (jnp.maximum(i - n0, 0), 0))

    return pl.pallas_call(
        functools.partial(_out_proj_joined_kernel, n0=n0),
        grid=(n0 + n1,),
        in_specs=[
            group0(d), group0(a0.shape[1]), group0(m0.shape[1]),
            group1(d), group1(a1.shape[1]), group1(m1.shape[1]),
            pl.BlockSpec(w.shape, lambda i: (0, 0)),
        ],
        out_specs=pl.BlockSpec((tm, d), lambda i: (i, 0)),
        out_shape=jax.ShapeDtypeStruct(((n0 + n1) * tm, d), F32),
        compiler_params=_params(("parallel",)),
        name="out_proj_joined",
    )(x0, a0, m0, x1, a1, m1, w)


def _ffn_kernel(x_ref, g_ref, wg_ref, wu_ref, wd_ref, o_ref, h_ref, acc_ref):
    f = pl.program_id(1)

    @pl.when(f == 0)
    def _():
        h_ref[...] = _rms(x_ref[...], g_ref[...]).astype(BF16)
        acc_ref[...] = jnp.zeros_like(acc_ref)

    h = h_ref[...]
    a = jax.nn.silu(_dot(h, wg_ref[...])) * _dot(h, wu_ref[...])
    acc_ref[...] += _dot(a.astype(BF16), wd_ref[...])

    @pl.when(f == pl.num_programs(1) - 1)
    def _():
        o_ref[...] = x_ref[...] + acc_ref[...]


def _ffn(x, g, w_gu, w_down):
    t, d = x.shape
    tm = min(TOKEN_TILE, t)
    tf = FFN_COLS
    nf = FFN_DIM // tf
    return pl.pallas_call(
        _ffn_kernel,
        grid=(t // tm, nf),
        in_specs=[
            pl.BlockSpec((tm, d), lambda i, f: (i, 0)),
            pl.BlockSpec((1, d), lambda i, f: (0, 0)),
            pl.BlockSpec((d, tf), lambda i, f: (0, f)),
            pl.BlockSpec((d, tf), lambda i, f: (0, nf + f)),
            pl.BlockSpec((tf, d), lambda i, f: (f, 0)),
        ],
        out_specs=pl.BlockSpec((tm, d), lambda i, f: (i, 0)),
        out_shape=jax.ShapeDtypeStruct((t, d), F32),
        scratch_shapes=[pltpu.VMEM((tm, d), BF16), pltpu.VMEM((tm, d), F32)],
        compiler_params=_params(("parallel", "arbitrary")),
        name="ffn",
    )(x, g.reshape(1, d), w_gu, w_gu, w_down)


def _router_kernel(x_ref, g_ref, wr_ref, r_ref):
    h = _rms(x_ref[...], g_ref[...])
    logits = jnp.dot(h, wr_ref[...], preferred_element_type=F32, precision=HIGHEST)
    lane = lax.broadcasted_iota(jnp.int32, logits.shape, 1).astype(F32)
    logits = jnp.where(lane < N_EXPERTS, logits, -jnp.inf)
    m1 = jnp.max(logits, axis=-1, keepdims=True)
    i1 = jnp.min(jnp.where(logits == m1, lane, float(LANES)), axis=-1, keepdims=True)
    rest = jnp.where(lane == i1, -jnp.inf, logits)
    m2 = jnp.max(rest, axis=-1, keepdims=True)
    i2 = jnp.min(jnp.where(rest == m2, lane, float(LANES)), axis=-1, keepdims=True)
    e2 = jnp.exp(m2 - m1)
    w1 = 1.0 / (1.0 + e2)
    w2 = e2 / (1.0 + e2)
    r = jnp.where(lane == 0, i1, 0.0)
    r = jnp.where(lane == 1, i2, r)
    r = jnp.where(lane == 2, w1, r)
    r_ref[...] = jnp.where(lane == 3, w2, r)


def _router(x, g, w_router_pad):
    t, d = x.shape
    tm = min(TOKEN_TILE, t)
    return pl.pallas_call(
        _router_kernel,
        grid=(t // tm,),
        in_specs=[
            pl.BlockSpec((tm, d), lambda i: (i, 0)),
            pl.BlockSpec((1, d), lambda i: (0, 0)),
            pl.BlockSpec((d, LANES), lambda i: (0, 0)),
        ],
        out_specs=pl.BlockSpec((tm, LANES), lambda i: (i, 0)),
        out_shape=jax.ShapeDtypeStruct((t, LANES), F32),
        compiler_params=_params(("parallel",)),
        name="router",
    )(x, g.reshape(1, d), w_router_pad)


def _start_row_gather(src_hbm, idx_ref, idx0, stride, dst, sem):
    def body(r, carry):
        row = idx_ref[idx0 + stride * r]
        pltpu.make_async_copy(src_hbm.at[pl.ds(row, 1)], dst.at[pl.ds(r, 1)], sem).start()
        return carry

    lax.fori_loop(0, dst.shape[0], body, 0, unroll=8)


def _wait_row_gather(src_hbm, dst, sem):
    pltpu.make_async_copy(src_hbm.at[pl.ds(0, dst.shape[0])], dst, sem).wait()


def _moe_kernel(te_ref, nt_ref, src_ref, x_hbm, g_ref, wg_ref, wu_ref, wd_ref, o_ref,
                xbuf, h_ref, acc_ref, sem):
    i = pl.program_id(0)
    f = pl.program_id(1)
    tm = h_ref.shape[0]
    nt = nt_ref[0]
    slot = i % 2

    @pl.when((i == 0) & (f == 0))
    def _():
        _start_row_gather(x_hbm, src_ref, 0, 1, xbuf.at[0], sem.at[0])

    nf = pl.num_programs(1)
    share = tm // EXPERT_STEPS

    def column_step(prefetch):
        if prefetch:
            for r in range(share):
                row = src_ref[(i + 1) * tm + f * share + r]
                pltpu.make_async_copy(
                    x_hbm.at[pl.ds(row, 1)], xbuf.at[1 - slot, pl.ds(f * share + r, 1)], sem.at[1 - slot]
                ).start()
        h = h_ref[...]
        a = jax.nn.silu(_dot(h, wg_ref[...])) * _dot(h, wu_ref[...])
        acc_ref[...] += _dot(a.astype(BF16), wd_ref[...])

    @pl.when(i < nt)
    def _():
        @pl.when(f == 0)
        def _():
            _wait_row_gather(x_hbm, xbuf.at[slot], sem.at[slot])
            h_ref[...] = _rms(xbuf[slot], g_ref[...]).astype(BF16)
            acc_ref[...] = jnp.zeros_like(acc_ref)

        @pl.when(i + 1 < nt)
        def _():
            column_step(True)

        @pl.when(i + 1 >= nt)
        def _():
            column_step(False)

        @pl.when(f == nf - 1)
        def _():
            o_ref[...] = acc_ref[...]

    @pl.when((i >= nt) & (f == 0))
    def _():
        o_ref[...] = jnp.zeros_like(o_ref)


def _moe_ffn(x, g, src, tile_expert, n_tiles, w_gu, w_down):
    d = x.shape[1]
    p = src.shape[0]
    tm = MOE_TILE
    tf = EXPERT_COLS
    nf = EXPERT_DIM // tf

    def col(i, f, nt):
        return jnp.where(i < nt[0], f, nf - 1)

    grid_spec = pltpu.PrefetchScalarGridSpec(
        num_scalar_prefetch=3,
        grid=(p // tm, nf),
        in_specs=[
            pl.BlockSpec(memory_space=pl.ANY),
            pl.BlockSpec((1, d), lambda i, f, te, nt, src: (0, 0)),
            pl.BlockSpec((None, d, tf), lambda i, f, te, nt, src: (te[i], 0, col(i, f, nt))),
            pl.BlockSpec((None, d, tf), lambda i, f, te, nt, src: (te[i], 0, nf + col(i, f, nt))),
            pl.BlockSpec((None, tf, d), lambda i, f, te, nt, src: (te[i], col(i, f, nt), 0)),
        ],
        out_specs=pl.BlockSpec((tm, d), lambda i, f, te, nt, src: (i, 0)),
        scratch_shapes=[
            pltpu.VMEM((2, tm, d), F32),
            pltpu.VMEM((tm, d), BF16),
            pltpu.VMEM((tm, d), F32),
            pltpu.SemaphoreType.DMA((2,)),
        ],
    )
    return pl.pallas_call(
        _moe_kernel,
        grid_spec=grid_spec,
        out_shape=jax.ShapeDtypeStruct((p, d), F32),
        compiler_params=_params(("arbitrary", "arbitrary")),
        name="moe_ffn",
    )(tile_expert, n_tiles, src, x, g.reshape(1, d), w_gu, w_gu, w_down)


def _combine_norm_kernel(dest_ref, x_ref, r_ref, y_hbm, g_ref, o_ref, buf, sem, *, tok0):
    i = pl.program_id(0)
    tm = x_ref.shape[0]
    slot = i % 2

    def fetch(tile, s):
        for k in range(2):
            _start_row_gather(y_hbm, dest_ref, 2 * (tok0 + tile * tm) + k, 2, buf.at[s, k], sem.at[s, k])

    @pl.when(i == 0)
    def _():
        fetch(0, 0)

    @pl.when(i + 1 < pl.num_programs(0))
    def _():
        fetch(i + 1, 1 - slot)

    for k in range(2):
        _wait_row_gather(y_hbm, buf.at[slot, k], sem.at[slot, k])
    moe = buf[slot, 0] * r_ref[:, 2:3] + buf[slot, 1] * r_ref[:, 3:4]
    o_ref[...] = _rms(x_ref[...] + moe, g_ref[...])


def _combine_norm(x, route, rows_out, dest, g, tok0, rows):
    d = x.shape[1]
    tm = min(TOKEN_TILE, rows)
    blk0 = tok0 // tm
    grid_spec = pltpu.PrefetchScalarGridSpec(
        num_scalar_prefetch=1,
        grid=(rows // tm,),
        in_specs=[
            pl.BlockSpec((tm, d), lambda i, dest: (i + blk0, 0)),
            pl.BlockSpec((tm, LANES), lambda i, dest: (i + blk0, 0)),
            pl.BlockSpec(memory_space=pl.ANY),
            pl.BlockSpec((1, d), lambda i, dest: (0, 0)),
        ],
        out_specs=pl.BlockSpec((tm, d), lambda i, dest: (i, 0)),
        scratch_shapes=[pltpu.VMEM((2, 2, tm, d), F32), pltpu.SemaphoreType.DMA((2, 2))],
    )
    return pl.pallas_call(
        functools.partial(_combine_norm_kernel, tok0=tok0),
        grid_spec=grid_spec,
        out_shape=jax.ShapeDtypeStruct((rows, d), F32),
        compiler_params=_params(("arbitrary",)),
        name="combine_norm",
    )(dest, x, route, rows_out, g.reshape(1, d))


def _head_softmax_pv(s, v):
    m = jnp.max(s, axis=-1, keepdims=True)
    p = jnp.exp(s - m)
    return _dot(p.astype(BF16), v) / jnp.sum(p, axis=-1, keepdims=True)


def _mem_attn_shared_kernel(q_ref, kv_ref, o_ref):
    q = q_ref[...]
    k = kv_ref[0, :, 0:MEM_WIDTH].astype(BF16)
    v = kv_ref[0, :, MEM_WIDTH:].astype(BF16)
    col = lax.broadcasted_iota(jnp.int32, (1, MEM_WIDTH), 1)
    acc = jnp.zeros(q.shape, F32)
    for h in range(MEM_HEADS):
        in_head = (col >= h * MEM_HEAD_DIM) & (col < (h + 1) * MEM_HEAD_DIM)
        qh = jnp.where(in_head, q, 0.0).astype(BF16)
        s = _dot_nt(qh, k) * MEM_HEAD_DIM ** -0.5
        acc = acc + jnp.where(in_head, _head_softmax_pv(s, v), 0.0)
    o_ref[...] = acc


def _mem_attn_shared(proj, q_col, mem_kv, seq_len):
    t = proj.shape[0]
    tq = min(TOKEN_TILE, seq_len)
    per_seq = seq_len // tq
    return pl.pallas_call(
        _mem_attn_shared_kernel,
        grid=(t // tq,),
        in_specs=[
            pl.BlockSpec((tq, MEM_WIDTH), lambda i: (i, q_col)),
            pl.BlockSpec((1, N_MEM, 2 * MEM_WIDTH), lambda i: (i // per_seq, 0, 0)),
        ],
        out_specs=pl.BlockSpec((tq, MEM_WIDTH), lambda i: (i, 0)),
        out_shape=jax.ShapeDtypeStruct((t, MEM_WIDTH), F32),
        compiler_params=_params(("parallel",)),
        name="mem_attn_shared",
    )(proj, mem_kv)


def _mem_attn_decode_kernel(q_ref, k_ref, v_ref, o_ref, *, seq_len):
    rows = MEM_HEADS * seq_len
    row_head = _div(lax.broadcasted_iota(jnp.int32, (rows, MEM_WIDTH), 0), seq_len)
    col_head = _div(lax.broadcasted_iota(jnp.int32, (rows, MEM_WIDTH), 1), MEM_HEAD_DIM)
    diag = row_head == col_head
    out_head = _div(lax.broadcasted_iota(jnp.int32, (seq_len, MEM_WIDTH), 1), MEM_HEAD_DIM)
    for b in range(k_ref.shape[0]):
        q = q_ref[b * seq_len:(b + 1) * seq_len, :]
        qd = jnp.where(diag, jnp.concatenate([q] * MEM_HEADS, axis=0), 0.0).astype(BF16)
        s = _dot_nt(qd, k_ref[b].astype(BF16)) * MEM_HEAD_DIM ** -0.5
        o = _head_softmax_pv(s, v_ref[b].astype(BF16))
        acc = jnp.zeros((seq_len, MEM_WIDTH), F32)
        for h in range(MEM_HEADS):
            acc = acc + jnp.where(out_head == h, o[h * seq_len:(h + 1) * seq_len], 0.0)
        o_ref[b * seq_len:(b + 1) * seq_len, :] = acc


def _mem_attn_decode(proj, q_col, mem_k, mem_v, seq_len):
    t = proj.shape[0]
    nb = mem_k.shape[0]
    bb = min(SEQ_BATCH, nb)
    return pl.pallas_call(
        functools.partial(_mem_attn_decode_kernel, seq_len=seq_len),
        grid=(nb // bb,),
        in_specs=[
            pl.BlockSpec((bb * seq_len, MEM_WIDTH), lambda i: (i, q_col)),
            pl.BlockSpec((bb, N_MEM, MEM_WIDTH), lambda i: (i, 0, 0)),
            pl.BlockSpec((bb, N_MEM, MEM_WIDTH), lambda i: (i, 0, 0)),
        ],
        out_specs=pl.BlockSpec((bb * seq_len, MEM_WIDTH), lambda i: (i, 0)),
        out_shape=jax.ShapeDtypeStruct((t, MEM_WIDTH), F32),
        compiler_params=_params(("parallel",)),
        name="mem_attn_decode",
    )(proj, mem_k, mem_v)


def _sink_softmax_pv(s, sink, v):
    m = jnp.maximum(jnp.max(s, axis=-1, keepdims=True), sink)
    p = jnp.exp(s - m)
    denom = jnp.sum(p, axis=-1, keepdims=True) + jnp.exp(sink - m)
    return _dot(p.astype(BF16), v) / denom


def _swa_prefill_kernel(slope_ref, sink_ref, q_ref, kp_ref, kc_ref, vp_ref, vc_ref, o_ref, *, blocks_per_seq):
    w = SWA_WINDOW
    first = (pl.program_id(0) % blocks_per_seq) == 0
    q = q_ref[...]
    k = jnp.concatenate([kp_ref[...], kc_ref[...]], axis=0)
    v = jnp.concatenate([vp_ref[...], vc_ref[...]], axis=0)
    qi = lax.broadcasted_iota(jnp.int32, (w, 2 * w), 0)
    kj = lax.broadcasted_iota(jnp.int32, (w, 2 * w), 1)
    dist = qi + w - kj
    valid = (dist >= 0) & (dist < w) & ((kj >= w) | jnp.logical_not(first))
    distf = dist.astype(F32)
    outs = []
    for kh in range(SWA_KV_HEADS):
        kk = k[:, kh * SWA_HEAD_DIM:(kh + 1) * SWA_HEAD_DIM].astype(BF16)
        vv = v[:, kh * SWA_HEAD_DIM:(kh + 1) * SWA_HEAD_DIM].astype(BF16)
        heads = [kh * SWA_GROUP + g for g in range(SWA_GROUP)]
        qg = jnp.concatenate([q[:, h * SWA_HEAD_DIM:(h + 1) * SWA_HEAD_DIM] for h in heads], axis=0).astype(BF16)
        s3 = _dot_nt(qg, kk) * SWA_HEAD_DIM ** -0.5
        for g, h in enumerate(heads):
            s = s3[g * w:(g + 1) * w] - slope_ref[h] * distf
            s = jnp.where(valid, s, NEG_INF)
            outs.append(_sink_softmax_pv(s, sink_ref[h], vv))
    o_ref[...] = jnp.concatenate(outs, axis=1)


def _swa_prefill(proj, slopes, sinks, seq_len):
    t = proj.shape[0]
    w = SWA_WINDOW
    per_seq = seq_len // w
    kcol = SWA_Q_WIDTH // SWA_KV_WIDTH
    vcol = kcol + 1

    def prev(i):
        return jnp.where(i % per_seq == 0, i, i - 1)

    smem = pl.BlockSpec(memory_space=pltpu.SMEM)
    return pl.pallas_call(
        functools.partial(_swa_prefill_kernel, blocks_per_seq=per_seq),
        grid=(t // w,),
        in_specs=[
            smem,
            smem,
            pl.BlockSpec((w, SWA_Q_WIDTH), lambda i: (i, 0)),
            pl.BlockSpec((w, SWA_KV_WIDTH), lambda i: (prev(i), kcol)),
            pl.BlockSpec((w, SWA_KV_WIDTH), lambda i: (i, kcol)),
            pl.BlockSpec((w, SWA_KV_WIDTH), lambda i: (prev(i), vcol)),
            pl.BlockSpec((w, SWA_KV_WIDTH), lambda i: (i, vcol)),
        ],
        out_specs=pl.BlockSpec((w, SWA_Q_WIDTH), lambda i: (i, 0)),
        out_shape=jax.ShapeDtypeStruct((t, SWA_Q_WIDTH), F32),
        compiler_params=_params(("parallel",)),
        name="swa_prefill",
    )(slopes, sinks, proj, proj, proj, proj, proj)


def _swa_decode_kernel(q_ref, kn_ref, vn_ref, kc_ref, vc_ref, rep_ref, slope_ref, sink_ref, o_ref, *, seq_len):
    w = SWA_WINDOW
    rows = SWA_HEADS * seq_len
    span = 2 * w
    row = lax.broadcasted_iota(jnp.int32, (rows, SWA_Q_WIDTH), 0)
    col = lax.broadcasted_iota(jnp.int32, (rows, SWA_Q_WIDTH), 1)
    diag = _div(row, seq_len) == _div(col, SWA_HEAD_DIM)
    ql = lax.broadcasted_iota(jnp.int32, (rows, span), 0) & (seq_len - 1)
    kj = lax.broadcasted_iota(jnp.int32, (rows, span), 1)
    dist = w + ql - kj
    valid = (dist >= 0) & (dist < w)
    bias = slope_ref[...] * dist.astype(F32)
    sink = sink_ref[...]
    out_head = _div(lax.broadcasted_iota(jnp.int32, (seq_len, SWA_Q_WIDTH), 1), SWA_HEAD_DIM)
    rep = rep_ref[...]
    tail = jnp.zeros((w - seq_len, SWA_KV_WIDTH), F32)
    for b in range(kc_ref.shape[0]):
        rs = slice(b * seq_len, (b + 1) * seq_len)
        k = jnp.concatenate([kc_ref[b], kn_ref[rs, :], tail], axis=0).astype(BF16)
        v = jnp.concatenate([vc_ref[b], vn_ref[rs, :], tail], axis=0).astype(BF16)
        k_rep = _dot(k, rep).astype(BF16)
        v_rep = _dot(v, rep).astype(BF16)
        qd = jnp.where(diag, jnp.concatenate([q_ref[rs, :]] * SWA_HEADS, axis=0), 0.0).astype(BF16)
        s = _dot_nt(qd, k_rep) * SWA_HEAD_DIM ** -0.5 - bias
        s = jnp.where(valid, s, NEG_INF)
        o = _sink_softmax_pv(s, sink, v_rep)
        acc = jnp.zeros((seq_len, SWA_Q_WIDTH), F32)
        for h in range(SWA_HEADS):
            acc = acc + jnp.where(out_head == h, o[h * seq_len:(h + 1) * seq_len], 0.0)
        o_ref[rs, :] = acc


def _swa_decode(proj, cache_k, cache_v, slopes, sinks, seq_len):
    t = proj.shape[0]
    nb = cache_k.shape[0]
    bb = min(SEQ_BATCH, nb)
    w = SWA_WINDOW
    kcol = SWA_Q_WIDTH // SWA_KV_WIDTH
    src = (jnp.arange(SWA_Q_WIDTH) // SWA_HEAD_DIM // SWA_GROUP) * SWA_HEAD_DIM + jnp.arange(SWA_Q_WIDTH) % SWA_HEAD_DIM
    rep = (jnp.arange(SWA_KV_WIDTH)[:, None] == src[None, :]).astype(BF16)
    slope_rows = jnp.repeat(slopes, seq_len).reshape(-1, 1)
    sink_rows = jnp.repeat(sinks, seq_len).reshape(-1, 1)
    rows = SWA_HEADS * seq_len
    return pl.pallas_call(
        functools.partial(_swa_decode_kernel, seq_len=seq_len),
        grid=(nb // bb,),
        in_specs=[
            pl.BlockSpec((bb * seq_len, SWA_Q_WIDTH), lambda i: (i, 0)),
            pl.BlockSpec((bb * seq_len, SWA_KV_WIDTH), lambda i: (i, kcol)),
            pl.BlockSpec((bb * seq_len, SWA_KV_WIDTH), lambda i: (i, kcol + 1)),
            pl.BlockSpec((bb, w, SWA_KV_WIDTH), lambda i: (i, 0, 0)),
            pl.BlockSpec((bb, w, SWA_KV_WIDTH), lambda i: (i, 0, 0)),
            pl.BlockSpec((SWA_KV_WIDTH, SWA_Q_WIDTH), lambda i: (0, 0)),
            pl.BlockSpec((rows, 1), lambda i: (0, 0)),
            pl.BlockSpec((rows, 1), lambda i: (0, 0)),
        ],
        out_specs=pl.BlockSpec((bb * seq_len, SWA_Q_WIDTH), lambda i: (i, 0)),
        out_shape=jax.ShapeDtypeStruct((t, SWA_Q_WIDTH), F32),
        compiler_params=_params(("parallel",)),
        name="swa_decode",
    )(proj, proj, proj, cache_k, cache_v, rep, slope_rows, sink_rows)


def _gdn_kernel(*refs, nblk, spb):
    r = GDN_ROWS
    c = r // spb
    qkv_refs = refs[0:nblk]
    z_refs = refs[nblk:2 * nblk]
    ba_refs = refs[2 * nblk:3 * nblk]
    conv0_ref, s0_ref, cw_ref, alog_ref, dtb_ref, gn_ref, o_ref, sfin_ref, s_scr, fbuf = refs[3 * nblk:]
    step = pl.program_id(1)
    nseq = nblk * spb
    tail = fbuf.shape[1]
    hist = GDN_CONV_K - 1

    @pl.when(step == 0)
    def _():
        s_scr[...] = s0_ref[...]
        fbuf[...] = jnp.zeros_like(fbuf)
        for s in range(nseq):
            fbuf[s, tail - hist:tail, :] = conv0_ref[s]

    ri = lax.broadcasted_iota(jnp.int32, (r, r), 0)
    ci = lax.broadcasted_iota(jnp.int32, (r, r), 1)
    same = _div(ri, c) == _div(ci, c)
    tri = same & (ri >= ci)
    strict = same & (ri > ci)
    eye = (ri == ci).astype(F32)
    cum_mat = jnp.concatenate([tri.astype(F32), same.astype(F32)], axis=0)
    row_seq = _div(lax.broadcasted_iota(jnp.int32, (r, 1), 0), c)
    row_tail = lax.broadcasted_iota(jnp.int32, (tail, 1), 0)
    zeros_rr = jnp.zeros((r, LANES), F32)
    cw = cw_ref[...]
    neg_a = -jnp.exp(alog_ref[...])
    gn = gn_ref[...]

    blocks = []
    for n in range(nblk):
        u = qkv_refs[n][...]
        pieces = []
        for s in range(spb):
            idx = n * spb + s
            us = u[s * c:(s + 1) * c]
            prev = fbuf[idx]
            acc = us * cw[hist:hist + 1]
            for back in range(1, GDN_CONV_K):
                moved = pltpu.roll(us, back, 0)
                head = jnp.where(row_tail < back, pltpu.roll(prev, back, 0), moved[0:tail])
                moved = head if c == tail else jnp.concatenate([head, moved[tail:]], axis=0)
                acc = acc + moved * cw[hist - back:hist - back + 1]
            fbuf[idx] = us[c - tail:c]
            pieces.append(acc)
        conv = pieces[0] if spb == 1 else jnp.concatenate(pieces, axis=0)
        qkv = jax.nn.silu(conv)

        ba = ba_refs[n][...]
        beta_all = jax.nn.sigmoid(ba)
        xg = ba + dtb_ref[...]
        g_all = neg_a * (jnp.maximum(xg, 0.0) + jnp.log(1.0 + jnp.exp(-jnp.abs(xg))))
        gsum = jnp.dot(cum_mat, g_all, preferred_element_type=F32, precision=HIGHEST)
        gcum = gsum[0:r]
        gtot = gsum[r:2 * r]
        gcum_t = jnp.transpose(jnp.concatenate([gcum, zeros_rr], axis=0))

        blocks.append((qkv, beta_all, gcum, gtot, gcum_t))

    items = [(n, h) for n in range(nblk) for h in range(GDN_HEADS)]
    bf = lambda x: x.astype(BF16)
    k_n, kb_n, q_n, qk_dec, vb_n, kdec_t, decay_n, glast_n = [], [], [], [], [], [], [], []
    for n, h in items:
        qkv, beta_all, gcum, gtot, gcum_t = blocks[n]
        lo = h * GDN_DK
        q = qkv[:, lo:lo + GDN_DK]
        k = qkv[:, GDN_QK_WIDTH + lo:GDN_QK_WIDTH + lo + GDN_DK]
        v = qkv[:, 2 * GDN_QK_WIDTH + lo:2 * GDN_QK_WIDTH + lo + GDN_DV]
        q = q * lax.rsqrt(jnp.sum(q * q, axis=-1, keepdims=True) + L2_EPS) * GDN_DK ** -0.5
        k = k * lax.rsqrt(jnp.sum(k * k, axis=-1, keepdims=True) + L2_EPS)
        beta = beta_all[:, h:h + 1]
        gcol = gcum[:, GDN_HEADS + h:GDN_HEADS + h + 1]
        grow = gcum_t[GDN_HEADS + h:GDN_HEADS + h + 1, 0:r]
        glast = gtot[:, GDN_HEADS + h:GDN_HEADS + h + 1]
        eg = jnp.exp(gcol)
        kb = k * beta
        decay_n.append(jnp.where(tri, jnp.exp(jnp.where(tri, gcol - grow, 0.0)), 0.0))
        k_n.append(bf(k))
        kb_n.append(kb)
        q_n.append(q)
        vb_n.append(v * beta)
        qk_dec.append(bf(jnp.concatenate([kb * eg, q * eg], axis=0)))
        kdec = jnp.concatenate([k * jnp.exp(glast - gcol), zeros_rr], axis=0)
        kdec_t.append(bf(jnp.transpose(kdec)[:, 0:r]))
        glast_n.append(glast)

    kq = [_dot_nt(bf(jnp.concatenate([kb_n[i], q_n[i]], axis=0)), k_n[i]) for i in range(len(items))]
    power = [jnp.where(strict, kq[i][0:r] * decay_n[i], 0.0) for i in range(len(items))]
    qk = [bf(kq[i][r:2 * r] * decay_n[i]) for i in range(len(items))]
    inv = [eye - p for p in power]
    span = 2
    while span < c:
        power = [_dot(bf(p), bf(p)) for p in power]
        inv = [_dot(bf(a), bf(eye + p)) for a, p in zip(inv, power)]
        span *= 2
    inv = [bf(a) for a in inv]
    tq = [jnp.concatenate([a, bf(_dot(b, a))], axis=0) for a, b in zip(inv, qk)]

    if spb == 1:
        ks_qs = [_dot(qk_dec[i], bf(s_scr[n, h])) for i, (n, h) in enumerate(items)]
        resid = [vb_n[i] - ks_qs[i][0:r] for i in range(len(items))]
        qs = [x[r:2 * r] for x in ks_qs]
    else:
        resid, qs = [], []
        for i, (n, h) in enumerate(items):
            both = []
            for s in range(spb):
                rows = jnp.concatenate([qk_dec[i][s * c:(s + 1) * c], qk_dec[i][r + s * c:r + (s + 1) * c]], axis=0)
                both.append(_dot(rows, bf(s_scr[n * spb + s, h])))
            resid.append(vb_n[i] - jnp.concatenate([x[0:c] for x in both], axis=0))
            qs.append(jnp.concatenate([x[c:2 * c] for x in both], axis=0))

    vo = [_dot(tq[i], bf(resid[i])) for i in range(len(items))]
    for i, (n, h) in enumerate(items):
        v_new = vo[i][0:r]
        for s in range(spb):
            idx = n * spb + s
            vs = v_new if spb == 1 else jnp.where(row_seq == s, v_new, 0.0)
            carry = jnp.exp(glast_n[i][s * c:s * c + 1, :])
            s_scr[idx, h] = s_scr[idx, h] * carry + _dot(kdec_t[i], bf(vs))
    for i, (n, h) in enumerate(items):
        on = _rms(qs[i] + vo[i][r:2 * r], gn)
        zh = z_refs[n][:, h * GDN_DV:(h + 1) * GDN_DV]
        o_ref[n, :, h * GDN_DV:(h + 1) * GDN_DV] = on * jax.nn.silu(zh)

    @pl.when(step == pl.num_programs(1) - 1)
    def _():
        sfin_ref[...] = s_scr[...]


def _gdn(proj, conv_buf, s0, conv_w, a_log, dt_bias, norm_g, *, nblk, spb, steps):
    t = proj.shape[0]
    r = GDN_ROWS
    groups = t // (r * nblk * steps)
    nseq = nblk * spb
    c = r // spb
    z_col = GDN_CONV_DIM // GDN_V_WIDTH
    ba_col = (GDN_CONV_DIM + GDN_V_WIDTH + MEM_WIDTH) // LANES

    def rows(n, col):
        return lambda g, l: ((g * nblk + n) * steps + l, col)

    lane6 = jnp.zeros((1, LANES), F32)
    alog = lane6.at[0, GDN_HEADS:2 * GDN_HEADS].set(a_log)
    dtb = lane6.at[0, GDN_HEADS:2 * GDN_HEADS].set(dt_bias)
    const = lambda g, l: (0, 0)
    in_specs = (
        [pl.BlockSpec((r, GDN_CONV_DIM), rows(n, 0)) for n in range(nblk)]
        + [pl.BlockSpec((r, GDN_V_WIDTH), rows(n, z_col)) for n in range(nblk)]
        + [pl.BlockSpec((r, LANES), rows(n, ba_col)) for n in range(nblk)]
        + [
            pl.BlockSpec((nseq, GDN_CONV_K - 1, GDN_CONV_DIM), lambda g, l: (g, 0, 0)),
            pl.BlockSpec((nseq, GDN_HEADS, GDN_DK, GDN_DV), lambda g, l: (g, 0, 0, 0)),
            pl.BlockSpec((GDN_CONV_K, GDN_CONV_DIM), const),
            pl.BlockSpec((1, LANES), const),
            pl.BlockSpec((1, LANES), const),
            pl.BlockSpec((1, GDN_DV), const),
        ]
    )
    out, s_fin = pl.pallas_call(
        functools.partial(_gdn_kernel, nblk=nblk, spb=spb),
        grid=(groups, steps),
        in_specs=in_specs,
        out_specs=[
            pl.BlockSpec((nblk, r, GDN_V_WIDTH), lambda g, l: (0, g * steps + l, 0)),
            pl.BlockSpec((nseq, GDN_HEADS, GDN_DK, GDN_DV), lambda g, l: (g, 0, 0, 0)),
        ],
        out_shape=[
            jax.ShapeDtypeStruct((nblk, groups * steps * r, GDN_V_WIDTH), F32),
            jax.ShapeDtypeStruct((groups * nseq, GDN_HEADS, GDN_DK, GDN_DV), F32),
        ],
        scratch_shapes=[
            pltpu.VMEM((nseq, GDN_HEADS, GDN_DK, GDN_DV), F32),
            pltpu.VMEM((nseq, 8, GDN_CONV_DIM), F32),
        ],
        compiler_params=_params(("parallel", "arbitrary")),
        name="gdn",
    )(*([proj] * (3 * nblk)), conv_buf, s0, conv_w, alog, dtb, norm_g.reshape(1, GDN_DV))
    return out.reshape(t, GDN_V_WIDTH), s_fin


def _dispatch(route, tm, p):
    t = route.shape[0]
    experts = route[:, 0:2].astype(jnp.int32).reshape(-1)
    onehot = (experts[:, None] == jnp.arange(N_EXPERTS)[None, :]).astype(jnp.int32)
    csum = jnp.cumsum(onehot, axis=0)
    rank = jnp.take_along_axis(csum, experts[:, None], axis=1)[:, 0] - 1
    tiles = (csum[-1] + tm - 1) // tm
    tile_end = jnp.cumsum(tiles)
    start = (tile_end - tiles) * tm
    dest = start[experts] + rank
    n_tiles = tile_end[-1:].astype(jnp.int32)
    tile_expert = jnp.sum(tile_end[None, :] <= jnp.arange(p // tm)[:, None], axis=1)
    tile_expert = jnp.minimum(tile_expert, N_EXPERTS - 1).astype(jnp.int32)
    src = jnp.zeros((p,), jnp.int32).at[dest].set(jnp.arange(2 * t, dtype=jnp.int32) // 2)
    return dest, src, tile_expert, n_tiles


def kernel(x_prompt, x_sample, state_gdn_conv, state_gdn_ssm, cache_swa_k, cache_swa_v, cache_mem_k, cache_mem_v, mem_prompt, attn_norm, ffn_norm, mem_norm, final_norm, w_in_gdn, gdn_conv_w, gdn_a_log, gdn_dt_bias, gdn_norm, w_out_gdn, w_in_swa, swa_sinks, w_out_swa, w_mem_kv, w_ffn_gu, w_ffn_down, w_router, w_exp_gu, w_exp_down):
    bp, lp, d = x_prompt.shape
    bs, ls, _ = x_sample.shape
    tp, ts = bp * lp, bs * ls
    xp = x_prompt.reshape(tp, d)
    xs = x_sample.reshape(ts, d)

    mem = mem_prompt.reshape(bp * N_MEM, d)
    mem_kv = [
        _norm_matmul(mem, mem_norm[i], w_mem_kv[i].astype(BF16)).reshape(bp, N_MEM, 2 * MEM_WIDTH)
        for i in range(2)
    ]
    new_mem_k = jnp.stack([kv[..., :MEM_WIDTH].reshape(bp, N_MEM, MEM_HEADS, MEM_HEAD_DIM) for kv in mem_kv])
    new_mem_v = jnp.stack([kv[..., MEM_WIDTH:].reshape(bp, N_MEM, MEM_HEADS, MEM_HEAD_DIM) for kv in mem_kv])

    w_in = w_in_gdn[0]
    o_z = GDN_CONV_DIM + GDN_V_WIDTH
    o_mem = o_z + 2 * GDN_HEADS
    w0 = jnp.concatenate(
        [w_in[:, :o_z], w_in[:, o_mem:], w_in[:, o_z:o_mem], jnp.zeros((d, LANES - 2 * GDN_HEADS), F32)], axis=1
    ).astype(BF16)
    mq_col0 = o_z // MEM_WIDTH
    w_out0 = w_out_gdn[0].astype(BF16)
    w_gu0 = w_ffn_gu[0].astype(BF16)
    w_dn0 = w_ffn_down[0].astype(BF16)

    proj_p = _norm_matmul(xp, attn_norm[0], w0)
    proj_s = _norm_matmul(xs, attn_norm[0], w0)

    zero_conv = jnp.zeros((bp, GDN_CONV_K - 1, GDN_CONV_DIM), F32)
    zero_state = jnp.zeros((bp, GDN_HEADS, GDN_DK, GDN_DV), F32)
    gdn_p, ssm_p = _gdn(proj_p, zero_conv, zero_state, gdn_conv_w[0], gdn_a_log[0], gdn_dt_bias[0], gdn_norm[0],
                        nblk=bp, spb=1, steps=lp // GDN_ROWS)
    gdn_s, ssm_s = _gdn(proj_s, state_gdn_conv[0], state_gdn_ssm[0], gdn_conv_w[0], gdn_a_log[0], gdn_dt_bias[0],
                        gdn_norm[0], nblk=1, spb=GDN_ROWS // ls, steps=1)
    hist = GDN_CONV_K - 1
    conv_p = proj_p.reshape(bp, lp, -1)[:, lp - hist:, :GDN_CONV_DIM]
    conv_s = proj_s.reshape(bs, ls, -1)[:, ls - hist:, :GDN_CONV_DIM]

    memo_p = _mem_attn_shared(proj_p, mq_col0, mem_kv[0], lp)
    memo_s = _mem_attn_decode(proj_s, mq_col0, cache_mem_k[0].reshape(bs, N_MEM, MEM_WIDTH),
                              cache_mem_v[0].reshape(bs, N_MEM, MEM_WIDTH), ls)
    xp = _out_proj(xp, gdn_p, memo_p, w_out0)
    xs = _out_proj(xs, gdn_s, memo_s, w_out0)
    xp = _ffn(xp, ffn_norm[0], w_gu0, w_dn0)
    xs = _ffn(xs, ffn_norm[0], w_gu0, w_dn0)

    w1 = w_in_swa[0].astype(BF16)
    mq_col1 = (SWA_Q_WIDTH + 2 * SWA_KV_WIDTH) // MEM_WIDTH
    w_out1 = w_out_swa[0].astype(BF16)
    slopes = 2.0 ** (-8.0 * jnp.arange(1, SWA_HEADS + 1, dtype=F32) / SWA_HEADS)
    sinks = swa_sinks[0].astype(F32)

    proj_p = _norm_matmul(xp, attn_norm[1], w1)
    proj_s = _norm_matmul(xs, attn_norm[1], w1)
    swa_p = _swa_prefill(proj_p, slopes, sinks, lp)
    cache_k = cache_swa_k[0].reshape(bs, SWA_WINDOW, SWA_KV_WIDTH)
    cache_v = cache_swa_v[0].reshape(bs, SWA_WINDOW, SWA_KV_WIDTH)
    swa_s = _swa_decode(proj_s, cache_k, cache_v, slopes, sinks, ls)

    k0, v0 = SWA_Q_WIDTH, SWA_Q_WIDTH + SWA_KV_WIDTH
    pp = proj_p.reshape(bp, lp, -1)
    ps = proj_s.reshape(bs, ls, -1)
    kv_shape = (SWA_KV_HEADS, SWA_HEAD_DIM)
    swk_p = pp[:, lp - SWA_WINDOW:, k0:k0 + SWA_KV_WIDTH].reshape(bp, SWA_WINDOW, *kv_shape)
    swv_p = pp[:, lp - SWA_WINDOW:, v0:v0 + SWA_KV_WIDTH].reshape(bp, SWA_WINDOW, *kv_shape)
    swk_s = jnp.concatenate([cache_k[:, ls:], ps[:, :, k0:k0 + SWA_KV_WIDTH]], axis=1).reshape(bs, SWA_WINDOW, *kv_shape)
    swv_s = jnp.concatenate([cache_v[:, ls:], ps[:, :, v0:v0 + SWA_KV_WIDTH]], axis=1).reshape(bs, SWA_WINDOW, *kv_shape)

    memo_p = _mem_attn_shared(proj_p, mq_col1, mem_kv[1], lp)
    memo_s = _mem_attn_decode(proj_s, mq_col1, cache_mem_k[1].reshape(bs, N_MEM, MEM_WIDTH),
                              cache_mem_v[1].reshape(bs, N_MEM, MEM_WIDTH), ls)
    t_all = tp + ts
    x_all = _out_proj_joined((xp, swa_p, memo_p), (xs, swa_s, memo_s), w_out1)

    w_r = jnp.concatenate([w_router[0], jnp.zeros((d, LANES - N_EXPERTS), F32)], axis=1)
    route = _router(x_all, ffn_norm[1], w_r)
    tm = MOE_TILE
    p_rows = -(-(2 * t_all + N_EXPERTS * (tm - 1)) // tm) * tm
    dest, src, tile_expert, n_tiles = _dispatch(route, tm, p_rows)
    rows_out = _moe_ffn(x_all, ffn_norm[1], src, tile_expert, n_tiles,
                        w_exp_gu[0].astype(BF16), w_exp_down[0].astype(BF16))
    y_p = _combine_norm(x_all, route, rows_out, dest, final_norm, 0, tp)
    y_s = _combine_norm(x_all, route, rows_out, dest, final_norm, tp, ts)

    return (
        y_p.reshape(bp, lp, d),
        y_s.reshape(bs, ls, d),
        conv_p[None],
        ssm_p[None],
        swk_p[None],
        swv_p[None],
        new_mem_k,
        new_mem_v,
        conv_s[None],
        ssm_s[None],
        swk_s[None],
        swv_s[None],
    )
```

```python
import functools

import jax
import jax.numpy as jnp
from jax import lax
from jax.experimental import pallas as pl
from jax.experimental.pallas import tpu as pltpu

F32 = jnp.float32
BF16 = jnp.bfloat16
HIGHEST = lax.Precision.HIGHEST

D_MODEL = 1024
RMS_EPS = 1e-6
L2_EPS = 1e-6
NEG_INF = -1e30

GDN_HEADS = 6
GDN_DK = 128
GDN_DV = 128
GDN_CONV_K = 4
GDN_QK_WIDTH = GDN_HEADS * GDN_DK
GDN_V_WIDTH = GDN_HEADS * GDN_DV
GDN_CONV_DIM = 2 * GDN_QK_WIDTH + GDN_V_WIDTH
GDN_ROWS = 64

SWA_HEADS = 12
SWA_KV_HEADS = 4
SWA_HEAD_DIM = 64
SWA_GROUP = SWA_HEADS // SWA_KV_HEADS
SWA_WINDOW = 128
SWA_Q_WIDTH = SWA_HEADS * SWA_HEAD_DIM
SWA_KV_WIDTH = SWA_KV_HEADS * SWA_HEAD_DIM

N_MEM = 256
MEM_HEADS = 4
MEM_HEAD_DIM = 64
MEM_WIDTH = MEM_HEADS * MEM_HEAD_DIM

FFN_DIM = 2816
N_EXPERTS = 8
EXPERT_DIM = 3584

LANES = 128
VMEM_LIMIT = 56 * 1024 * 1024

TOKEN_TILE = 512
MOE_TILE = 512
FFN_COLS = 1408
EXPERT_STEPS = 2
EXPERT_COLS = EXPERT_DIM // EXPERT_STEPS
SEQ_BATCH = 8
SWA_QUERY_BLOCKS = 1


def _params(sem):
    return pltpu.CompilerParams(dimension_semantics=sem, vmem_limit_bytes=VMEM_LIMIT)


def _rms(x, g):
    return x * lax.rsqrt(jnp.mean(x * x, axis=-1, keepdims=True) + RMS_EPS) * g


def _dot(a, b):
    return jnp.dot(a, b, preferred_element_type=F32)


def _div(x, d):
    assert d & (d - 1) == 0
    return lax.shift_right_logical(x, d.bit_length() - 1)


def _dot_nt(a, b):
    return lax.dot_general(a, b, (((1,), (1,)), ((), ())), preferred_element_type=F32)


def _norm_matmul_kernel(x_ref, g_ref, w_ref, o_ref):
    o_ref[...] = _dot(_rms(x_ref[...], g_ref[...]).astype(BF16), w_ref[...])


def _norm_matmul(x, g, w):
    t, k = x.shape
    n = w.shape[1]
    tm = min(TOKEN_TILE, t)
    return pl.pallas_call(
        _norm_matmul_kernel,
        grid=(t // tm,),
        in_specs=[
            pl.BlockSpec((tm, k), lambda i: (i, 0)),
            pl.BlockSpec((1, k), lambda i: (0, 0)),
            pl.BlockSpec((k, n), lambda i: (0, 0)),
        ],
        out_specs=pl.BlockSpec((tm, n), lambda i: (i, 0)),
        out_shape=jax.ShapeDtypeStruct((t, n), F32),
        compiler_params=_params(("parallel",)),
        name="norm_matmul",
    )(x, g.reshape(1, k), w)


def _out_proj_block(x_ref, a_ref, m_ref, w_ref, o_ref):
    na = a_ref.shape[1]
    y = _dot(a_ref[...].astype(BF16), w_ref[0:na, :])
    y = y + _dot(m_ref[...].astype(BF16), w_ref[na:, :])
    o_ref[...] = x_ref[...] + y


def _route(h, w_router):
    logits = jnp.dot(h, w_router, preferred_element_type=F32, precision=HIGHEST)
    lane = lax.broadcasted_iota(jnp.int32, logits.shape, 1).astype(F32)
    logits = jnp.where(lane < N_EXPERTS, logits, -jnp.inf)
    m1 = jnp.max(logits, axis=-1, keepdims=True)
    i1 = jnp.min(jnp.where(logits == m1, lane, float(LANES)), axis=-1, keepdims=True)
    rest = jnp.where(lane == i1, -jnp.inf, logits)
    m2 = jnp.max(rest, axis=-1, keepdims=True)
    i2 = jnp.min(jnp.where(rest == m2, lane, float(LANES)), axis=-1, keepdims=True)
    e2 = jnp.exp(m2 - m1)
    w1 = 1.0 / (1.0 + e2)
    w2 = e2 / (1.0 + e2)
    r = jnp.where(lane == 0, i1, 0.0)
    r = jnp.where(lane == 1, i2, r)
    r = jnp.where(lane == 2, w1, r)
    return jnp.where(lane == 3, w2, r)


def _out_proj_route_kernel(x0_ref, a0_ref, m0_ref, x1_ref, a1_ref, m1_ref, w_ref, g_ref, wr_ref,
                           o_ref, r_ref, *, n0):
    i = pl.program_id(0)

    @pl.when(i < n0)
    def _():
        _out_proj_block(x0_ref, a0_ref, m0_ref, w_ref, o_ref)

    @pl.when(i >= n0)
    def _():
        _out_proj_block(x1_ref, a1_ref, m1_ref, w_ref, o_ref)

    r_ref[...] = _route(_rms(o_ref[...], g_ref[...]), wr_ref[...])


def _out_proj_route(first, second, w, g, w_router_pad):
    (x0, a0, m0), (x1, a1, m1) = first, second
    d = x0.shape[1]
    tm = min(TOKEN_TILE, x0.shape[0], x1.shape[0])
    n0, n1 = x0.shape[0] // tm, x1.shape[0] // tm

    def group0(width):
        return pl.BlockSpec((tm, width), lambda i: (jnp.minimum(i, n0 - 1), 0))

    def group1(width):
        return pl.BlockSpec((tm, width), lambda i: (jnp.maximum(i - n0, 0), 0))

    return pl.pallas_call(
        functools.partial(_out_proj_route_kernel, n0=n0),
        grid=(n0 + n1,),
        in_specs=[
            group0(d), group0(a0.shape[1]), group0(m0.shape[1]),
            group1(d), group1(a1.shape[1]), group1(m1.shape[1]),
            pl.BlockSpec(w.shape, lambda i: (0, 0)),
            pl.BlockSpec((1, d), lambda i: (0, 0)),
            pl.BlockSpec((d, LANES), lambda i: (0, 0)),
        ],
        out_specs=[
            pl.BlockSpec((tm, d), lambda i: (i, 0)),
            pl.BlockSpec((tm, LANES), lambda i: (i, 0)),
        ],
        out_shape=[
            jax.ShapeDtypeStruct(((n0 + n1) * tm, d), F32),
            jax.ShapeDtypeStruct(((n0 + n1) * tm, LANES), F32),
        ],
        compiler_params=_params(("parallel",)),
        name="out_proj_route",
    )(x0, a0, m0, x1, a1, m1, w, g.reshape(1, d), w_router_pad)


def _mixer_ffn_kernel(x_ref, a_ref, m_ref, wo_ref, g_ref, wg_ref, wu_ref, wd_ref, o_ref, x1_ref, h_ref, acc_ref):
    f = pl.program_id(1)

    @pl.when(f == 0)
    def _():
        _out_proj_block(x_ref, a_ref, m_ref, wo_ref, x1_ref)
        h_ref[...] = _rms(x1_ref[...], g_ref[...]).astype(BF16)
        acc_ref[...] = jnp.zeros_like(acc_ref)

    h = h_ref[...]
    a = jax.nn.silu(_dot(h, wg_ref[...])) * _dot(h, wu_ref[...])
    acc_ref[...] += _dot(a.astype(BF16), wd_ref[...])

    @pl.when(f == pl.num_programs(1) - 1)
    def _():
        o_ref[...] = x1_ref[...] + acc_ref[...]


def _mixer_ffn(x, a, m, w_out, g, w_gu, w_down):
    t, d = x.shape
    tm = min(TOKEN_TILE, t)
    tf = FFN_COLS
    nf = FFN_DIM // tf
    return pl.pallas_call(
        _mixer_ffn_kernel,
        grid=(t // tm, nf),
        in_specs=[
            pl.BlockSpec((tm, d), lambda i, f: (i, 0)),
            pl.BlockSpec((tm, a.shape[1]), lambda i, f: (i, 0)),
            pl.BlockSpec((tm, m.shape[1]), lambda i, f: (i, 0)),
            pl.BlockSpec(w_out.shape, lambda i, f: (0, 0)),
            pl.BlockSpec((1, d), lambda i, f: (0, 0)),
            pl.BlockSpec((d, tf), lambda i, f: (0, f)),
            pl.BlockSpec((d, tf), lambda i, f: (0, nf + f)),
            pl.BlockSpec((tf, d), lambda i, f: (f, 0)),
        ],
        out_specs=pl.BlockSpec((tm, d), lambda i, f: (i, 0)),
        out_shape=jax.ShapeDtypeStruct((t, d), F32),
        scratch_shapes=[pltpu.VMEM((tm, d), F32), pltpu.VMEM((tm, d), BF16), pltpu.VMEM((tm, d), F32)],
        compiler_params=_params(("parallel", "arbitrary")),
        name="mixer_ffn",
    )(x, a, m, w_out, g.reshape(1, d), w_gu, w_gu, w_down)


def _start_row_gather(src_hbm, idx_ref, dst, sem):
    def body(r, carry):
        row = idx_ref[r]
        pltpu.make_async_copy(src_hbm.at[pl.ds(row, 1)], dst.at[pl.ds(r, 1)], sem).start()
        return carry

    lax.fori_loop(0, dst.shape[0], body, 0, unroll=8)


def _wait_row_gather(src_hbm, dst, sem):
    pltpu.make_async_copy(src_hbm.at[pl.ds(0, dst.shape[0])], dst, sem).wait()


def _moe_kernel(te_ref, nt_ref, src_ref, x_hbm, g_ref, wg_ref, wu_ref, wd_ref, o_ref,
                xbuf, h_ref, acc_ref, sem):
    i = pl.program_id(0)
    f = pl.program_id(1)
    tm = h_ref.shape[0]
    nt = nt_ref[0]
    slot = i % 2

    @pl.when((i == 0) & (f == 0))
    def _():
        _start_row_gather(x_hbm, src_ref, xbuf.at[0], sem.at[0])

    nf = pl.num_programs(1)
    share = tm // EXPERT_STEPS

    def column_step(prefetch):
        if prefetch:
            for r in range(share):
                row = src_ref[(i + 1) * tm + f * share + r]
                pltpu.make_async_copy(
                    x_hbm.at[pl.ds(row, 1)], xbuf.at[1 - slot, pl.ds(f * share + r, 1)], sem.at[1 - slot]
                ).start()
        h = h_ref[...]
        a = jax.nn.silu(_dot(h, wg_ref[...])) * _dot(h, wu_ref[...])
        acc_ref[...] += _dot(a.astype(BF16), wd_ref[...])

    @pl.when(i < nt)
    def _():
        @pl.when(f == 0)
        def _():
            _wait_row_gather(x_hbm, xbuf.at[slot], sem.at[slot])
            h_ref[...] = _rms(xbuf[slot], g_ref[...]).astype(BF16)
            acc_ref[...] = jnp.zeros_like(acc_ref)

        @pl.when(i + 1 < nt)
        def _():
            column_step(True)

        @pl.when(i + 1 >= nt)
        def _():
            column_step(False)

        @pl.when(f == nf - 1)
        def _():
            o_ref[...] = acc_ref[...]

    @pl.when((i >= nt) & (f == 0))
    def _():
        o_ref[...] = jnp.zeros_like(o_ref)


def _moe_ffn(x, g, src, tile_expert, n_tiles, w_gu, w_down):
    d = x.shape[1]
    p = src.shape[0]
    tm = MOE_TILE
    tf = EXPERT_COLS
    nf = EXPERT_DIM // tf

    def col(i, f, nt):
        return jnp.where(i < nt[0], f, nf - 1)

    grid_spec = pltpu.PrefetchScalarGridSpec(
        num_scalar_prefetch=3,
        grid=(p // tm, nf),
        in_specs=[
            pl.BlockSpec(memory_space=pl.ANY),
            pl.BlockSpec((1, d), lambda i, f, te, nt, src: (0, 0)),
            pl.BlockSpec((None, d, tf), lambda i, f, te, nt, src: (te[i], 0, col(i, f, nt))),
            pl.BlockSpec((None, d, tf), lambda i, f, te, nt, src: (te[i], 0, nf + col(i, f, nt))),
            pl.BlockSpec((None, tf, d), lambda i, f, te, nt, src: (te[i], col(i, f, nt), 0)),
        ],
        out_specs=pl.BlockSpec((tm, d), lambda i, f, te, nt, src: (i, 0)),
        scratch_shapes=[
            pltpu.VMEM((2, tm, d), F32),
            pltpu.VMEM((tm, d), BF16),
            pltpu.VMEM((tm, d), F32),
            pltpu.SemaphoreType.DMA((2,)),
        ],
    )
    return pl.pallas_call(
        _moe_kernel,
        grid_spec=grid_spec,
        out_shape=jax.ShapeDtypeStruct((p, d), F32),
        compiler_params=_params(("arbitrary", "arbitrary")),
        name="moe_ffn",
    )(tile_expert, n_tiles, src, x, g.reshape(1, d), w_gu, w_gu, w_down)


def _combine_norm_kernel(dest_ref, x_ref, r_ref, y_hbm, g_ref, o_ref, buf, sem, *, tok0):
    i = pl.program_id(0)
    tm = x_ref.shape[0]
    slot = i % 2

    def fetch(tile, s):
        base = 2 * (tok0 + tile * tm)

        def body(r, carry):
            for k in range(2):
                row = dest_ref[base + 2 * r + k]
                pltpu.make_async_copy(
                    y_hbm.at[pl.ds(row, 1)], buf.at[s, k, pl.ds(r, 1)], sem.at[s, k]
                ).start(priority=k)
            return carry

        lax.fori_loop(0, tm, body, 0, unroll=8)

    @pl.when(i == 0)
    def _():
        fetch(0, 0)

    @pl.when(i + 1 < pl.num_programs(0))
    def _():
        fetch(i + 1, 1 - slot)

    for k in range(2):
        _wait_row_gather(y_hbm, buf.at[slot, k], sem.at[slot, k])
    moe = buf[slot, 0] * r_ref[:, 2:3] + buf[slot, 1] * r_ref[:, 3:4]
    o_ref[...] = _rms(x_ref[...] + moe, g_ref[...])


def _combine_norm(x, route, rows_out, dest, g, tok0, rows):
    d = x.shape[1]
    tm = min(TOKEN_TILE, rows)
    blk0 = tok0 // tm
    grid_spec = pltpu.PrefetchScalarGridSpec(
        num_scalar_prefetch=1,
        grid=(rows // tm,),
        in_specs=[
            pl.BlockSpec((tm, d), lambda i, dest: (i + blk0, 0)),
            pl.BlockSpec((tm, LANES), lambda i, dest: (i + blk0, 0)),
            pl.BlockSpec(memory_space=pl.ANY),
            pl.BlockSpec((1, d), lambda i, dest: (0, 0)),
        ],
        out_specs=pl.BlockSpec((tm, d), lambda i, dest: (i, 0)),
        scratch_shapes=[pltpu.VMEM((2, 2, tm, d), F32), pltpu.SemaphoreType.DMA((2, 2))],
    )
    return pl.pallas_call(
        functools.partial(_combine_norm_kernel, tok0=tok0),
        grid_spec=grid_spec,
        out_shape=jax.ShapeDtypeStruct((rows, d), F32),
        compiler_params=_params(("arbitrary",)),
        name="combine_norm",
    )(dest, x, route, rows_out, g.reshape(1, d))


def _head_softmax_pv(s, v):
    m = jnp.max(s, axis=-1, keepdims=True)
    p = jnp.exp(s - m)
    return _dot(p.astype(BF16), v) / jnp.sum(p, axis=-1, keepdims=True)


def _mem_attn_shared_kernel(q_ref, kv_ref, o_ref):
    q = q_ref[...]
    k = kv_ref[0, :, 0:MEM_WIDTH].astype(BF16)
    v = kv_ref[0, :, MEM_WIDTH:].astype(BF16)
    col = lax.broadcasted_iota(jnp.int32, (1, MEM_WIDTH), 1)
    acc = jnp.zeros(q.shape, F32)
    for h in range(MEM_HEADS):
        in_head = (col >= h * MEM_HEAD_DIM) & (col < (h + 1) * MEM_HEAD_DIM)
        qh = jnp.where(in_head, q, 0.0).astype(BF16)
        s = _dot_nt(qh, k) * MEM_HEAD_DIM ** -0.5
        acc = acc + jnp.where(in_head, _head_softmax_pv(s, v), 0.0)
    o_ref[...] = acc


def _mem_attn_shared(proj, q_col, mem_kv, seq_len):
    t = proj.shape[0]
    tq = min(TOKEN_TILE, seq_len)
    per_seq = seq_len // tq
    return pl.pallas_call(
        _mem_attn_shared_kernel,
        grid=(t // tq,),
        in_specs=[
            pl.BlockSpec((tq, MEM_WIDTH), lambda i: (i, q_col)),
            pl.BlockSpec((1, N_MEM, 2 * MEM_WIDTH), lambda i: (i // per_seq, 0, 0)),
        ],
        out_specs=pl.BlockSpec((tq, MEM_WIDTH), lambda i: (i, 0)),
        out_shape=jax.ShapeDtypeStruct((t, MEM_WIDTH), F32),
        compiler_params=_params(("parallel",)),
        name="mem_attn_shared",
    )(proj, mem_kv)


def _mem_attn_decode_kernel(q_ref, k_ref, v_ref, o_ref, *, seq_len):
    rows = MEM_HEADS * seq_len
    row_head = _div(lax.broadcasted_iota(jnp.int32, (rows, MEM_WIDTH), 0), seq_len)
    col_head = _div(lax.broadcasted_iota(jnp.int32, (rows, MEM_WIDTH), 1), MEM_HEAD_DIM)
    diag = row_head == col_head
    out_head = _div(lax.broadcasted_iota(jnp.int32, (seq_len, MEM_WIDTH), 1), MEM_HEAD_DIM)
    for b in range(k_ref.shape[0]):
        q = q_ref[b * seq_len:(b + 1) * seq_len, :]
        qd = jnp.where(diag, jnp.concatenate([q] * MEM_HEADS, axis=0), 0.0).astype(BF16)
        s = _dot_nt(qd, k_ref[b].astype(BF16)) * MEM_HEAD_DIM ** -0.5
        o = _head_softmax_pv(s, v_ref[b].astype(BF16))
        acc = jnp.zeros((seq_len, MEM_WIDTH), F32)
        for h in range(MEM_HEADS):
            acc = acc + jnp.where(out_head == h, o[h * seq_len:(h + 1) * seq_len], 0.0)
        o_ref[b * seq_len:(b + 1) * seq_len, :] = acc


def _mem_attn_decode(proj, q_col, mem_k, mem_v, seq_len):
    t = proj.shape[0]
    nb = mem_k.shape[0]
    bb = min(SEQ_BATCH, nb)
    return pl.pallas_call(
        functools.partial(_mem_attn_decode_kernel, seq_len=seq_len),
        grid=(nb // bb,),
        in_specs=[
            pl.BlockSpec((bb * seq_len, MEM_WIDTH), lambda i: (i, q_col)),
            pl.BlockSpec((bb, N_MEM, MEM_WIDTH), lambda i: (i, 0, 0)),
            pl.BlockSpec((bb, N_MEM, MEM_WIDTH), lambda i: (i, 0, 0)),
        ],
        out_specs=pl.BlockSpec((bb * seq_len, MEM_WIDTH), lambda i: (i, 0)),
        out_shape=jax.ShapeDtypeStruct((t, MEM_WIDTH), F32),
        compiler_params=_params(("parallel",)),
        name="mem_attn_decode",
    )(proj, mem_k, mem_v)


def _sink_softmax_pv(s, sink, v):
    m = jnp.maximum(jnp.max(s, axis=-1, keepdims=True), sink)
    p = jnp.exp(s - m)
    denom = jnp.sum(p, axis=-1, keepdims=True) + jnp.exp(sink - m)
    return _dot(p.astype(BF16), v) / denom


def _swa_prefill_kernel(slope_ref, sink_ref, q_ref, kp_ref, kc_ref, vp_ref, vc_ref, o_ref, *, steps_per_seq):
    w = SWA_WINDOW
    first = (pl.program_id(0) % steps_per_seq) == 0
    k_all = jnp.concatenate([kp_ref[...], kc_ref[...]], axis=0)
    v_all = jnp.concatenate([vp_ref[...], vc_ref[...]], axis=0)
    qi = lax.broadcasted_iota(jnp.int32, (w, 2 * w), 0)
    kj = lax.broadcasted_iota(jnp.int32, (w, 2 * w), 1)
    dist = qi + w - kj
    in_window = (dist >= 0) & (dist < w)
    distf = dist.astype(F32)
    for j in range(q_ref.shape[0] // w):
        q = q_ref[j * w:(j + 1) * w, :]
        k = k_all[j * w:(j + 2) * w]
        v = v_all[j * w:(j + 2) * w]
        valid = in_window & ((kj >= w) | jnp.logical_not(first)) if j == 0 else in_window
        outs = []
        for kh in range(SWA_KV_HEADS):
            kk = k[:, kh * SWA_HEAD_DIM:(kh + 1) * SWA_HEAD_DIM].astype(BF16)
            vv = v[:, kh * SWA_HEAD_DIM:(kh + 1) * SWA_HEAD_DIM].astype(BF16)
            heads = [kh * SWA_GROUP + g for g in range(SWA_GROUP)]
            qg = jnp.concatenate([q[:, h * SWA_HEAD_DIM:(h + 1) * SWA_HEAD_DIM] for h in heads], axis=0).astype(BF16)
            s3 = _dot_nt(qg, kk) * SWA_HEAD_DIM ** -0.5
            for g, h in enumerate(heads):
                s = s3[g * w:(g + 1) * w] - slope_ref[h] * distf
                s = jnp.where(valid, s, NEG_INF)
                outs.append(_sink_softmax_pv(s, sink_ref[h], vv))
        o_ref[j * w:(j + 1) * w, :] = jnp.concatenate(outs, axis=1)


def _swa_prefill(proj, slopes, sinks, seq_len):
    t = proj.shape[0]
    w = SWA_WINDOW
    nq = SWA_QUERY_BLOCKS
    per_seq = seq_len // (nq * w)
    kcol = SWA_Q_WIDTH // SWA_KV_WIDTH
    vcol = kcol + 1

    def prev(i):
        return jnp.where(i % per_seq == 0, nq * i, nq * i - 1)

    smem = pl.BlockSpec(memory_space=pltpu.SMEM)
    return pl.pallas_call(
        functools.partial(_swa_prefill_kernel, steps_per_seq=per_seq),
        grid=(t // (nq * w),),
        in_specs=[
            smem,
            smem,
            pl.BlockSpec((nq * w, SWA_Q_WIDTH), lambda i: (i, 0)),
            pl.BlockSpec((w, SWA_KV_WIDTH), lambda i: (prev(i), kcol)),
            pl.BlockSpec((nq * w, SWA_KV_WIDTH), lambda i: (i, kcol)),
            pl.BlockSpec((w, SWA_KV_WIDTH), lambda i: (prev(i), vcol)),
            pl.BlockSpec((nq * w, SWA_KV_WIDTH), lambda i: (i, vcol)),
        ],
        out_specs=pl.BlockSpec((nq * w, SWA_Q_WIDTH), lambda i: (i, 0)),
        out_shape=jax.ShapeDtypeStruct((t, SWA_Q_WIDTH), F32),
        compiler_params=_params(("parallel",)),
        name="swa_prefill",
    )(slopes, sinks, proj, proj, proj, proj, proj)


def _swa_decode_kernel(q_ref, kn_ref, vn_ref, kc_ref, vc_ref, rep_ref, slope_ref, sink_ref, o_ref, *, seq_len):
    w = SWA_WINDOW
    rows = SWA_HEADS * seq_len
    span = 2 * w
    row = lax.broadcasted_iota(jnp.int32, (rows, SWA_Q_WIDTH), 0)
    col = lax.broadcasted_iota(jnp.int32, (rows, SWA_Q_WIDTH), 1)
    diag = _div(row, seq_len) == _div(col, SWA_HEAD_DIM)
    ql = lax.broadcasted_iota(jnp.int32, (rows, span), 0) & (seq_len - 1)
    kj = lax.broadcasted_iota(jnp.int32, (rows, span), 1)
    dist = w + ql - kj
    valid = (dist >= 0) & (dist < w)
    bias = slope_ref[...] * dist.astype(F32)
    sink = sink_ref[...]
    out_head = _div(lax.broadcasted_iota(jnp.int32, (seq_len, SWA_Q_WIDTH), 1), SWA_HEAD_DIM)
    rep = rep_ref[...]
    tail = jnp.zeros((w - seq_len, SWA_KV_WIDTH), F32)
    for b in range(kc_ref.shape[0]):
        rs = slice(b * seq_len, (b + 1) * seq_len)
        k = jnp.concatenate([kc_ref[b], kn_ref[rs, :], tail], axis=0).astype(BF16)
        v = jnp.concatenate([vc_ref[b], vn_ref[rs, :], tail], axis=0).astype(BF16)
        k_rep = _dot(k, rep).astype(BF16)
        v_rep = _dot(v, rep).astype(BF16)
        qd = jnp.where(diag, jnp.concatenate([q_ref[rs, :]] * SWA_HEADS, axis=0), 0.0).astype(BF16)
        s = _dot_nt(qd, k_rep) * SWA_HEAD_DIM ** -0.5 - bias
        s = jnp.where(valid, s, NEG_INF)
        o = _sink_softmax_pv(s, sink, v_rep)
        acc = jnp.zeros((seq_len, SWA_Q_WIDTH), F32)
        for h in range(SWA_HEADS):
            acc = acc + jnp.where(out_head == h, o[h * seq_len:(h + 1) * seq_len], 0.0)
        o_ref[rs, :] = acc


def _swa_decode(proj, cache_k, cache_v, slopes, sinks, seq_len):
    t = proj.shape[0]
    nb = cache_k.shape[0]
    bb = min(SEQ_BATCH, nb)
    w = SWA_WINDOW
    kcol = SWA_Q_WIDTH // SWA_KV_WIDTH
    src = (jnp.arange(SWA_Q_WIDTH) // SWA_HEAD_DIM // SWA_GROUP) * SWA_HEAD_DIM + jnp.arange(SWA_Q_WIDTH) % SWA_HEAD_DIM
    rep = (jnp.arange(SWA_KV_WIDTH)[:, None] == src[None, :]).astype(BF16)
    slope_rows = jnp.repeat(slopes, seq_len).reshape(-1, 1)
    sink_rows = jnp.repeat(sinks, seq_len).reshape(-1, 1)
    rows = SWA_HEADS * seq_len
    return pl.pallas_call(
        functools.partial(_swa_decode_kernel, seq_len=seq_len),
        grid=(nb // bb,),
        in_specs=[
            pl.BlockSpec((bb * seq_len, SWA_Q_WIDTH), lambda i: (i, 0)),
            pl.BlockSpec((bb * seq_len, SWA_KV_WIDTH), lambda i: (i, kcol)),
            pl.BlockSpec((bb * seq_len, SWA_KV_WIDTH), lambda i: (i, kcol + 1)),
            pl.BlockSpec((bb, w, SWA_KV_WIDTH), lambda i: (i, 0, 0)),
            pl.BlockSpec((bb, w, SWA_KV_WIDTH), lambda i: (i, 0, 0)),
            pl.BlockSpec((SWA_KV_WIDTH, SWA_Q_WIDTH), lambda i: (0, 0)),
            pl.BlockSpec((rows, 1), lambda i: (0, 0)),
            pl.BlockSpec((rows, 1), lambda i: (0, 0)),
        ],
        out_specs=pl.BlockSpec((bb * seq_len, SWA_Q_WIDTH), lambda i: (i, 0)),
        out_shape=jax.ShapeDtypeStruct((t, SWA_Q_WIDTH), F32),
        compiler_params=_params(("parallel",)),
        name="swa_decode",
    )(proj, proj, proj, cache_k, cache_v, rep, slope_rows, sink_rows)


def _gdn_kernel(*refs, nblk, spb):
    r = GDN_ROWS
    c = r // spb
    qkv_refs = refs[0:nblk]
    z_refs = refs[nblk:2 * nblk]
    ba_refs = refs[2 * nblk:3 * nblk]
    conv0_ref, s0_ref, cw_ref, alog_ref, dtb_ref, gn_ref, o_ref, sfin_ref, s_scr, fbuf = refs[3 * nblk:]
    step = pl.program_id(1)
    nseq = nblk * spb
    tail = fbuf.shape[1]
    hist = GDN_CONV_K - 1

    @pl.when(step == 0)
    def _():
        s_scr[...] = s0_ref[...]
        fbuf[...] = jnp.zeros_like(fbuf)
        for s in range(nseq):
            fbuf[s, tail - hist:tail, :] = conv0_ref[s]

    ri = lax.broadcasted_iota(jnp.int32, (r, r), 0)
    ci = lax.broadcasted_iota(jnp.int32, (r, r), 1)
    same = _div(ri, c) == _div(ci, c)
    tri = same & (ri >= ci)
    strict = same & (ri > ci)
    eye = (ri == ci).astype(F32)
    cum_mat = jnp.concatenate([tri.astype(F32), same.astype(F32)], axis=0)
    row_seq = _div(lax.broadcasted_iota(jnp.int32, (r, 1), 0), c)
    row_tail = lax.broadcasted_iota(jnp.int32, (tail, 1), 0)
    zeros_rr = jnp.zeros((r, LANES), F32)
    cw = cw_ref[...]
    neg_a = -jnp.exp(alog_ref[...])
    gn = gn_ref[...]

    blocks = []
    for n in range(nblk):
        u = qkv_refs[n][...]
        pieces = []
        for s in range(spb):
            idx = n * spb + s
            us = u[s * c:(s + 1) * c]
            prev = fbuf[idx]
            acc = us * cw[hist:hist + 1]
            for back in range(1, GDN_CONV_K):
                moved = pltpu.roll(us, back, 0)
                head = jnp.where(row_tail < back, pltpu.roll(prev, back, 0), moved[0:tail])
                moved = head if c == tail else jnp.concatenate([head, moved[tail:]], axis=0)
                acc = acc + moved * cw[hist - back:hist - back + 1]
            fbuf[idx] = us[c - tail:c]
            pieces.append(acc)
        conv = pieces[0] if spb == 1 else jnp.concatenate(pieces, axis=0)
        qkv = jax.nn.silu(conv)

        ba = ba_refs[n][...]
        beta_all = jax.nn.sigmoid(ba)
        xg = ba + dtb_ref[...]
        g_all = neg_a * (jnp.maximum(xg, 0.0) + jnp.log(1.0 + jnp.exp(-jnp.abs(xg))))
        gsum = jnp.dot(cum_mat, g_all, preferred_element_type=F32, precision=HIGHEST)
        gcum = gsum[0:r]
        gtot = gsum[r:2 * r]
        gcum_t = jnp.transpose(jnp.concatenate([gcum, zeros_rr], axis=0))

        blocks.append((qkv, beta_all, gcum, gtot, gcum_t))

    items = [(n, h) for n in range(nblk) for h in range(GDN_HEADS)]
    bf = lambda x: x.astype(BF16)
    k_n, kb_n, q_n, qk_dec, vb_n, kdec_t, decay_n, glast_n = [], [], [], [], [], [], [], []
    for n, h in items:
        qkv, beta_all, gcum, gtot, gcum_t = blocks[n]
        lo = h * GDN_DK
        q = qkv[:, lo:lo + GDN_DK]
        k = qkv[:, GDN_QK_WIDTH + lo:GDN_QK_WIDTH + lo + GDN_DK]
        v = qkv[:, 2 * GDN_QK_WIDTH + lo:2 * GDN_QK_WIDTH + lo + GDN_DV]
        q = q * lax.rsqrt(jnp.sum(q * q, axis=-1, keepdims=True) + L2_EPS) * GDN_DK ** -0.5
        k = k * lax.rsqrt(jnp.sum(k * k, axis=-1, keepdims=True) + L2_EPS)
        beta = beta_all[:, h:h + 1]
        gcol = gcum[:, GDN_HEADS + h:GDN_HEADS + h + 1]
        grow = gcum_t[GDN_HEADS + h:GDN_HEADS + h + 1, 0:r]
        glast = gtot[:, GDN_HEADS + h:GDN_HEADS + h + 1]
        eg = jnp.exp(gcol)
        kb = k * beta
        decay_n.append(jnp.where(tri, jnp.exp(jnp.where(tri, gcol - grow, 0.0)), 0.0))
        k_n.append(bf(k))
        kb_n.append(kb)
        q_n.append(q)
        vb_n.append(v * beta)
        qk_dec.append(bf(jnp.concatenate([kb * eg, q * eg], axis=0)))
        kdec = jnp.concatenate([k * jnp.exp(glast - gcol), zeros_rr], axis=0)
        kdec_t.append(bf(jnp.transpose(kdec)[:, 0:r]))
        glast_n.append(glast)

    kq = [_dot_nt(bf(jnp.concatenate([kb_n[i], q_n[i]], axis=0)), k_n[i]) for i in range(len(items))]
    power = [jnp.where(strict, kq[i][0:r] * decay_n[i], 0.0) for i in range(len(items))]
    qk = [bf(kq[i][r:2 * r] * decay_n[i]) for i in range(len(items))]
    inv = [eye - p for p in power]
    span = 2
    while span < c:
        power = [_dot(bf(p), bf(p)) for p in power]
        inv = [_dot(bf(a), bf(eye + p)) for a, p in zip(inv, power)]
        span *= 2
    inv = [bf(a) for a in inv]
    tq = [jnp.concatenate([a, bf(_dot(b, a))], axis=0) for a, b in zip(inv, qk)]

    if spb == 1:
        ks_qs = [_dot(qk_dec[i], bf(s_scr[n, h])) for i, (n, h) in enumerate(items)]
        resid = [vb_n[i] - ks_qs[i][0:r] for i in range(len(items))]
        qs = [x[r:2 * r] for x in ks_qs]
    else:
        resid, qs = [], []
        for i, (n, h) in enumerate(items):
            both = []
            for s in range(spb):
                rows = jnp.concatenate([qk_dec[i][s * c:(s + 1) * c], qk_dec[i][r + s * c:r + (s + 1) * c]], axis=0)
                both.append(_dot(rows, bf(s_scr[n * spb + s, h])))
            resid.append(vb_n[i] - jnp.concatenate([x[0:c] for x in both], axis=0))
            qs.append(jnp.concatenate([x[c:2 * c] for x in both], axis=0))

    vo = [_dot(tq[i], bf(resid[i])) for i in range(len(items))]
    for i, (n, h) in enumerate(items):
        v_new = vo[i][0:r]
        for s in range(spb):
            idx = n * spb + s
            vs = v_new if spb == 1 else jnp.where(row_seq == s, v_new, 0.0)
            carry = jnp.exp(glast_n[i][s * c:s * c + 1, :])
            s_scr[idx, h] = s_scr[idx, h] * carry + _dot(kdec_t[i], bf(vs))
    for i, (n, h) in enumerate(items):
        on = _rms(qs[i] + vo[i][r:2 * r], gn)
        zh = z_refs[n][:, h * GDN_DV:(h + 1) * GDN_DV]
        o_ref[n, :, h * GDN_DV:(h + 1) * GDN_DV] = on * jax.nn.silu(zh)

    @pl.when(step == pl.num_programs(1) - 1)
    def _():
        sfin_ref[...] = s_scr[...]


def _gdn(proj, conv_buf, s0, conv_w, a_log, dt_bias, norm_g, *, nblk, spb, steps):
    t = proj.shape[0]
    r = GDN_ROWS
    groups = t // (r * nblk * steps)
    nseq = nblk * spb
    c = r // spb
    z_col = GDN_CONV_DIM // GDN_V_WIDTH
    ba_col = (GDN_CONV_DIM + GDN_V_WIDTH + MEM_WIDTH) // LANES

    def rows(n, col):
        return lambda g, l: ((g * nblk + n) * steps + l, col)

    lane6 = jnp.zeros((1, LANES), F32)
    alog = lane6.at[0, GDN_HEADS:2 * GDN_HEADS].set(a_log)
    dtb = lane6.at[0, GDN_HEADS:2 * GDN_HEADS].set(dt_bias)
    const = lambda g, l: (0, 0)
    in_specs = (
        [pl.BlockSpec((r, GDN_CONV_DIM), rows(n, 0)) for n in range(nblk)]
        + [pl.BlockSpec((r, GDN_V_WIDTH), rows(n, z_col)) for n in range(nblk)]
        + [pl.BlockSpec((r, LANES), rows(n, ba_col)) for n in range(nblk)]
        + [
            pl.BlockSpec((nseq, GDN_CONV_K - 1, GDN_CONV_DIM), lambda g, l: (g, 0, 0)),
            pl.BlockSpec((nseq, GDN_HEADS, GDN_DK, GDN_DV), lambda g, l: (g, 0, 0, 0)),
            pl.BlockSpec((GDN_CONV_K, GDN_CONV_DIM), const),
            pl.BlockSpec((1, LANES), const),
            pl.BlockSpec((1, LANES), const),
            pl.BlockSpec((1, GDN_DV), const),
        ]
    )
    out, s_fin = pl.pallas_call(
        functools.partial(_gdn_kernel, nblk=nblk, spb=spb),
        grid=(groups, steps),
        in_specs=in_specs,
        out_specs=[
            pl.BlockSpec((nblk, r, GDN_V_WIDTH), lambda g, l: (0, g * steps + l, 0)),
            pl.BlockSpec((nseq, GDN_HEADS, GDN_DK, GDN_DV), lambda g, l: (g, 0, 0, 0)),
        ],
        out_shape=[
            jax.ShapeDtypeStruct((nblk, groups * steps * r, GDN_V_WIDTH), F32),
            jax.ShapeDtypeStruct((groups * nseq, GDN_HEADS, GDN_DK, GDN_DV), F32),
        ],
        scratch_shapes=[
            pltpu.VMEM((nseq, GDN_HEADS, GDN_DK, GDN_DV), F32),
            pltpu.VMEM((nseq, 8, GDN_CONV_DIM), F32),
        ],
        compiler_params=_params(("parallel", "arbitrary")),
        name="gdn",
    )(*([proj] * (3 * nblk)), conv_buf, s0, conv_w, alog, dtb, norm_g.reshape(1, GDN_DV))
    return out.reshape(t, GDN_V_WIDTH), s_fin


def _dispatch(route, tm, p):
    t = route.shape[0]
    experts = route[:, 0:2].astype(jnp.int32).reshape(-1)
    onehot = (experts[:, None] == jnp.arange(N_EXPERTS)[None, :]).astype(jnp.int32)
    csum = jnp.cumsum(onehot, axis=0)
    rank = jnp.take_along_axis(csum, experts[:, None], axis=1)[:, 0] - 1
    tiles = (csum[-1] + tm - 1) // tm
    tile_end = jnp.cumsum(tiles)
    start = (tile_end - tiles) * tm
    dest = start[experts] + rank
    n_tiles = tile_end[-1:].astype(jnp.int32)
    tile_expert = jnp.sum(tile_end[None, :] <= jnp.arange(p // tm)[:, None], axis=1)
    tile_expert = jnp.minimum(tile_expert, N_EXPERTS - 1).astype(jnp.int32)
    src = jnp.zeros((p,), jnp.int32).at[dest].set(jnp.arange(2 * t, dtype=jnp.int32) // 2)
    return dest, src, tile_expert, n_tiles


def kernel(x_prompt, x_sample, state_gdn_conv, state_gdn_ssm, cache_swa_k, cache_swa_v, cache_mem_k, cache_mem_v, mem_prompt, attn_norm, ffn_norm, mem_norm, final_norm, w_in_gdn, gdn_conv_w, gdn_a_log, gdn_dt_bias, gdn_norm, w_out_gdn, w_in_swa, swa_sinks, w_out_swa, w_mem_kv, w_ffn_gu, w_ffn_down, w_router, w_exp_gu, w_exp_down):
    bp, lp, d = x_prompt.shape
    bs, ls, _ = x_sample.shape
    tp, ts = bp * lp, bs * ls
    xp = x_prompt.reshape(tp, d)
    xs = x_sample.reshape(ts, d)

    mem = mem_prompt.reshape(bp * N_MEM, d)
    mem_kv = [
        _norm_matmul(mem, mem_norm[i], w_mem_kv[i].astype(BF16)).reshape(bp, N_MEM, 2 * MEM_WIDTH)
        for i in range(2)
    ]
    new_mem_k = jnp.stack([kv[..., :MEM_WIDTH].reshape(bp, N_MEM, MEM_HEADS, MEM_HEAD_DIM) for kv in mem_kv])
    new_mem_v = jnp.stack([kv[..., MEM_WIDTH:].reshape(bp, N_MEM, MEM_HEADS, MEM_HEAD_DIM) for kv in mem_kv])

    w_in = w_in_gdn[0]
    o_z = GDN_CONV_DIM + GDN_V_WIDTH
    o_mem = o_z + 2 * GDN_HEADS
    w0 = jnp.concatenate(
        [w_in[:, :o_z], w_in[:, o_mem:], w_in[:, o_z:o_mem], jnp.zeros((d, LANES - 2 * GDN_HEADS), F32)], axis=1
    ).astype(BF16)
    mq_col0 = o_z // MEM_WIDTH
    w_out0 = w_out_gdn[0].astype(BF16)
    w_gu0 = w_ffn_gu[0].astype(BF16)
    w_dn0 = w_ffn_down[0].astype(BF16)

    proj_p = _norm_matmul(xp, attn_norm[0], w0)
    proj_s = _norm_matmul(xs, attn_norm[0], w0)

    zero_conv = jnp.zeros((bp, GDN_CONV_K - 1, GDN_CONV_DIM), F32)
    zero_state = jnp.zeros((bp, GDN_HEADS, GDN_DK, GDN_DV), F32)
    gdn_p, ssm_p = _gdn(proj_p, zero_conv, zero_state, gdn_conv_w[0], gdn_a_log[0], gdn_dt_bias[0], gdn_norm[0],
                        nblk=bp, spb=1, steps=lp // GDN_ROWS)
    gdn_s, ssm_s = _gdn(proj_s, state_gdn_conv[0], state_gdn_ssm[0], gdn_conv_w[0], gdn_a_log[0], gdn_dt_bias[0],
                        gdn_norm[0], nblk=1, spb=GDN_ROWS // ls, steps=1)
    hist = GDN_CONV_K - 1
    conv_p = proj_p.reshape(bp, lp, -1)[:, lp - hist:, :GDN_CONV_DIM]
    conv_s = proj_s.reshape(bs, ls, -1)[:, ls - hist:, :GDN_CONV_DIM]

    memo_p = _mem_attn_shared(proj_p, mq_col0, mem_kv[0], lp)
    memo_s = _mem_attn_decode(proj_s, mq_col0, cache_mem_k[0].reshape(bs, N_MEM, MEM_WIDTH),
                              cache_mem_v[0].reshape(bs, N_MEM, MEM_WIDTH), ls)
    xp = _mixer_ffn(xp, gdn_p, memo_p, w_out0, ffn_norm[0], w_gu0, w_dn0)
    xs = _mixer_ffn(xs, gdn_s, memo_s, w_out0, ffn_norm[0], w_gu0, w_dn0)

    w1 = w_in_swa[0].astype(BF16)
    mq_col1 = (SWA_Q_WIDTH + 2 * SWA_KV_WIDTH) // MEM_WIDTH
    w_out1 = w_out_swa[0].astype(BF16)
    slopes = 2.0 ** (-8.0 * jnp.arange(1, SWA_HEADS + 1, dtype=F32) / SWA_HEADS)
    sinks = swa_sinks[0].astype(F32)

    proj_p = _norm_matmul(xp, attn_norm[1], w1)
    proj_s = _norm_matmul(xs, attn_norm[1], w1)
    swa_p = _swa_prefill(proj_p, slopes, sinks, lp)
    cache_k = cache_swa_k[0].reshape(bs, SWA_WINDOW, SWA_KV_WIDTH)
    cache_v = cache_swa_v[0].reshape(bs, SWA_WINDOW, SWA_KV_WIDTH)
    swa_s = _swa_decode(proj_s, cache_k, cache_v, slopes, sinks, ls)

    k0, v0 = SWA_Q_WIDTH, SWA_Q_WIDTH + SWA_KV_WIDTH
    pp = proj_p.reshape(bp, lp, -1)
    ps = proj_s.reshape(bs, ls, -1)
    kv_shape = (SWA_KV_HEADS, SWA_HEAD_DIM)
    swk_p = pp[:, lp - SWA_WINDOW:, k0:k0 + SWA_KV_WIDTH].reshape(bp, SWA_WINDOW, *kv_shape)
    swv_p = pp[:, lp - SWA_WINDOW:, v0:v0 + SWA_KV_WIDTH].reshape(bp, SWA_WINDOW, *kv_shape)
    swk_s = jnp.concatenate([cache_k[:, ls:], ps[:, :, k0:k0 + SWA_KV_WIDTH]], axis=1).reshape(bs, SWA_WINDOW, *kv_shape)
    swv_s = jnp.concatenate([cache_v[:, ls:], ps[:, :, v0:v0 + SWA_KV_WIDTH]], axis=1).reshape(bs, SWA_WINDOW, *kv_shape)

    memo_p = _mem_attn_shared(proj_p, mq_col1, mem_kv[1], lp)
    memo_s = _mem_attn_decode(proj_s, mq_col1, cache_mem_k[1].reshape(bs, N_MEM, MEM_WIDTH),
                              cache_mem_v[1].reshape(bs, N_MEM, MEM_WIDTH), ls)
    t_all = tp + ts
    w_r = jnp.concatenate([w_router[0], jnp.zeros((d, LANES - N_EXPERTS), F32)], axis=1)
    x_all, route = _out_proj_route((xp, swa_p, memo_p), (xs, swa_s, memo_s), w_out1, ffn_norm[1], w_r)
    tm = MOE_TILE
    p_rows = -(-(2 * t_all + N_EXPERTS * (tm - 1)) // tm) * tm
    dest, src, tile_expert, n_tiles = _dispatch(route, tm, p_rows)
    rows_out = _moe_ffn(x_all, ffn_norm[1], src, tile_expert, n_tiles,
                        w_exp_gu[0].astype(BF16), w_exp_down[0].astype(BF16))
    y_p = _combine_norm(x_all, route, rows_out, dest, final_norm, 0, tp)
    y_s = _combine_norm(x_all, route, rows_out, dest, final_norm, tp, ts)

    return (
        y_p.reshape(bp, lp, d),
        y_s.reshape(bs, ls, d),
        conv_p[None],
        ssm_p[None],
        swk_p[None],
        swv_p[None],
        new_mem_k,
        new_mem_v,
        conv_s[None],
        ssm_s[None],
        swk_s[None],
        swv_s[None],
    )
```

```python
import functools

import jax
import jax.numpy as jnp
from jax import lax
from jax.experimental import pallas as pl
from jax.experimental.pallas import tpu as pltpu

F32 = jnp.float32
BF16 = jnp.bfloat16
HIGHEST = lax.Precision.HIGHEST

D_MODEL = 1024
RMS_EPS = 1e-6
L2_EPS = 1e-6
NEG_INF = -1e30

GDN_HEADS = 6
GDN_DK = 128
GDN_DV = 128
GDN_CONV_K = 4
GDN_QK_WIDTH = GDN_HEADS * GDN_DK
GDN_V_WIDTH = GDN_HEADS * GDN_DV
GDN_CONV_DIM = 2 * GDN_QK_WIDTH + GDN_V_WIDTH
GDN_ROWS = 64

SWA_HEADS = 12
SWA_KV_HEADS = 4
SWA_HEAD_DIM = 64
SWA_GROUP = SWA_HEADS // SWA_KV_HEADS
SWA_WINDOW = 128
SWA_Q_WIDTH = SWA_HEADS * SWA_HEAD_DIM
SWA_KV_WIDTH = SWA_KV_HEADS * SWA_HEAD_DIM

N_MEM = 256
MEM_HEADS = 4
MEM_HEAD_DIM = 64
MEM_WIDTH = MEM_HEADS * MEM_HEAD_DIM

FFN_DIM = 2816
N_EXPERTS = 8
EXPERT_DIM = 3584

LANES = 128
VMEM_LIMIT = 56 * 1024 * 1024

TOKEN_TILE = 512
MOE_TILE = 1024
MOE_SUB = 256
FFN_COLS = 1408
EXPERT_STEPS = 2
EXPERT_COLS = EXPERT_DIM // EXPERT_STEPS
SEQ_BATCH = 8
SWA_QUERY_BLOCKS = 1


def _params(sem):
    return pltpu.CompilerParams(dimension_semantics=sem, vmem_limit_bytes=VMEM_LIMIT)


def _rms(x, g):
    return x * lax.rsqrt(jnp.mean(x * x, axis=-1, keepdims=True) + RMS_EPS) * g


def _dot(a, b):
    return jnp.dot(a, b, preferred_element_type=F32)


def _div(x, d):
    assert d & (d - 1) == 0
    return lax.shift_right_logical(x, d.bit_length() - 1)


def _dot_nt(a, b):
    return lax.dot_general(a, b, (((1,), (1,)), ((), ())), preferred_element_type=F32)


def _norm_matmul_kernel(x_ref, g_ref, w_ref, o_ref):
    o_ref[...] = _dot(_rms(x_ref[...], g_ref[...]).astype(BF16), w_ref[...])


def _norm_matmul(x, g, w):
    t, k = x.shape
    n = w.shape[1]
    tm = min(TOKEN_TILE, t)
    return pl.pallas_call(
        _norm_matmul_kernel,
        grid=(t // tm,),
        in_specs=[
            pl.BlockSpec((tm, k), lambda i: (i, 0)),
            pl.BlockSpec((1, k), lambda i: (0, 0)),
            pl.BlockSpec((k, n), lambda i: (0, 0)),
        ],
        out_specs=pl.BlockSpec((tm, n), lambda i: (i, 0)),
        out_shape=jax.ShapeDtypeStruct((t, n), F32),
        compiler_params=_params(("parallel",)),
        name="norm_matmul",
    )(x, g.reshape(1, k), w)


def _out_proj_block(x_ref, a_ref, m_ref, w_ref, o_ref):
    na = a_ref.shape[1]
    y = _dot(a_ref[...].astype(BF16), w_ref[0:na, :])
    y = y + _dot(m_ref[...].astype(BF16), w_ref[na:, :])
    o_ref[...] = x_ref[...] + y


def _route(h, w_router):
    logits = jnp.dot(h, w_router, preferred_element_type=F32, precision=HIGHEST)
    lane = lax.broadcasted_iota(jnp.int32, logits.shape, 1).astype(F32)
    logits = jnp.where(lane < N_EXPERTS, logits, -jnp.inf)
    m1 = jnp.max(logits, axis=-1, keepdims=True)
    i1 = jnp.min(jnp.where(logits == m1, lane, float(LANES)), axis=-1, keepdims=True)
    rest = jnp.where(lane == i1, -jnp.inf, logits)
    m2 = jnp.max(rest, axis=-1, keepdims=True)
    i2 = jnp.min(jnp.where(rest == m2, lane, float(LANES)), axis=-1, keepdims=True)
    e2 = jnp.exp(m2 - m1)
    w1 = 1.0 / (1.0 + e2)
    w2 = e2 / (1.0 + e2)
    r = jnp.where(lane == 0, i1, 0.0)
    r = jnp.where(lane == 1, i2, r)
    r = jnp.where(lane == 2, w1, r)
    return jnp.where(lane == 3, w2, r)


def _out_proj_route_kernel(x0_ref, a0_ref, m0_ref, x1_ref, a1_ref, m1_ref, w_ref, g_ref, wr_ref,
                           o_ref, r_ref, *, n0):
    i = pl.program_id(0)

    @pl.when(i < n0)
    def _():
        _out_proj_block(x0_ref, a0_ref, m0_ref, w_ref, o_ref)

    @pl.when(i >= n0)
    def _():
        _out_proj_block(x1_ref, a1_ref, m1_ref, w_ref, o_ref)

    r_ref[...] = _route(_rms(o_ref[...], g_ref[...]), wr_ref[...])


def _out_proj_route(first, second, w, g, w_router_pad):
    (x0, a0, m0), (x1, a1, m1) = first, second
    d = x0.shape[1]
    tm = min(TOKEN_TILE, x0.shape[0], x1.shape[0])
    n0, n1 = x0.shape[0] // tm, x1.shape[0] // tm

    def group0(width):
        return pl.BlockSpec((tm, width), lambda i: (jnp.minimum(i, n0 - 1), 0))

    def group1(width):
        return pl.BlockSpec((tm, width), lambda i: (jnp.maximum(i - n0, 0), 0))

    return pl.pallas_call(
        functools.partial(_out_proj_route_kernel, n0=n0),
        grid=(n0 + n1,),
        in_specs=[
            group0(d), group0(a0.shape[1]), group0(m0.shape[1]),
            group1(d), group1(a1.shape[1]), group1(m1.shape[1]),
            pl.BlockSpec(w.shape, lambda i: (0, 0)),
            pl.BlockSpec((1, d), lambda i: (0, 0)),
            pl.BlockSpec((d, LANES), lambda i: (0, 0)),
        ],
        out_specs=[
            pl.BlockSpec((tm, d), lambda i: (i, 0)),
            pl.BlockSpec((tm, LANES), lambda i: (i, 0)),
        ],
        out_shape=[
            jax.ShapeDtypeStruct(((n0 + n1) * tm, d), F32),
            jax.ShapeDtypeStruct(((n0 + n1) * tm, LANES), F32),
        ],
        compiler_params=_params(("parallel",)),
        name="out_proj_route",
    )(x0, a0, m0, x1, a1, m1, w, g.reshape(1, d), w_router_pad)


def _mixer_ffn_kernel(x_ref, a_ref, m_ref, wo_ref, g_ref, wg_ref, wu_ref, wd_ref, o_ref, x1_ref, h_ref, acc_ref):
    f = pl.program_id(1)

    @pl.when(f == 0)
    def _():
        _out_proj_block(x_ref, a_ref, m_ref, wo_ref, x1_ref)
        h_ref[...] = _rms(x1_ref[...], g_ref[...]).astype(BF16)
        acc_ref[...] = jnp.zeros_like(acc_ref)

    h = h_ref[...]
    a = jax.nn.silu(_dot(h, wg_ref[...])) * _dot(h, wu_ref[...])
    acc_ref[...] += _dot(a.astype(BF16), wd_ref[...])

    @pl.when(f == pl.num_programs(1) - 1)
    def _():
        o_ref[...] = x1_ref[...] + acc_ref[...]


def _mixer_ffn(x, a, m, w_out, g, w_gu, w_down):
    t, d = x.shape
    tm = min(TOKEN_TILE, t)
    tf = FFN_COLS
    nf = FFN_DIM // tf
    return pl.pallas_call(
        _mixer_ffn_kernel,
        grid=(t // tm, nf),
        in_specs=[
            pl.BlockSpec((tm, d), lambda i, f: (i, 0)),
            pl.BlockSpec((tm, a.shape[1]), lambda i, f: (i, 0)),
            pl.BlockSpec((tm, m.shape[1]), lambda i, f: (i, 0)),
            pl.BlockSpec(w_out.shape, lambda i, f: (0, 0)),
            pl.BlockSpec((1, d), lambda i, f: (0, 0)),
            pl.BlockSpec((d, tf), lambda i, f: (0, f)),
            pl.BlockSpec((d, tf), lambda i, f: (0, nf + f)),
            pl.BlockSpec((tf, d), lambda i, f: (f, 0)),
        ],
        out_specs=pl.BlockSpec((tm, d), lambda i, f: (i, 0)),
        out_shape=jax.ShapeDtypeStruct((t, d), F32),
        scratch_shapes=[pltpu.VMEM((tm, d), F32), pltpu.VMEM((tm, d), BF16), pltpu.VMEM((tm, d), F32)],
        compiler_params=_params(("parallel", "arbitrary")),
        name="mixer_ffn",
    )(x, a, m, w_out, g.reshape(1, d), w_gu, w_gu, w_down)


def _start_row_gather(src_hbm, idx_ref, dst, sem):
    def body(r, carry):
        row = idx_ref[r]
        pltpu.make_async_copy(src_hbm.at[pl.ds(row, 1)], dst.at[pl.ds(r, 1)], sem).start()
        return carry

    lax.fori_loop(0, dst.shape[0], body, 0, unroll=8)


def _wait_row_gather(src_hbm, dst, sem):
    pltpu.make_async_copy(src_hbm.at[pl.ds(0, dst.shape[0])], dst, sem).wait()


def _moe_kernel(te_ref, nt_ref, tv_ref, src_ref, x_hbm, g_ref, wg_ref, wu_ref, wd_ref, o_ref,
                xbuf, h_ref, acc_ref, sem):
    i = pl.program_id(0)
    f = pl.program_id(1)
    tm = h_ref.shape[0]
    nt = nt_ref[0]
    used = tv_ref[i]
    slot = i % 2

    @pl.when((i == 0) & (f == 0))
    def _():
        _start_row_gather(x_hbm, src_ref, xbuf.at[0], sem.at[0])

    nf = pl.num_programs(1)
    share = tm // EXPERT_STEPS

    def column_step(prefetch):
        def sub_tile(s):
            if prefetch and s == 0:
                for r in range(share):
                    row = src_ref[(i + 1) * tm + f * share + r]
                    pltpu.make_async_copy(
                        x_hbm.at[pl.ds(row, 1)], xbuf.at[1 - slot, pl.ds(f * share + r, 1)], sem.at[1 - slot]
                    ).start()
            rows = pl.ds(s * MOE_SUB, MOE_SUB)
            h = h_ref[rows, :]
            a = jax.nn.silu(_dot(h, wg_ref[...])) * _dot(h, wu_ref[...])
            acc_ref[rows, :] += _dot(a.astype(BF16), wd_ref[...])

        sub_tile(0)
        for s in range(1, tm // MOE_SUB):
            pl.when(s * MOE_SUB < used)(functools.partial(sub_tile, s))

    @pl.when(i < nt)
    def _():
        @pl.when(f == 0)
        def _():
            _wait_row_gather(x_hbm, xbuf.at[slot], sem.at[slot])
            h_ref[...] = _rms(xbuf[slot], g_ref[...]).astype(BF16)
            acc_ref[...] = jnp.zeros_like(acc_ref)

        @pl.when(i + 1 < nt)
        def _():
            column_step(True)

        @pl.when(i + 1 >= nt)
        def _():
            column_step(False)

        @pl.when(f == nf - 1)
        def _():
            o_ref[...] = acc_ref[...]

    @pl.when((i >= nt) & (f == 0))
    def _():
        o_ref[...] = jnp.zeros_like(o_ref)


def _moe_ffn(x, g, src, tile_expert, n_tiles, tile_used, w_gu, w_down):
    d = x.shape[1]
    p = src.shape[0]
    tm = MOE_TILE
    tf = EXPERT_COLS
    nf = EXPERT_DIM // tf

    def col(i, f, nt):
        return jnp.where(i < nt[0], f, nf - 1)

    grid_spec = pltpu.PrefetchScalarGridSpec(
        num_scalar_prefetch=4,
        grid=(p // tm, nf),
        in_specs=[
            pl.BlockSpec(memory_space=pl.ANY),
            pl.BlockSpec((1, d), lambda i, f, te, nt, tv, src: (0, 0)),
            pl.BlockSpec((None, d, tf), lambda i, f, te, nt, tv, src: (te[i], 0, col(i, f, nt))),
            pl.BlockSpec((None, d, tf), lambda i, f, te, nt, tv, src: (te[i], 0, nf + col(i, f, nt))),
            pl.BlockSpec((None, tf, d), lambda i, f, te, nt, tv, src: (te[i], col(i, f, nt), 0)),
        ],
        out_specs=pl.BlockSpec((tm, d), lambda i, f, te, nt, tv, src: (i, 0)),
        scratch_shapes=[
            pltpu.VMEM((2, tm, d), F32),
            pltpu.VMEM((tm, d), BF16),
            pltpu.VMEM((tm, d), F32),
            pltpu.SemaphoreType.DMA((2,)),
        ],
    )
    return pl.pallas_call(
        _moe_kernel,
        grid_spec=grid_spec,
        out_shape=jax.ShapeDtypeStruct((p, d), F32),
        compiler_params=_params(("arbitrary", "arbitrary")),
        name="moe_ffn",
    )(tile_expert, n_tiles, tile_used, src, x, g.reshape(1, d), w_gu, w_gu, w_down)


def _combine_norm_kernel(dest_ref, x_ref, r_ref, y_hbm, g_ref, o_ref, buf, sem, *, tok0):
    i = pl.program_id(0)
    tm = x_ref.shape[0]
    slot = i % 2

    def fetch(tile, s):
        base = 2 * (tok0 + tile * tm)

        def body(r, carry):
            for k in range(2):
                row = dest_ref[base + 2 * r + k]
                pltpu.make_async_copy(
                    y_hbm.at[pl.ds(row, 1)], buf.at[s, k, pl.ds(r, 1)], sem.at[s, k]
                ).start(priority=k)
            return carry

        lax.fori_loop(0, tm, body, 0, unroll=8)

    @pl.when(i == 0)
    def _():
        fetch(0, 0)

    @pl.when(i + 1 < pl.num_programs(0))
    def _():
        fetch(i + 1, 1 - slot)

    for k in range(2):
        _wait_row_gather(y_hbm, buf.at[slot, k], sem.at[slot, k])
    moe = buf[slot, 0] * r_ref[:, 2:3] + buf[slot, 1] * r_ref[:, 3:4]
    o_ref[...] = _rms(x_ref[...] + moe, g_ref[...])


def _combine_norm(x, route, rows_out, dest, g, tok0, rows):
    d = x.shape[1]
    tm = min(TOKEN_TILE, rows)
    blk0 = tok0 // tm
    grid_spec = pltpu.PrefetchScalarGridSpec(
        num_scalar_prefetch=1,
        grid=(rows // tm,),
        in_specs=[
            pl.BlockSpec((tm, d), lambda i, dest: (i + blk0, 0)),
            pl.BlockSpec((tm, LANES), lambda i, dest: (i + blk0, 0)),
            pl.BlockSpec(memory_space=pl.ANY),
            pl.BlockSpec((1, d), lambda i, dest: (0, 0)),
        ],
        out_specs=pl.BlockSpec((tm, d), lambda i, dest: (i, 0)),
        scratch_shapes=[pltpu.VMEM((2, 2, tm, d), F32), pltpu.SemaphoreType.DMA((2, 2))],
    )
    return pl.pallas_call(
        functools.partial(_combine_norm_kernel, tok0=tok0),
        grid_spec=grid_spec,
        out_shape=jax.ShapeDtypeStruct((rows, d), F32),
        compiler_params=_params(("arbitrary",)),
        name="combine_norm",
    )(dest, x, route, rows_out, g.reshape(1, d))


def _head_softmax_pv(s, v):
    m = jnp.max(s, axis=-1, keepdims=True)
    p = jnp.exp(s - m)
    return _dot(p.astype(BF16), v) / jnp.sum(p, axis=-1, keepdims=True)


def _mem_attn_shared_kernel(q_ref, kv_ref, o_ref):
    q = q_ref[...]
    k = kv_ref[0, :, 0:MEM_WIDTH].astype(BF16)
    v = kv_ref[0, :, MEM_WIDTH:].astype(BF16)
    col = lax.broadcasted_iota(jnp.int32, (1, MEM_WIDTH), 1)
    acc = jnp.zeros(q.shape, F32)
    for h in range(MEM_HEADS):
        in_head = (col >= h * MEM_HEAD_DIM) & (col < (h + 1) * MEM_HEAD_DIM)
        qh = jnp.where(in_head, q, 0.0).astype(BF16)
        s = _dot_nt(qh, k) * MEM_HEAD_DIM ** -0.5
        acc = acc + jnp.where(in_head, _head_softmax_pv(s, v), 0.0)
    o_ref[...] = acc


def _mem_attn_shared(proj, q_col, mem_kv, seq_len):
    t = proj.shape[0]
    tq = min(TOKEN_TILE, seq_len)
    per_seq = seq_len // tq
    return pl.pallas_call(
        _mem_attn_shared_kernel,
        grid=(t // tq,),
        in_specs=[
            pl.BlockSpec((tq, MEM_WIDTH), lambda i: (i, q_col)),
            pl.BlockSpec((1, N_MEM, 2 * MEM_WIDTH), lambda i: (i // per_seq, 0, 0)),
        ],
        out_specs=pl.BlockSpec((tq, MEM_WIDTH), lambda i: (i, 0)),
        out_shape=jax.ShapeDtypeStruct((t, MEM_WIDTH), F32),
        compiler_params=_params(("parallel",)),
        name="mem_attn_shared",
    )(proj, mem_kv)


def _mem_attn_decode_kernel(q_ref, k_ref, v_ref, o_ref, *, seq_len):
    rows = MEM_HEADS * seq_len
    row_head = _div(lax.broadcasted_iota(jnp.int32, (rows, MEM_WIDTH), 0), seq_len)
    col_head = _div(lax.broadcasted_iota(jnp.int32, (rows, MEM_WIDTH), 1), MEM_HEAD_DIM)
    diag = row_head == col_head
    out_head = _div(lax.broadcasted_iota(jnp.int32, (seq_len, MEM_WIDTH), 1), MEM_HEAD_DIM)
    for b in range(k_ref.shape[0]):
        q = q_ref[b * seq_len:(b + 1) * seq_len, :]
        qd = jnp.where(diag, jnp.concatenate([q] * MEM_HEADS, axis=0), 0.0).astype(BF16)
        s = _dot_nt(qd, k_ref[b].astype(BF16)) * MEM_HEAD_DIM ** -0.5
        o = _head_softmax_pv(s, v_ref[b].astype(BF16))
        acc = jnp.zeros((seq_len, MEM_WIDTH), F32)
        for h in range(MEM_HEADS):
            acc = acc + jnp.where(out_head == h, o[h * seq_len:(h + 1) * seq_len], 0.0)
        o_ref[b * seq_len:(b + 1) * seq_len, :] = acc


def _mem_attn_decode(proj, q_col, mem_k, mem_v, seq_len):
    t = proj.shape[0]
    nb = mem_k.shape[0]
    bb = min(SEQ_BATCH, nb)
    return pl.pallas_call(
        functools.partial(_mem_attn_decode_kernel, seq_len=seq_len),
        grid=(nb // bb,),
        in_specs=[
            pl.BlockSpec((bb * seq_len, MEM_WIDTH), lambda i: (i, q_col)),
            pl.BlockSpec((bb, N_MEM, MEM_WIDTH), lambda i: (i, 0, 0)),
            pl.BlockSpec((bb, N_MEM, MEM_WIDTH), lambda i: (i, 0, 0)),
        ],
        out_specs=pl.BlockSpec((bb * seq_len, MEM_WIDTH), lambda i: (i, 0)),
        out_shape=jax.ShapeDtypeStruct((t, MEM_WIDTH), F32),
        compiler_params=_params(("parallel",)),
        name="mem_attn_decode",
    )(proj, mem_k, mem_v)


def _sink_softmax_pv(s, sink, v):
    m = jnp.maximum(jnp.max(s, axis=-1, keepdims=True), sink)
    p = jnp.exp(s - m)
    denom = jnp.sum(p, axis=-1, keepdims=True) + jnp.exp(sink - m)
    return _dot(p.astype(BF16), v) / denom


def _swa_prefill_kernel(slope_ref, sink_ref, q_ref, kp_ref, kc_ref, vp_ref, vc_ref, o_ref, *, steps_per_seq):
    w = SWA_WINDOW
    first = (pl.program_id(0) % steps_per_seq) == 0
    k_all = jnp.concatenate([kp_ref[...], kc_ref[...]], axis=0)
    v_all = jnp.concatenate([vp_ref[...], vc_ref[...]], axis=0)
    qi = lax.broadcasted_iota(jnp.int32, (w, 2 * w), 0)
    kj = lax.broadcasted_iota(jnp.int32, (w, 2 * w), 1)
    dist = qi + w - kj
    in_window = (dist >= 0) & (dist < w)
    distf = dist.astype(F32)
    for j in range(q_ref.shape[0] // w):
        q = q_ref[j * w:(j + 1) * w, :]
        k = k_all[j * w:(j + 2) * w]
        v = v_all[j * w:(j + 2) * w]
        valid = in_window & ((kj >= w) | jnp.logical_not(first)) if j == 0 else in_window
        outs = []
        for kh in range(SWA_KV_HEADS):
            kk = k[:, kh * SWA_HEAD_DIM:(kh + 1) * SWA_HEAD_DIM].astype(BF16)
            vv = v[:, kh * SWA_HEAD_DIM:(kh + 1) * SWA_HEAD_DIM].astype(BF16)
            heads = [kh * SWA_GROUP + g for g in range(SWA_GROUP)]
            qg = jnp.concatenate([q[:, h * SWA_HEAD_DIM:(h + 1) * SWA_HEAD_DIM] for h in heads], axis=0).astype(BF16)
            s3 = _dot_nt(qg, kk) * SWA_HEAD_DIM ** -0.5
            for g, h in enumerate(heads):
                s = s3[g * w:(g + 1) * w] - slope_ref[h] * distf
                s = jnp.where(valid, s, NEG_INF)
                outs.append(_sink_softmax_pv(s, sink_ref[h], vv))
        o_ref[j * w:(j + 1) * w, :] = jnp.concatenate(outs, axis=1)


def _swa_prefill(proj, slopes, sinks, seq_len):
    t = proj.shape[0]
    w = SWA_WINDOW
    nq = SWA_QUERY_BLOCKS
    per_seq = seq_len // (nq * w)
    kcol = SWA_Q_WIDTH // SWA_KV_WIDTH
    vcol = kcol + 1

    def prev(i):
        return jnp.where(i % per_seq == 0, nq * i, nq * i - 1)

    smem = pl.BlockSpec(memory_space=pltpu.SMEM)
    return pl.pallas_call(
        functools.partial(_swa_prefill_kernel, steps_per_seq=per_seq),
        grid=(t // (nq * w),),
        in_specs=[
            smem,
            smem,
            pl.BlockSpec((nq * w, SWA_Q_WIDTH), lambda i: (i, 0)),
            pl.BlockSpec((w, SWA_KV_WIDTH), lambda i: (prev(i), kcol)),
            pl.BlockSpec((nq * w, SWA_KV_WIDTH), lambda i: (i, kcol)),
            pl.BlockSpec((w, SWA_KV_WIDTH), lambda i: (prev(i), vcol)),
            pl.BlockSpec((nq * w, SWA_KV_WIDTH), lambda i: (i, vcol)),
        ],
        out_specs=pl.BlockSpec((nq * w, SWA_Q_WIDTH), lambda i: (i, 0)),
        out_shape=jax.ShapeDtypeStruct((t, SWA_Q_WIDTH), F32),
        compiler_params=_params(("parallel",)),
        name="swa_prefill",
    )(slopes, sinks, proj, proj, proj, proj, proj)


def _swa_decode_kernel(q_ref, kn_ref, vn_ref, kc_ref, vc_ref, rep_ref, slope_ref, sink_ref, o_ref, *, seq_len):
    w = SWA_WINDOW
    rows = SWA_HEADS * seq_len
    span = 2 * w
    row = lax.broadcasted_iota(jnp.int32, (rows, SWA_Q_WIDTH), 0)
    col = lax.broadcasted_iota(jnp.int32, (rows, SWA_Q_WIDTH), 1)
    diag = _div(row, seq_len) == _div(col, SWA_HEAD_DIM)
    ql = lax.broadcasted_iota(jnp.int32, (rows, span), 0) & (seq_len - 1)
    kj = lax.broadcasted_iota(jnp.int32, (rows, span), 1)
    dist = w + ql - kj
    valid = (dist >= 0) & (dist < w)
    bias = slope_ref[...] * dist.astype(F32)
    sink = sink_ref[...]
    out_head = _div(lax.broadcasted_iota(jnp.int32, (seq_len, SWA_Q_WIDTH), 1), SWA_HEAD_DIM)
    rep = rep_ref[...]
    tail = jnp.zeros((w - seq_len, SWA_KV_WIDTH), F32)
    for b in range(kc_ref.shape[0]):
        rs = slice(b * seq_len, (b + 1) * seq_len)
        k = jnp.concatenate([kc_ref[b], kn_ref[rs, :], tail], axis=0).astype(BF16)
        v = jnp.concatenate([vc_ref[b], vn_ref[rs, :], tail], axis=0).astype(BF16)
        k_rep = _dot(k, rep).astype(BF16)
        v_rep = _dot(v, rep).astype(BF16)
        qd = jnp.where(diag, jnp.concatenate([q_ref[rs, :]] * SWA_HEADS, axis=0), 0.0).astype(BF16)
        s = _dot_nt(qd, k_rep) * SWA_HEAD_DIM ** -0.5 - bias
        s = jnp.where(valid, s, NEG_INF)
        o = _sink_softmax_pv(s, sink, v_rep)
        acc = jnp.zeros((seq_len, SWA_Q_WIDTH), F32)
        for h in range(SWA_HEADS):
            acc = acc + jnp.where(out_head == h, o[h * seq_len:(h + 1) * seq_len], 0.0)
        o_ref[rs, :] = acc


def _swa_decode(proj, cache_k, cache_v, slopes, sinks, seq_len):
    t = proj.shape[0]
    nb = cache_k.shape[0]
    bb = min(SEQ_BATCH, nb)
    w = SWA_WINDOW
    kcol = SWA_Q_WIDTH // SWA_KV_WIDTH
    src = (jnp.arange(SWA_Q_WIDTH) // SWA_HEAD_DIM // SWA_GROUP) * SWA_HEAD_DIM + jnp.arange(SWA_Q_WIDTH) % SWA_HEAD_DIM
    rep = (jnp.arange(SWA_KV_WIDTH)[:, None] == src[None, :]).astype(BF16)
    slope_rows = jnp.repeat(slopes, seq_len).reshape(-1, 1)
    sink_rows = jnp.repeat(sinks, seq_len).reshape(-1, 1)
    rows = SWA_HEADS * seq_len
    return pl.pallas_call(
        functools.partial(_swa_decode_kernel, seq_len=seq_len),
        grid=(nb // bb,),
        in_specs=[
            pl.BlockSpec((bb * seq_len, SWA_Q_WIDTH), lambda i: (i, 0)),
            pl.BlockSpec((bb * seq_len, SWA_KV_WIDTH), lambda i: (i, kcol)),
            pl.BlockSpec((bb * seq_len, SWA_KV_WIDTH), lambda i: (i, kcol + 1)),
            pl.BlockSpec((bb, w, SWA_KV_WIDTH), lambda i: (i, 0, 0)),
            pl.BlockSpec((bb, w, SWA_KV_WIDTH), lambda i: (i, 0, 0)),
            pl.BlockSpec((SWA_KV_WIDTH, SWA_Q_WIDTH), lambda i: (0, 0)),
            pl.BlockSpec((rows, 1), lambda i: (0, 0)),
            pl.BlockSpec((rows, 1), lambda i: (0, 0)),
        ],
        out_specs=pl.BlockSpec((bb * seq_len, SWA_Q_WIDTH), lambda i: (i, 0)),
        out_shape=jax.ShapeDtypeStruct((t, SWA_Q_WIDTH), F32),
        compiler_params=_params(("parallel",)),
        name="swa_decode",
    )(proj, proj, proj, cache_k, cache_v, rep, slope_rows, sink_rows)


def _gdn_kernel(*refs, nblk, spb):
    r = GDN_ROWS
    c = r // spb
    qkv_refs = refs[0:nblk]
    z_refs = refs[nblk:2 * nblk]
    ba_refs = refs[2 * nblk:3 * nblk]
    conv0_ref, s0_ref, cw_ref, alog_ref, dtb_ref, gn_ref, o_ref, sfin_ref, s_scr, fbuf = refs[3 * nblk:]
    step = pl.program_id(1)
    nseq = nblk * spb
    tail = fbuf.shape[1]
    hist = GDN_CONV_K - 1

    @pl.when(step == 0)
    def _():
        s_scr[...] = s0_ref[...]
        fbuf[...] = jnp.zeros_like(fbuf)
        for s in range(nseq):
            fbuf[s, tail - hist:tail, :] = conv0_ref[s]

    ri = lax.broadcasted_iota(jnp.int32, (r, r), 0)
    ci = lax.broadcasted_iota(jnp.int32, (r, r), 1)
    same = _div(ri, c) == _div(ci, c)
    tri = same & (ri >= ci)
    strict = same & (ri > ci)
    eye = (ri == ci).astype(F32)
    cum_mat = jnp.concatenate([tri.astype(F32), same.astype(F32)], axis=0)
    row_seq = _div(lax.broadcasted_iota(jnp.int32, (r, 1), 0), c)
    row_tail = lax.broadcasted_iota(jnp.int32, (tail, 1), 0)
    zeros_rr = jnp.zeros((r, LANES), F32)
    cw = cw_ref[...]
    neg_a = -jnp.exp(alog_ref[...])
    gn = gn_ref[...]

    blocks = []
    for n in range(nblk):
        u = qkv_refs[n][...]
        pieces = []
        for s in range(spb):
            idx = n * spb + s
            us = u[s * c:(s + 1) * c]
            prev = fbuf[idx]
            acc = us * cw[hist:hist + 1]
            for back in range(1, GDN_CONV_K):
                moved = pltpu.roll(us, back, 0)
                head = jnp.where(row_tail < back, pltpu.roll(prev, back, 0), moved[0:tail])
                moved = head if c == tail else jnp.concatenate([head, moved[tail:]], axis=0)
                acc = acc + moved * cw[hist - back:hist - back + 1]
            fbuf[idx] = us[c - tail:c]
            pieces.append(acc)
        conv = pieces[0] if spb == 1 else jnp.concatenate(pieces, axis=0)
        qkv = jax.nn.silu(conv)

        ba = ba_refs[n][...]
        beta_all = jax.nn.sigmoid(ba)
        xg = ba + dtb_ref[...]
        g_all = neg_a * (jnp.maximum(xg, 0.0) + jnp.log(1.0 + jnp.exp(-jnp.abs(xg))))
        gsum = jnp.dot(cum_mat, g_all, preferred_element_type=F32, precision=HIGHEST)
        gcum = gsum[0:r]
        gtot = gsum[r:2 * r]
        gcum_t = jnp.transpose(jnp.concatenate([gcum, zeros_rr], axis=0))

        blocks.append((qkv, beta_all, gcum, gtot, gcum_t))

    items = [(n, h) for n in range(nblk) for h in range(GDN_HEADS)]
    bf = lambda x: x.astype(BF16)
    k_n, kb_n, q_n, qk_dec, vb_n, kdec_t, decay_n, glast_n = [], [], [], [], [], [], [], []
    for n, h in items:
        qkv, beta_all, gcum, gtot, gcum_t = blocks[n]
        lo = h * GDN_DK
        q = qkv[:, lo:lo + GDN_DK]
        k = qkv[:, GDN_QK_WIDTH + lo:GDN_QK_WIDTH + lo + GDN_DK]
        v = qkv[:, 2 * GDN_QK_WIDTH + lo:2 * GDN_QK_WIDTH + lo + GDN_DV]
        q = q * lax.rsqrt(jnp.sum(q * q, axis=-1, keepdims=True) + L2_EPS) * GDN_DK ** -0.5
        k = k * lax.rsqrt(jnp.sum(k * k, axis=-1, keepdims=True) + L2_EPS)
        beta = beta_all[:, h:h + 1]
        gcol = gcum[:, GDN_HEADS + h:GDN_HEADS + h + 1]
        grow = gcum_t[GDN_HEADS + h:GDN_HEADS + h + 1, 0:r]
        glast = gtot[:, GDN_HEADS + h:GDN_HEADS + h + 1]
        eg = jnp.exp(gcol)
        kb = k * beta
        decay_n.append(jnp.where(tri, jnp.exp(jnp.where(tri, gcol - grow, 0.0)), 0.0))
        k_n.append(bf(k))
        kb_n.append(kb)
        q_n.append(q)
        vb_n.append(v * beta)
        qk_dec.append(bf(jnp.concatenate([kb * eg, q * eg], axis=0)))
        kdec = jnp.concatenate([k * jnp.exp(glast - gcol), zeros_rr], axis=0)
        kdec_t.append(bf(jnp.transpose(kdec)[:, 0:r]))
        glast_n.append(glast)

    kq = [_dot_nt(bf(jnp.concatenate([kb_n[i], q_n[i]], axis=0)), k_n[i]) for i in range(len(items))]
    power = [jnp.where(strict, kq[i][0:r] * decay_n[i], 0.0) for i in range(len(items))]
    qk = [bf(kq[i][r:2 * r] * decay_n[i]) for i in range(len(items))]
    inv = [eye - p for p in power]
    span = 2
    while span < c:
        power = [_dot(bf(p), bf(p)) for p in power]
        inv = [_dot(bf(a), bf(eye + p)) for a, p in zip(inv, power)]
        span *= 2
    inv = [bf(a) for a in inv]
    tq = [jnp.concatenate([a, bf(_dot(b, a))], axis=0) for a, b in zip(inv, qk)]

    if spb == 1:
        ks_qs = [_dot(qk_dec[i], bf(s_scr[n, h])) for i, (n, h) in enumerate(items)]
        resid = [vb_n[i] - ks_qs[i][0:r] for i in range(len(items))]
        qs = [x[r:2 * r] for x in ks_qs]
    else:
        resid, qs = [], []
        for i, (n, h) in enumerate(items):
            both = []
            for s in range(spb):
                rows = jnp.concatenate([qk_dec[i][s * c:(s + 1) * c], qk_dec[i][r + s * c:r + (s + 1) * c]], axis=0)
                both.append(_dot(rows, bf(s_scr[n * spb + s, h])))
            resid.append(vb_n[i] - jnp.concatenate([x[0:c] for x in both], axis=0))
            qs.append(jnp.concatenate([x[c:2 * c] for x in both], axis=0))

    vo = [_dot(tq[i], bf(resid[i])) for i in range(len(items))]
    for i, (n, h) in enumerate(items):
        v_new = vo[i][0:r]
        for s in range(spb):
            idx = n * spb + s
            vs = v_new if spb == 1 else jnp.where(row_seq == s, v_new, 0.0)
            carry = jnp.exp(glast_n[i][s * c:s * c + 1, :])
            s_scr[idx, h] = s_scr[idx, h] * carry + _dot(kdec_t[i], bf(vs))
    for i, (n, h) in enumerate(items):
        on = _rms(qs[i] + vo[i][r:2 * r], gn)
        zh = z_refs[n][:, h * GDN_DV:(h + 1) * GDN_DV]
        o_ref[n, :, h * GDN_DV:(h + 1) * GDN_DV] = on * jax.nn.silu(zh)

    @pl.when(step == pl.num_programs(1) - 1)
    def _():
        sfin_ref[...] = s_scr[...]


def _gdn(proj, conv_buf, s0, conv_w, a_log, dt_bias, norm_g, *, nblk, spb, steps):
    t = proj.shape[0]
    r = GDN_ROWS
    groups = t // (r * nblk * steps)
    nseq = nblk * spb
    c = r // spb
    z_col = GDN_CONV_DIM // GDN_V_WIDTH
    ba_col = (GDN_CONV_DIM + GDN_V_WIDTH + MEM_WIDTH) // LANES

    def rows(n, col):
        return lambda g, l: ((g * nblk + n) * steps + l, col)

    lane6 = jnp.zeros((1, LANES), F32)
    alog = lane6.at[0, GDN_HEADS:2 * GDN_HEADS].set(a_log)
    dtb = lane6.at[0, GDN_HEADS:2 * GDN_HEADS].set(dt_bias)
    const = lambda g, l: (0, 0)
    in_specs = (
        [pl.BlockSpec((r, GDN_CONV_DIM), rows(n, 0)) for n in range(nblk)]
        + [pl.BlockSpec((r, GDN_V_WIDTH), rows(n, z_col)) for n in range(nblk)]
        + [pl.BlockSpec((r, LANES), rows(n, ba_col)) for n in range(nblk)]
        + [
            pl.BlockSpec((nseq, GDN_CONV_K - 1, GDN_CONV_DIM), lambda g, l: (g, 0, 0)),
            pl.BlockSpec((nseq, GDN_HEADS, GDN_DK, GDN_DV), lambda g, l: (g, 0, 0, 0)),
            pl.BlockSpec((GDN_CONV_K, GDN_CONV_DIM), const),
            pl.BlockSpec((1, LANES), const),
            pl.BlockSpec((1, LANES), const),
            pl.BlockSpec((1, GDN_DV), const),
        ]
    )
    out, s_fin = pl.pallas_call(
        functools.partial(_gdn_kernel, nblk=nblk, spb=spb),
        grid=(groups, steps),
        in_specs=in_specs,
        out_specs=[
            pl.BlockSpec((nblk, r, GDN_V_WIDTH), lambda g, l: (0, g * steps + l, 0)),
            pl.BlockSpec((nseq, GDN_HEADS, GDN_DK, GDN_DV), lambda g, l: (g, 0, 0, 0)),
        ],
        out_shape=[
            jax.ShapeDtypeStruct((nblk, groups * steps * r, GDN_V_WIDTH), F32),
            jax.ShapeDtypeStruct((groups * nseq, GDN_HEADS, GDN_DK, GDN_DV), F32),
        ],
        scratch_shapes=[
            pltpu.VMEM((nseq, GDN_HEADS, GDN_DK, GDN_DV), F32),
            pltpu.VMEM((nseq, 8, GDN_CONV_DIM), F32),
        ],
        compiler_params=_params(("parallel", "arbitrary")),
        name="gdn",
    )(*([proj] * (3 * nblk)), conv_buf, s0, conv_w, alog, dtb, norm_g.reshape(1, GDN_DV))
    return out.reshape(t, GDN_V_WIDTH), s_fin


def _dispatch(route, tm, p):
    t = route.shape[0]
    experts = route[:, 0:2].astype(jnp.int32).reshape(-1)
    onehot = (experts[:, None] == jnp.arange(N_EXPERTS)[None, :]).astype(jnp.int32)
    csum = jnp.cumsum(onehot, axis=0)
    rank = jnp.take_along_axis(csum, experts[:, None], axis=1)[:, 0] - 1
    tiles = (csum[-1] + tm - 1) // tm
    tile_end = jnp.cumsum(tiles)
    start = (tile_end - tiles) * tm
    dest = start[experts] + rank
    n_tiles = tile_end[-1:].astype(jnp.int32)
    tile_expert = jnp.sum(tile_end[None, :] <= jnp.arange(p // tm)[:, None], axis=1)
    tile_expert = jnp.minimum(tile_expert, N_EXPERTS - 1).astype(jnp.int32)
    tile_in_expert = jnp.arange(p // tm) - (tile_end - tiles)[tile_expert]
    tile_used = jnp.clip(csum[-1][tile_expert] - tile_in_expert * tm, 0, tm).astype(jnp.int32)
    src = jnp.zeros((p,), jnp.int32).at[dest].set(jnp.arange(2 * t, dtype=jnp.int32) // 2)
    return dest, src, tile_expert, n_tiles, tile_used


def kernel(x_prompt, x_sample, state_gdn_conv, state_gdn_ssm, cache_swa_k, cache_swa_v, cache_mem_k, cache_mem_v, mem_prompt, attn_norm, ffn_norm, mem_norm, final_norm, w_in_gdn, gdn_conv_w, gdn_a_log, gdn_dt_bias, gdn_norm, w_out_gdn, w_in_swa, swa_sinks, w_out_swa, w_mem_kv, w_ffn_gu, w_ffn_down, w_router, w_exp_gu, w_exp_down):
    bp, lp, d = x_prompt.shape
    bs, ls, _ = x_sample.shape
    tp, ts = bp * lp, bs * ls
    xp = x_prompt.reshape(tp, d)
    xs = x_sample.reshape(ts, d)

    mem = mem_prompt.reshape(bp * N_MEM, d)
    mem_kv = [
        _norm_matmul(mem, mem_norm[i], w_mem_kv[i].astype(BF16)).reshape(bp, N_MEM, 2 * MEM_WIDTH)
        for i in range(2)
    ]
    new_mem_k = jnp.stack([kv[..., :MEM_WIDTH].reshape(bp, N_MEM, MEM_HEADS, MEM_HEAD_DIM) for kv in mem_kv])
    new_mem_v = jnp.stack([kv[..., MEM_WIDTH:].reshape(bp, N_MEM, MEM_HEADS, MEM_HEAD_DIM) for kv in mem_kv])

    w_in = w_in_gdn[0]
    o_z = GDN_CONV_DIM + GDN_V_WIDTH
    o_mem = o_z + 2 * GDN_HEADS
    w0 = jnp.concatenate(
        [w_in[:, :o_z], w_in[:, o_mem:], w_in[:, o_z:o_mem], jnp.zeros((d, LANES - 2 * GDN_HEADS), F32)], axis=1
    ).astype(BF16)
    mq_col0 = o_z // MEM_WIDTH
    w_out0 = w_out_gdn[0].astype(BF16)
    w_gu0 = w_ffn_gu[0].astype(BF16)
    w_dn0 = w_ffn_down[0].astype(BF16)

    proj_p = _norm_matmul(xp, attn_norm[0], w0)
    proj_s = _norm_matmul(xs, attn_norm[0], w0)

    zero_conv = jnp.zeros((bp, GDN_CONV_K - 1, GDN_CONV_DIM), F32)
    zero_state = jnp.zeros((bp, GDN_HEADS, GDN_DK, GDN_DV), F32)
    gdn_p, ssm_p = _gdn(proj_p, zero_conv, zero_state, gdn_conv_w[0], gdn_a_log[0], gdn_dt_bias[0], gdn_norm[0],
                        nblk=bp, spb=1, steps=lp // GDN_ROWS)
    gdn_s, ssm_s = _gdn(proj_s, state_gdn_conv[0], state_gdn_ssm[0], gdn_conv_w[0], gdn_a_log[0], gdn_dt_bias[0],
                        gdn_norm[0], nblk=1, spb=GDN_ROWS // ls, steps=1)
    hist = GDN_CONV_K - 1
    conv_p = proj_p.reshape(bp, lp, -1)[:, lp - hist:, :GDN_CONV_DIM]
    conv_s = proj_s.reshape(bs, ls, -1)[:, ls - hist:, :GDN_CONV_DIM]

    memo_p = _mem_attn_shared(proj_p, mq_col0, mem_kv[0], lp)
    memo_s = _mem_attn_decode(proj_s, mq_col0, cache_mem_k[0].reshape(bs, N_MEM, MEM_WIDTH),
                              cache_mem_v[0].reshape(bs, N_MEM, MEM_WIDTH), ls)
    xp = _mixer_ffn(xp, gdn_p, memo_p, w_out0, ffn_norm[0], w_gu0, w_dn0)
    xs = _mixer_ffn(xs, gdn_s, memo_s, w_out0, ffn_norm[0], w_gu0, w_dn0)

    w1 = w_in_swa[0].astype(BF16)
    mq_col1 = (SWA_Q_WIDTH + 2 * SWA_KV_WIDTH) // MEM_WIDTH
    w_out1 = w_out_swa[0].astype(BF16)
    slopes = 2.0 ** (-8.0 * jnp.arange(1, SWA_HEADS + 1, dtype=F32) / SWA_HEADS)
    sinks = swa_sinks[0].astype(F32)

    proj_p = _norm_matmul(xp, attn_norm[1], w1)
    proj_s = _norm_matmul(xs, attn_norm[1], w1)
    swa_p = _swa_prefill(proj_p, slopes, sinks, lp)
    cache_k = cache_swa_k[0].reshape(bs, SWA_WINDOW, SWA_KV_WIDTH)
    cache_v = cache_swa_v[0].reshape(bs, SWA_WINDOW, SWA_KV_WIDTH)
    swa_s = _swa_decode(proj_s, cache_k, cache_v, slopes, sinks, ls)

    k0, v0 = SWA_Q_WIDTH, SWA_Q_WIDTH + SWA_KV_WIDTH
    pp = proj_p.reshape(bp, lp, -1)
    ps = proj_s.reshape(bs, ls, -1)
    kv_shape = (SWA_KV_HEADS, SWA_HEAD_DIM)
    swk_p = pp[:, lp - SWA_WINDOW:, k0:k0 + SWA_KV_WIDTH].reshape(bp, SWA_WINDOW, *kv_shape)
    swv_p = pp[:, lp - SWA_WINDOW:, v0:v0 + SWA_KV_WIDTH].reshape(bp, SWA_WINDOW, *kv_shape)
    swk_s = jnp.concatenate([cache_k[:, ls:], ps[:, :, k0:k0 + SWA_KV_WIDTH]], axis=1).reshape(bs, SWA_WINDOW, *kv_shape)
    swv_s = jnp.concatenate([cache_v[:, ls:], ps[:, :, v0:v0 + SWA_KV_WIDTH]], axis=1).reshape(bs, SWA_WINDOW, *kv_shape)

    memo_p = _mem_attn_shared(proj_p, mq_col1, mem_kv[1], lp)
    memo_s = _mem_attn_decode(proj_s, mq_col1, cache_mem_k[1].reshape(bs, N_MEM, MEM_WIDTH),
                              cache_mem_v[1].reshape(bs, N_MEM, MEM_WIDTH), ls)
    t_all = tp + ts
    w_r = jnp.concatenate([w_router[0], jnp.zeros((d, LANES - N_EXPERTS), F32)], axis=1)
    x_all, route = _out_proj_route((xp, swa_p, memo_p), (xs, swa_s, memo_s), w_out1, ffn_norm[1], w_r)
    tm = MOE_TILE
    p_rows = -(-(2 * t_all + N_EXPERTS * (tm - 1)) // tm) * tm
    dest, src, tile_expert, n_tiles, tile_used = _dispatch(route, tm, p_rows)
    rows_out = _moe_ffn(x_all, ffn_norm[1], src, tile_expert, n_tiles, tile_used,
                        w_exp_gu[0].astype(BF16), w_exp_down[0].astype(BF16))
    y_p = _combine_norm(x_all, route, rows_out, dest, final_norm, 0, tp)
    y_s = _combine_norm(x_all, route, rows_out, dest, final_norm, tp, ts)

    return (
        y_p.reshape(bp, lp, d),
        y_s.reshape(bs, ls, d),
        conv_p[None],
        ssm_p[None],
        swk_p[None],
        swv_p[None],
        new_mem_k,
        new_mem_v,
        conv_s[None],
        ssm_s[None],
        swk_s[None],
        swv_s[None],
    )
```

```python
import functools

import jax
import jax.numpy as jnp
from jax import lax
from jax.experimental import pallas as pl
from jax.experimental.pallas import tpu as pltpu

F32 = jnp.float32
BF16 = jnp.bfloat16
HIGHEST = lax.Precision.HIGHEST

D_MODEL = 1024
RMS_EPS = 1e-6
L2_EPS = 1e-6
NEG_INF = -1e30

GDN_HEADS = 6
GDN_DK = 128
GDN_DV = 128
GDN_CONV_K = 4
GDN_QK_WIDTH = GDN_HEADS * GDN_DK
GDN_V_WIDTH = GDN_HEADS * GDN_DV
GDN_CONV_DIM = 2 * GDN_QK_WIDTH + GDN_V_WIDTH
GDN_ROWS = 64

SWA_HEADS = 12
SWA_KV_HEADS = 4
SWA_HEAD_DIM = 64
SWA_GROUP = SWA_HEADS // SWA_KV_HEADS
SWA_WINDOW = 128
SWA_Q_WIDTH = SWA_HEADS * SWA_HEAD_DIM
SWA_KV_WIDTH = SWA_KV_HEADS * SWA_HEAD_DIM

N_MEM = 256
MEM_HEADS = 4
MEM_HEAD_DIM = 64
MEM_WIDTH = MEM_HEADS * MEM_HEAD_DIM

FFN_DIM = 2816
N_EXPERTS = 8
EXPERT_DIM = 3584

LANES = 128
SUBLANES = 8
VMEM_LIMIT = 56 * 1024 * 1024

TOKEN_TILE = 512
MOE_TILE = 512
FFN_COLS = 1408
EXPERT_STEPS = 2
EXPERT_COLS = EXPERT_DIM // EXPERT_STEPS
SEQ_BATCH = 8
SWA_QUERY_BLOCKS = 1


def _params(sem):
    return pltpu.CompilerParams(dimension_semantics=sem, vmem_limit_bytes=VMEM_LIMIT)


def _rms(x, g):
    return x * lax.rsqrt(jnp.mean(x * x, axis=-1, keepdims=True) + RMS_EPS) * g


def _dot(a, b):
    return jnp.dot(a, b, preferred_element_type=F32)


def _div(x, d):
    assert d & (d - 1) == 0
    return lax.shift_right_logical(x, d.bit_length() - 1)


def _dot_nt(a, b):
    return lax.dot_general(a, b, (((1,), (1,)), ((), ())), preferred_element_type=F32)


def _norm_matmul_kernel(x_ref, g_ref, w_ref, o_ref):
    o_ref[...] = _dot(_rms(x_ref[...], g_ref[...]).astype(BF16), w_ref[...])


def _norm_matmul(x, g, w):
    t, k = x.shape
    n = w.shape[1]
    tm = min(TOKEN_TILE, t)
    return pl.pallas_call(
        _norm_matmul_kernel,
        grid=(t // tm,),
        in_specs=[
            pl.BlockSpec((tm, k), lambda i: (i, 0)),
            pl.BlockSpec((1, k), lambda i: (0, 0)),
            pl.BlockSpec((k, n), lambda i: (0, 0)),
        ],
        out_specs=pl.BlockSpec((tm, n), lambda i: (i, 0)),
        out_shape=jax.ShapeDtypeStruct((t, n), F32),
        compiler_params=_params(("parallel",)),
        name="norm_matmul",
    )(x, g.reshape(1, k), w)


def _out_proj_block(x_ref, a_ref, m_ref, w_ref, o_ref):
    na = a_ref.shape[1]
    y = _dot(a_ref[...].astype(BF16), w_ref[0:na, :])
    y = y + _dot(m_ref[...].astype(BF16), w_ref[na:, :])
    o_ref[...] = x_ref[...] + y


def _route(h, w_router):
    logits = jnp.dot(h, w_router, preferred_element_type=F32, precision=HIGHEST)
    lane = lax.broadcasted_iota(jnp.int32, logits.shape, 1).astype(F32)
    logits = jnp.where(lane < N_EXPERTS, logits, -jnp.inf)
    m1 = jnp.max(logits, axis=-1, keepdims=True)
    i1 = jnp.min(jnp.where(logits == m1, lane, float(LANES)), axis=-1, keepdims=True)
    rest = jnp.where(lane == i1, -jnp.inf, logits)
    m2 = jnp.max(rest, axis=-1, keepdims=True)
    i2 = jnp.min(jnp.where(rest == m2, lane, float(LANES)), axis=-1, keepdims=True)
    e2 = jnp.exp(m2 - m1)
    w1 = 1.0 / (1.0 + e2)
    w2 = e2 / (1.0 + e2)
    r = jnp.where(lane == 0, i1, 0.0)
    r = jnp.where(lane == 1, i2, r)
    r = jnp.where(lane == 2, w1, r)
    return jnp.where(lane == 3, w2, r)


def _out_proj_route_kernel(x0_ref, a0_ref, m0_ref, x1_ref, a1_ref, m1_ref, w_ref, g_ref, wr_ref,
                           o_ref, r_ref, *, n0):
    i = pl.program_id(0)

    @pl.when(i < n0)
    def _():
        _out_proj_block(x0_ref, a0_ref, m0_ref, w_ref, o_ref)

    @pl.when(i >= n0)
    def _():
        _out_proj_block(x1_ref, a1_ref, m1_ref, w_ref, o_ref)

    r_ref[...] = _route(_rms(o_ref[...], g_ref[...]), wr_ref[...])


def _out_proj_route(first, second, w, g, w_router_pad):
    (x0, a0, m0), (x1, a1, m1) = first, second
    d = x0.shape[1]
    tm = min(TOKEN_TILE, x0.shape[0], x1.shape[0])
    n0, n1 = x0.shape[0] // tm, x1.shape[0] // tm

    def group0(width):
        return pl.BlockSpec((tm, width), lambda i: (jnp.minimum(i, n0 - 1), 0))

    def group1(width):
        return pl.BlockSpec((tm, width), lambda i: (jnp.maximum(i - n0, 0), 0))

    return pl.pallas_call(
        functools.partial(_out_proj_route_kernel, n0=n0),
        grid=(n0 + n1,),
        in_specs=[
            group0(d), group0(a0.shape[1]), group0(m0.shape[1]),
            group1(d), group1(a1.shape[1]), group1(m1.shape[1]),
            pl.BlockSpec(w.shape, lambda i: (0, 0)),
            pl.BlockSpec((1, d), lambda i: (0, 0)),
            pl.BlockSpec((d, LANES), lambda i: (0, 0)),
        ],
        out_specs=[
            pl.BlockSpec((tm, d), lambda i: (i, 0)),
            pl.BlockSpec((tm, LANES), lambda i: (i, 0)),
        ],
        out_shape=[
            jax.ShapeDtypeStruct(((n0 + n1) * tm, d), F32),
            jax.ShapeDtypeStruct(((n0 + n1) * tm, LANES), F32),
        ],
        compiler_params=_params(("parallel",)),
        name="out_proj_route",
    )(x0, a0, m0, x1, a1, m1, w, g.reshape(1, d), w_router_pad)


def _mixer_ffn_kernel(x_ref, a_ref, m_ref, wo_ref, g_ref, wg_ref, wu_ref, wd_ref, o_ref, x1_ref, h_ref, acc_ref):
    f = pl.program_id(1)

    @pl.when(f == 0)
    def _():
        _out_proj_block(x_ref, a_ref, m_ref, wo_ref, x1_ref)
        h_ref[...] = _rms(x1_ref[...], g_ref[...]).astype(BF16)
        acc_ref[...] = jnp.zeros_like(acc_ref)

    h = h_ref[...]
    a = jax.nn.silu(_dot(h, wg_ref[...])) * _dot(h, wu_ref[...])
    acc_ref[...] += _dot(a.astype(BF16), wd_ref[...])

    @pl.when(f == pl.num_programs(1) - 1)
    def _():
        o_ref[...] = x1_ref[...] + acc_ref[...]


def _mixer_ffn(x, a, m, w_out, g, w_gu, w_down):
    t, d = x.shape
    tm = min(TOKEN_TILE, t)
    tf = FFN_COLS
    nf = FFN_DIM // tf
    return pl.pallas_call(
        _mixer_ffn_kernel,
        grid=(t // tm, nf),
        in_specs=[
            pl.BlockSpec((tm, d), lambda i, f: (i, 0)),
            pl.BlockSpec((tm, a.shape[1]), lambda i, f: (i, 0)),
            pl.BlockSpec((tm, m.shape[1]), lambda i, f: (i, 0)),
            pl.BlockSpec(w_out.shape, lambda i, f: (0, 0)),
            pl.BlockSpec((1, d), lambda i, f: (0, 0)),
            pl.BlockSpec((d, tf), lambda i, f: (0, f)),
            pl.BlockSpec((d, tf), lambda i, f: (0, nf + f)),
            pl.BlockSpec((tf, d), lambda i, f: (f, 0)),
        ],
        out_specs=pl.BlockSpec((tm, d), lambda i, f: (i, 0)),
        out_shape=jax.ShapeDtypeStruct((t, d), F32),
        scratch_shapes=[pltpu.VMEM((tm, d), F32), pltpu.VMEM((tm, d), BF16), pltpu.VMEM((tm, d), F32)],
        compiler_params=_params(("parallel", "arbitrary")),
        name="mixer_ffn",
    )(x, a, m, w_out, g.reshape(1, d), w_gu, w_gu, w_down)


def _start_row_gather(src_hbm, idx_ref, dst, sem):
    def body(r, carry):
        row = idx_ref[r]
        pltpu.make_async_copy(src_hbm.at[pl.ds(row, 1)], dst.at[pl.ds(r, 1)], sem).start()
        return carry

    lax.fori_loop(0, dst.shape[0], body, 0, unroll=8)


def _wait_row_gather(src_hbm, dst, sem):
    pltpu.make_async_copy(src_hbm.at[pl.ds(0, dst.shape[0])], dst, sem).wait()


def _moe_kernel(te_ref, nt_ref, src_ref, x_hbm, g_ref, wg_ref, wu_ref, wd_ref, o_ref,
                xbuf, h_ref, acc_ref, sem):
    i = pl.program_id(0)
    f = pl.program_id(1)
    tm = h_ref.shape[0]
    nt = nt_ref[0]
    slot = i % 2

    @pl.when((i == 0) & (f == 0))
    def _():
        _start_row_gather(x_hbm, src_ref, xbuf.at[0], sem.at[0])

    nf = pl.num_programs(1)
    share = tm // EXPERT_STEPS

    def column_step(prefetch):
        if prefetch:
            for r in range(share):
                row = src_ref[(i + 1) * tm + f * share + r]
                pltpu.make_async_copy(
                    x_hbm.at[pl.ds(row, 1)], xbuf.at[1 - slot, pl.ds(f * share + r, 1)], sem.at[1 - slot]
                ).start()
        h = h_ref[...]
        a = jax.nn.silu(_dot(h, wg_ref[...])) * _dot(h, wu_ref[...])
        acc_ref[...] += _dot(a.astype(BF16), wd_ref[...])

    @pl.when(i < nt)
    def _():
        @pl.when(f == 0)
        def _():
            _wait_row_gather(x_hbm, xbuf.at[slot], sem.at[slot])
            h_ref[...] = _rms(xbuf[slot], g_ref[...]).astype(BF16)
            acc_ref[...] = jnp.zeros_like(acc_ref)

        @pl.when(i + 1 < nt)
        def _():
            column_step(True)

        @pl.when(i + 1 >= nt)
        def _():
            column_step(False)

        @pl.when(f == nf - 1)
        def _():
            o_ref[...] = acc_ref[...]

    @pl.when((i >= nt) & (f == 0))
    def _():
        o_ref[...] = jnp.zeros_like(o_ref)


def _moe_ffn(x, g, src, tile_expert, n_tiles, w_gu, w_down):
    d = x.shape[1]
    p = src.shape[0]
    tm = MOE_TILE
    tf = EXPERT_COLS
    nf = EXPERT_DIM // tf

    def col(i, f, nt):
        return jnp.where(i < nt[0], f, nf - 1)

    grid_spec = pltpu.PrefetchScalarGridSpec(
        num_scalar_prefetch=3,
        grid=(p // tm, nf),
        in_specs=[
            pl.BlockSpec(memory_space=pl.ANY),
            pl.BlockSpec((1, d), lambda i, f, te, nt, src: (0, 0)),
            pl.BlockSpec((None, d, tf), lambda i, f, te, nt, src: (te[i], 0, col(i, f, nt))),
            pl.BlockSpec((None, d, tf), lambda i, f, te, nt, src: (te[i], 0, nf + col(i, f, nt))),
            pl.BlockSpec((None, tf, d), lambda i, f, te, nt, src: (te[i], col(i, f, nt), 0)),
        ],
        out_specs=pl.BlockSpec((tm, d), lambda i, f, te, nt, src: (i, 0)),
        scratch_shapes=[
            pltpu.VMEM((2, tm, d), F32),
            pltpu.VMEM((tm, d), BF16),
            pltpu.VMEM((tm, d), F32),
            pltpu.SemaphoreType.DMA((2,)),
        ],
    )
    return pl.pallas_call(
        _moe_kernel,
        grid_spec=grid_spec,
        out_shape=jax.ShapeDtypeStruct((p, d), F32),
        compiler_params=_params(("arbitrary", "arbitrary")),
        name="moe_ffn",
    )(tile_expert, n_tiles, src, x, g.reshape(1, d), w_gu, w_gu, w_down)


def _combine_norm_kernel(dest_ref, x_ref, r_ref, y_hbm, g_ref, o_ref, buf, sem, *, tok0):
    i = pl.program_id(0)
    tm = x_ref.shape[0]
    slot = i % 2

    def fetch(tile, s):
        base = 2 * (tok0 + tile * tm)

        def body(r8, carry):
            for j in range(SUBLANES):
                for k in range(2):
                    row = dest_ref[base + 2 * (r8 * SUBLANES + j) + k]
                    pltpu.make_async_copy(
                        y_hbm.at[pl.ds(row, 1)], buf.at[s, k, r8, pl.ds(j, 1)], sem.at[s, k]
                    ).start(priority=k)
            return carry

        lax.fori_loop(0, tm // SUBLANES, body, 0)

    @pl.when(i == 0)
    def _():
        fetch(0, 0)

    @pl.when(i + 1 < pl.num_programs(0))
    def _():
        fetch(i + 1, 1 - slot)

    for k in range(2):
        pltpu.make_async_copy(buf.at[slot, k], buf.at[slot, k], sem.at[slot, k]).wait()
    picks = [buf[slot, k].reshape(tm, buf.shape[-1]) for k in range(2)]
    moe = picks[0] * r_ref[:, 2:3] + picks[1] * r_ref[:, 3:4]
    o_ref[...] = _rms(x_ref[...] + moe, g_ref[...])


def _combine_norm(x, route, rows_out, dest, g, tok0, rows):
    d = x.shape[1]
    tm = min(TOKEN_TILE, rows)
    blk0 = tok0 // tm
    grid_spec = pltpu.PrefetchScalarGridSpec(
        num_scalar_prefetch=1,
        grid=(rows // tm,),
        in_specs=[
            pl.BlockSpec((tm, d), lambda i, dest: (i + blk0, 0)),
            pl.BlockSpec((tm, LANES), lambda i, dest: (i + blk0, 0)),
            pl.BlockSpec(memory_space=pl.ANY),
            pl.BlockSpec((1, d), lambda i, dest: (0, 0)),
        ],
        out_specs=pl.BlockSpec((tm, d), lambda i, dest: (i, 0)),
        scratch_shapes=[pltpu.VMEM((2, 2, tm // SUBLANES, SUBLANES, d), F32), pltpu.SemaphoreType.DMA((2, 2))],
    )
    return pl.pallas_call(
        functools.partial(_combine_norm_kernel, tok0=tok0),
        grid_spec=grid_spec,
        out_shape=jax.ShapeDtypeStruct((rows, d), F32),
        compiler_params=_params(("arbitrary",)),
        name="combine_norm",
    )(dest, x, route, rows_out, g.reshape(1, d))


def _head_softmax_pv(s, v):
    m = jnp.max(s, axis=-1, keepdims=True)
    p = jnp.exp(s - m)
    return _dot(p.astype(BF16), v) / jnp.sum(p, axis=-1, keepdims=True)


def _mem_attn_shared_kernel(q_ref, kv_ref, o_ref):
    q = q_ref[...]
    k = kv_ref[0, :, 0:MEM_WIDTH].astype(BF16)
    v = kv_ref[0, :, MEM_WIDTH:].astype(BF16)
    col = lax.broadcasted_iota(jnp.int32, (1, MEM_WIDTH), 1)
    acc = jnp.zeros(q.shape, F32)
    for h in range(MEM_HEADS):
        in_head = (col >= h * MEM_HEAD_DIM) & (col < (h + 1) * MEM_HEAD_DIM)
        qh = jnp.where(in_head, q, 0.0).astype(BF16)
        s = _dot_nt(qh, k) * MEM_HEAD_DIM ** -0.5
        acc = acc + jnp.where(in_head, _head_softmax_pv(s, v), 0.0)
    o_ref[...] = acc


def _mem_attn_shared(proj, q_col, mem_kv, seq_len):
    t = proj.shape[0]
    tq = min(TOKEN_TILE, seq_len)
    per_seq = seq_len // tq
    return pl.pallas_call(
        _mem_attn_shared_kernel,
        grid=(t // tq,),
        in_specs=[
            pl.BlockSpec((tq, MEM_WIDTH), lambda i: (i, q_col)),
            pl.BlockSpec((1, N_MEM, 2 * MEM_WIDTH), lambda i: (i // per_seq, 0, 0)),
        ],
        out_specs=pl.BlockSpec((tq, MEM_WIDTH), lambda i: (i, 0)),
        out_shape=jax.ShapeDtypeStruct((t, MEM_WIDTH), F32),
        compiler_params=_params(("parallel",)),
        name="mem_attn_shared",
    )(proj, mem_kv)


def _mem_attn_decode_kernel(q_ref, k_ref, v_ref, o_ref, *, seq_len):
    rows = MEM_HEADS * seq_len
    row_head = _div(lax.broadcasted_iota(jnp.int32, (rows, MEM_WIDTH), 0), seq_len)
    col_head = _div(lax.broadcasted_iota(jnp.int32, (rows, MEM_WIDTH), 1), MEM_HEAD_DIM)
    diag = row_head == col_head
    out_head = _div(lax.broadcasted_iota(jnp.int32, (seq_len, MEM_WIDTH), 1), MEM_HEAD_DIM)
    for b in range(k_ref.shape[0]):
        q = q_ref[b * seq_len:(b + 1) * seq_len, :]
        qd = jnp.where(diag, jnp.concatenate([q] * MEM_HEADS, axis=0), 0.0).astype(BF16)
        s = _dot_nt(qd, k_ref[b].astype(BF16)) * MEM_HEAD_DIM ** -0.5
        o = _head_softmax_pv(s, v_ref[b].astype(BF16))
        acc = jnp.zeros((seq_len, MEM_WIDTH), F32)
        for h in range(MEM_HEADS):
            acc = acc + jnp.where(out_head == h, o[h * seq_len:(h + 1) * seq_len], 0.0)
        o_ref[b * seq_len:(b + 1) * seq_len, :] = acc


def _mem_attn_decode(proj, q_col, mem_k, mem_v, seq_len):
    t = proj.shape[0]
    nb = mem_k.shape[0]
    bb = min(SEQ_BATCH, nb)
    return pl.pallas_call(
        functools.partial(_mem_attn_decode_kernel, seq_len=seq_len),
        grid=(nb // bb,),
        in_specs=[
            pl.BlockSpec((bb * seq_len, MEM_WIDTH), lambda i: (i, q_col)),
            pl.BlockSpec((bb, N_MEM, MEM_WIDTH), lambda i: (i, 0, 0)),
            pl.BlockSpec((bb, N_MEM, MEM_WIDTH), lambda i: (i, 0, 0)),
        ],
        out_specs=pl.BlockSpec((bb * seq_len, MEM_WIDTH), lambda i: (i, 0)),
        out_shape=jax.ShapeDtypeStruct((t, MEM_WIDTH), F32),
        compiler_params=_params(("parallel",)),
        name="mem_attn_decode",
    )(proj, mem_k, mem_v)


def _sink_softmax_pv(s, sink, v):
    m = jnp.maximum(jnp.max(s, axis=-1, keepdims=True), sink)
    p = jnp.exp(s - m)
    denom = jnp.sum(p, axis=-1, keepdims=True) + jnp.exp(sink - m)
    return _dot(p.astype(BF16), v) / denom


def _swa_prefill_kernel(slope_ref, sink_ref, q_ref, kp_ref, kc_ref, vp_ref, vc_ref, o_ref, *, steps_per_seq):
    w = SWA_WINDOW
    first = (pl.program_id(0) % steps_per_seq) == 0
    k_all = jnp.concatenate([kp_ref[...], kc_ref[...]], axis=0)
    v_all = jnp.concatenate([vp_ref[...], vc_ref[...]], axis=0)
    qi = lax.broadcasted_iota(jnp.int32, (w, 2 * w), 0)
    kj = lax.broadcasted_iota(jnp.int32, (w, 2 * w), 1)
    dist = qi + w - kj
    in_window = (dist >= 0) & (dist < w)
    distf = dist.astype(F32)
    for j in range(q_ref.shape[0] // w):
        q = q_ref[j * w:(j + 1) * w, :]
        k = k_all[j * w:(j + 2) * w]
        v = v_all[j * w:(j + 2) * w]
        valid = in_window & ((kj >= w) | jnp.logical_not(first)) if j == 0 else in_window
        outs = []
        for kh in range(SWA_KV_HEADS):
            kk = k[:, kh * SWA_HEAD_DIM:(kh + 1) * SWA_HEAD_DIM].astype(BF16)
            vv = v[:, kh * SWA_HEAD_DIM:(kh + 1) * SWA_HEAD_DIM].astype(BF16)
            heads = [kh * SWA_GROUP + g for g in range(SWA_GROUP)]
            qg = jnp.concatenate([q[:, h * SWA_HEAD_DIM:(h + 1) * SWA_HEAD_DIM] for h in heads], axis=0).astype(BF16)
            s3 = _dot_nt(qg, kk) * SWA_HEAD_DIM ** -0.5
            for g, h in enumerate(heads):
                s = s3[g * w:(g + 1) * w] - slope_ref[h] * distf
                s = jnp.where(valid, s, NEG_INF)
                outs.append(_sink_softmax_pv(s, sink_ref[h], vv))
        o_ref[j * w:(j + 1) * w, :] = jnp.concatenate(outs, axis=1)


def _swa_prefill(proj, slopes, sinks, seq_len):
    t = proj.shape[0]
    w = SWA_WINDOW
    nq = SWA_QUERY_BLOCKS
    per_seq = seq_len // (nq * w)
    kcol = SWA_Q_WIDTH // SWA_KV_WIDTH
    vcol = kcol + 1

    def prev(i):
        return jnp.where(i % per_seq == 0, nq * i, nq * i - 1)

    smem = pl.BlockSpec(memory_space=pltpu.SMEM)
    return pl.pallas_call(
        functools.partial(_swa_prefill_kernel, steps_per_seq=per_seq),
        grid=(t // (nq * w),),
        in_specs=[
            smem,
            smem,
            pl.BlockSpec((nq * w, SWA_Q_WIDTH), lambda i: (i, 0)),
            pl.BlockSpec((w, SWA_KV_WIDTH), lambda i: (prev(i), kcol)),
            pl.BlockSpec((nq * w, SWA_KV_WIDTH), lambda i: (i, kcol)),
            pl.BlockSpec((w, SWA_KV_WIDTH), lambda i: (prev(i), vcol)),
            pl.BlockSpec((nq * w, SWA_KV_WIDTH), lambda i: (i, vcol)),
        ],
        out_specs=pl.BlockSpec((nq * w, SWA_Q_WIDTH), lambda i: (i, 0)),
        out_shape=jax.ShapeDtypeStruct((t, SWA_Q_WIDTH), F32),
        compiler_params=_params(("parallel",)),
        name="swa_prefill",
    )(slopes, sinks, proj, proj, proj, proj, proj)


def _swa_decode_kernel(q_ref, kn_ref, vn_ref, kc_ref, vc_ref, rep_ref, rep_t_ref, slope_ref, sink_ref, o_ref, *,
                       seq_len):
    w = SWA_WINDOW
    rows = SWA_HEADS * seq_len
    span = 2 * w
    row = lax.broadcasted_iota(jnp.int32, (rows, SWA_Q_WIDTH), 0)
    col = lax.broadcasted_iota(jnp.int32, (rows, SWA_Q_WIDTH), 1)
    diag = _div(row, seq_len) == _div(col, SWA_HEAD_DIM)
    ql = lax.broadcasted_iota(jnp.int32, (rows, span), 0) & (seq_len - 1)
    kj = lax.broadcasted_iota(jnp.int32, (rows, span), 1)
    dist = w + ql - kj
    valid = (dist >= 0) & (dist < w)
    bias = slope_ref[...] * dist.astype(F32)
    sink = sink_ref[...]
    out_head = _div(lax.broadcasted_iota(jnp.int32, (seq_len, SWA_Q_WIDTH), 1), SWA_HEAD_DIM)
    tail = jnp.zeros((w - seq_len, SWA_KV_WIDTH), F32)
    nb = kc_ref.shape[0]
    seqs = [slice(b * seq_len, (b + 1) * seq_len) for b in range(nb)]
    qd = [jnp.where(diag, jnp.concatenate([q_ref[rs, :]] * SWA_HEADS, axis=0), 0.0) for rs in seqs]
    qk = _dot(jnp.concatenate(qd, axis=0).astype(BF16), rep_t_ref[...]).astype(BF16)
    outs = []
    for b, rs in enumerate(seqs):
        k = jnp.concatenate([kc_ref[b], kn_ref[rs, :], tail], axis=0).astype(BF16)
        v = jnp.concatenate([vc_ref[b], vn_ref[rs, :], tail], axis=0).astype(BF16)
        s = _dot_nt(qk[b * rows:(b + 1) * rows], k) * SWA_HEAD_DIM ** -0.5 - bias
        s = jnp.where(valid, s, NEG_INF)
        outs.append(_sink_softmax_pv(s, sink, v).astype(BF16))
    o = _dot(jnp.concatenate(outs, axis=0), rep_ref[...])
    for b, rs in enumerate(seqs):
        acc = jnp.zeros((seq_len, SWA_Q_WIDTH), F32)
        for h in range(SWA_HEADS):
            acc = acc + jnp.where(out_head == h, o[b * rows + h * seq_len:b * rows + (h + 1) * seq_len], 0.0)
        o_ref[rs, :] = acc


def _swa_decode(proj, cache_k, cache_v, slopes, sinks, seq_len):
    t = proj.shape[0]
    nb = cache_k.shape[0]
    bb = min(SEQ_BATCH, nb)
    w = SWA_WINDOW
    kcol = SWA_Q_WIDTH // SWA_KV_WIDTH
    src = (jnp.arange(SWA_Q_WIDTH) // SWA_HEAD_DIM // SWA_GROUP) * SWA_HEAD_DIM + jnp.arange(SWA_Q_WIDTH) % SWA_HEAD_DIM
    rep = (jnp.arange(SWA_KV_WIDTH)[:, None] == src[None, :]).astype(BF16)
    slope_rows = jnp.repeat(slopes, seq_len).reshape(-1, 1)
    sink_rows = jnp.repeat(sinks, seq_len).reshape(-1, 1)
    rows = SWA_HEADS * seq_len
    return pl.pallas_call(
        functools.partial(_swa_decode_kernel, seq_len=seq_len),
        grid=(nb // bb,),
        in_specs=[
            pl.BlockSpec((bb * seq_len, SWA_Q_WIDTH), lambda i: (i, 0)),
            pl.BlockSpec((bb * seq_len, SWA_KV_WIDTH), lambda i: (i, kcol)),
            pl.BlockSpec((bb * seq_len, SWA_KV_WIDTH), lambda i: (i, kcol + 1)),
            pl.BlockSpec((bb, w, SWA_KV_WIDTH), lambda i: (i, 0, 0)),
            pl.BlockSpec((bb, w, SWA_KV_WIDTH), lambda i: (i, 0, 0)),
            pl.BlockSpec((SWA_KV_WIDTH, SWA_Q_WIDTH), lambda i: (0, 0)),
            pl.BlockSpec((SWA_Q_WIDTH, SWA_KV_WIDTH), lambda i: (0, 0)),
            pl.BlockSpec((rows, 1), lambda i: (0, 0)),
            pl.BlockSpec((rows, 1), lambda i: (0, 0)),
        ],
        out_specs=pl.BlockSpec((bb * seq_len, SWA_Q_WIDTH), lambda i: (i, 0)),
        out_shape=jax.ShapeDtypeStruct((t, SWA_Q_WIDTH), F32),
        compiler_params=_params(("parallel",)),
        name="swa_decode",
    )(proj, proj, proj, cache_k, cache_v, rep, rep.T, slope_rows, sink_rows)


def _gdn_kernel(*refs, nblk, spb):
    r = GDN_ROWS
    c = r // spb
    qkv_refs = refs[0:nblk]
    z_refs = refs[nblk:2 * nblk]
    ba_refs = refs[2 * nblk:3 * nblk]
    conv0_ref, s0_ref, cw_ref, alog_ref, dtb_ref, gn_ref, o_ref, sfin_ref, s_scr, fbuf = refs[3 * nblk:]
    step = pl.program_id(1)
    nseq = nblk * spb
    tail = fbuf.shape[1]
    hist = GDN_CONV_K - 1

    @pl.when(step == 0)
    def _():
        s_scr[...] = s0_ref[...]
        fbuf[...] = jnp.zeros_like(fbuf)
        for s in range(nseq):
            fbuf[s, tail - hist:tail, :] = conv0_ref[s]

    ri = lax.broadcasted_iota(jnp.int32, (r, r), 0)
    ci = lax.broadcasted_iota(jnp.int32, (r, r), 1)
    same = _div(ri, c) == _div(ci, c)
    tri = same & (ri >= ci)
    strict = same & (ri > ci)
    eye = (ri == ci).astype(F32)
    cum_mat = jnp.concatenate([tri.astype(F32), same.astype(F32)], axis=0)
    row_seq = _div(lax.broadcasted_iota(jnp.int32, (r, 1), 0), c)
    row_tail = lax.broadcasted_iota(jnp.int32, (tail, 1), 0)
    zeros_rr = jnp.zeros((r, LANES), F32)
    cw = cw_ref[...]
    neg_a = -jnp.exp(alog_ref[...])
    gn = gn_ref[...]

    blocks = []
    for n in range(nblk):
        u = qkv_refs[n][...]
        pieces = []
        for s in range(spb):
            idx = n * spb + s
            us = u[s * c:(s + 1) * c]
            prev = fbuf[idx]
            acc = us * cw[hist:hist + 1]
            for back in range(1, GDN_CONV_K):
                moved = pltpu.roll(us, back, 0)
                head = jnp.where(row_tail < back, pltpu.roll(prev, back, 0), moved[0:tail])
                moved = head if c == tail else jnp.concatenate([head, moved[tail:]], axis=0)
                acc = acc + moved * cw[hist - back:hist - back + 1]
            fbuf[idx] = us[c - tail:c]
            pieces.append(acc)
        conv = pieces[0] if spb == 1 else jnp.concatenate(pieces, axis=0)
        qkv = jax.nn.silu(conv)

        ba = ba_refs[n][...]
        beta_all = jax.nn.sigmoid(ba)
        xg = ba + dtb_ref[...]
        g_all = neg_a * (jnp.maximum(xg, 0.0) + jnp.log(1.0 + jnp.exp(-jnp.abs(xg))))
        gsum = jnp.dot(cum_mat, g_all, preferred_element_type=F32, precision=HIGHEST)
        gcum = gsum[0:r]
        gtot = gsum[r:2 * r]
        gcum_t = jnp.transpose(jnp.concatenate([gcum, zeros_rr], axis=0))

        blocks.append((qkv, beta_all, gcum, gtot, gcum_t))

    items = [(n, h) for n in range(nblk) for h in range(GDN_HEADS)]
    bf = lambda x: x.astype(BF16)
    k_n, kb_n, q_n, qk_dec, vb_n, kdec_t, decay_n, glast_n = [], [], [], [], [], [], [], []
    for n, h in items:
        qkv, beta_all, gcum, gtot, gcum_t = blocks[n]
        lo = h * GDN_DK
        q = qkv[:, lo:lo + GDN_DK]
        k = qkv[:, GDN_QK_WIDTH + lo:GDN_QK_WIDTH + lo + GDN_DK]
        v = qkv[:, 2 * GDN_QK_WIDTH + lo:2 * GDN_QK_WIDTH + lo + GDN_DV]
        q = q * lax.rsqrt(jnp.sum(q * q, axis=-1, keepdims=True) + L2_EPS) * GDN_DK ** -0.5
        k = k * lax.rsqrt(jnp.sum(k * k, axis=-1, keepdims=True) + L2_EPS)
        beta = beta_all[:, h:h + 1]
        gcol = gcum[:, GDN_HEADS + h:GDN_HEADS + h + 1]
        grow = gcum_t[GDN_HEADS + h:GDN_HEADS + h + 1, 0:r]
        glast = gtot[:, GDN_HEADS + h:GDN_HEADS + h + 1]
        eg = jnp.exp(gcol)
        kb = k * beta
        decay_n.append(jnp.where(tri, jnp.exp(jnp.where(tri, gcol - grow, 0.0)), 0.0))
        k_n.append(bf(k))
        kb_n.append(kb)
        q_n.append(q)
        vb_n.append(v * beta)
        qk_dec.append(bf(jnp.concatenate([kb * eg, q * eg], axis=0)))
        kdec = jnp.concatenate([k * jnp.exp(glast - gcol), zeros_rr], axis=0)
        kdec_t.append(bf(jnp.transpose(kdec)[:, 0:r]))
        glast_n.append(glast)

    kq = [_dot_nt(bf(jnp.concatenate([kb_n[i], q_n[i]], axis=0)), k_n[i]) for i in range(len(items))]
    power = [jnp.where(strict, kq[i][0:r] * decay_n[i], 0.0) for i in range(len(items))]
    qk = [bf(kq[i][r:2 * r] * decay_n[i]) for i in range(len(items))]
    inv = [eye - p for p in power]
    span = 2
    while span < c:
        power = [_dot(bf(p), bf(p)) for p in power]
        inv = [_dot(bf(a), bf(eye + p)) for a, p in zip(inv, power)]
        span *= 2
    inv = [bf(a) for a in inv]
    tq = [jnp.concatenate([a, bf(_dot(b, a))], axis=0) for a, b in zip(inv, qk)]

    if spb == 1:
        ks_qs = [_dot(qk_dec[i], bf(s_scr[n, h])) for i, (n, h) in enumerate(items)]
        resid = [vb_n[i] - ks_qs[i][0:r] for i in range(len(items))]
        qs = [x[r:2 * r] for x in ks_qs]
    else:
        resid, qs = [], []
        for i, (n, h) in enumerate(items):
            both = []
            for s in range(spb):
                rows = jnp.concatenate([qk_dec[i][s * c:(s + 1) * c], qk_dec[i][r + s * c:r + (s + 1) * c]], axis=0)
                both.append(_dot(rows, bf(s_scr[n * spb + s, h])))
            resid.append(vb_n[i] - jnp.concatenate([x[0:c] for x in both], axis=0))
            qs.append(jnp.concatenate([x[c:2 * c] for x in both], axis=0))

    vo = [_dot(tq[i], bf(resid[i])) for i in range(len(items))]
    for i, (n, h) in enumerate(items):
        v_new = vo[i][0:r]
        for s in range(spb):
            idx = n * spb + s
            vs = v_new if spb == 1 else jnp.where(row_seq == s, v_new, 0.0)
            carry = jnp.exp(glast_n[i][s * c:s * c + 1, :])
            s_scr[idx, h] = s_scr[idx, h] * carry + _dot(kdec_t[i], bf(vs))
    for i, (n, h) in enumerate(items):
        on = _rms(qs[i] + vo[i][r:2 * r], gn)
        zh = z_refs[n][:, h * GDN_DV:(h + 1) * GDN_DV]
        o_ref[n, :, h * GDN_DV:(h + 1) * GDN_DV] = on * jax.nn.silu(zh)

    @pl.when(step == pl.num_programs(1) - 1)
    def _():
        sfin_ref[...] = s_scr[...]


def _gdn(proj, conv_buf, s0, conv_w, a_log, dt_bias, norm_g, *, nblk, spb, steps):
    t = proj.shape[0]
    r = GDN_ROWS
    groups = t // (r * nblk * steps)
    nseq = nblk * spb
    c = r // spb
    z_col = GDN_CONV_DIM // GDN_V_WIDTH
    ba_col = (GDN_CONV_DIM + GDN_V_WIDTH + MEM_WIDTH) // LANES

    def rows(n, col):
        return lambda g, l: ((g * nblk + n) * steps + l, col)

    lane6 = jnp.zeros((1, LANES), F32)
    alog = lane6.at[0, GDN_HEADS:2 * GDN_HEADS].set(a_log)
    dtb = lane6.at[0, GDN_HEADS:2 * GDN_HEADS].set(dt_bias)
    const = lambda g, l: (0, 0)
    in_specs = (
        [pl.BlockSpec((r, GDN_CONV_DIM), rows(n, 0)) for n in range(nblk)]
        + [pl.BlockSpec((r, GDN_V_WIDTH), rows(n, z_col)) for n in range(nblk)]
        + [pl.BlockSpec((r, LANES), rows(n, ba_col)) for n in range(nblk)]
        + [
            pl.BlockSpec((nseq, GDN_CONV_K - 1, GDN_CONV_DIM), lambda g, l: (g, 0, 0)),
            pl.BlockSpec((nseq, GDN_HEADS, GDN_DK, GDN_DV), lambda g, l: (g, 0, 0, 0)),
            pl.BlockSpec((GDN_CONV_K, GDN_CONV_DIM), const),
            pl.BlockSpec((1, LANES), const),
            pl.BlockSpec((1, LANES), const),
            pl.BlockSpec((1, GDN_DV), const),
        ]
    )
    out, s_fin = pl.pallas_call(
        functools.partial(_gdn_kernel, nblk=nblk, spb=spb),
        grid=(groups, steps),
        in_specs=in_specs,
        out_specs=[
            pl.BlockSpec((nblk, r, GDN_V_WIDTH), lambda g, l: (0, g * steps + l, 0)),
            pl.BlockSpec((nseq, GDN_HEADS, GDN_DK, GDN_DV), lambda g, l: (g, 0, 0, 0)),
        ],
        out_shape=[
            jax.ShapeDtypeStruct((nblk, groups * steps * r, GDN_V_WIDTH), F32),
            jax.ShapeDtypeStruct((groups * nseq, GDN_HEADS, GDN_DK, GDN_DV), F32),
        ],
        scratch_shapes=[
            pltpu.VMEM((nseq, GDN_HEADS, GDN_DK, GDN_DV), F32),
            pltpu.VMEM((nseq, 8, GDN_CONV_DIM), F32),
        ],
        compiler_params=_params(("parallel", "arbitrary")),
        name="gdn",
    )(*([proj] * (3 * nblk)), conv_buf, s0, conv_w, alog, dtb, norm_g.reshape(1, GDN_DV))
    return out.reshape(t, GDN_V_WIDTH), s_fin


def _dispatch(route, tm, p):
    t = route.shape[0]
    experts = route[:, 0:2].astype(jnp.int32).reshape(-1)
    onehot = (experts[:, None] == jnp.arange(N_EXPERTS)[None, :]).astype(jnp.int32)
    csum = jnp.cumsum(onehot, axis=0)
    rank = jnp.take_along_axis(csum, experts[:, None], axis=1)[:, 0] - 1
    tiles = (csum[-1] + tm - 1) // tm
    tile_end = jnp.cumsum(tiles)
    start = (tile_end - tiles) * tm
    dest = start[experts] + rank
    n_tiles = tile_end[-1:].astype(jnp.int32)
    tile_expert = jnp.sum(tile_end[None, :] <= jnp.arange(p // tm)[:, None], axis=1)
    tile_expert = jnp.minimum(tile_expert, N_EXPERTS - 1).astype(jnp.int32)
    src = jnp.zeros((p,), jnp.int32).at[dest].set(jnp.arange(2 * t, dtype=jnp.int32) // 2)
    return dest, src, tile_expert, n_tiles


def kernel(x_prompt, x_sample, state_gdn_conv, state_gdn_ssm, cache_swa_k, cache_swa_v, cache_mem_k, cache_mem_v, mem_prompt, attn_norm, ffn_norm, mem_norm, final_norm, w_in_gdn, gdn_conv_w, gdn_a_log, gdn_dt_bias, gdn_norm, w_out_gdn, w_in_swa, swa_sinks, w_out_swa, w_mem_kv, w_ffn_gu, w_ffn_down, w_router, w_exp_gu, w_exp_down):
    bp, lp, d = x_prompt.shape
    bs, ls, _ = x_sample.shape
    tp, ts = bp * lp, bs * ls
    xp = x_prompt.reshape(tp, d)
    xs = x_sample.reshape(ts, d)

    mem = mem_prompt.reshape(bp * N_MEM, d)
    mem_kv = [
        _norm_matmul(mem, mem_norm[i], w_mem_kv[i].astype(BF16)).reshape(bp, N_MEM, 2 * MEM_WIDTH)
        for i in range(2)
    ]
    new_mem_k = jnp.stack([kv[..., :MEM_WIDTH].reshape(bp, N_MEM, MEM_HEADS, MEM_HEAD_DIM) for kv in mem_kv])
    new_mem_v = jnp.stack([kv[..., MEM_WIDTH:].reshape(bp, N_MEM, MEM_HEADS, MEM_HEAD_DIM) for kv in mem_kv])

    w_in = w_in_gdn[0]
    o_z = GDN_CONV_DIM + GDN_V_WIDTH
    o_mem = o_z + 2 * GDN_HEADS
    w0 = jnp.concatenate(
        [w_in[:, :o_z], w_in[:, o_mem:], w_in[:, o_z:o_mem], jnp.zeros((d, LANES - 2 * GDN_HEADS), F32)], axis=1
    ).astype(BF16)
    mq_col0 = o_z // MEM_WIDTH
    w_out0 = w_out_gdn[0].astype(BF16)
    w_gu0 = w_ffn_gu[0].astype(BF16)
    w_dn0 = w_ffn_down[0].astype(BF16)

    proj_p = _norm_matmul(xp, attn_norm[0], w0)
    proj_s = _norm_matmul(xs, attn_norm[0], w0)

    zero_conv = jnp.zeros((bp, GDN_CONV_K - 1, GDN_CONV_DIM), F32)
    zero_state = jnp.zeros((bp, GDN_HEADS, GDN_DK, GDN_DV), F32)
    gdn_p, ssm_p = _gdn(proj_p, zero_conv, zero_state, gdn_conv_w[0], gdn_a_log[0], gdn_dt_bias[0], gdn_norm[0],
                        nblk=bp, spb=1, steps=lp // GDN_ROWS)
    gdn_s, ssm_s = _gdn(proj_s, state_gdn_conv[0], state_gdn_ssm[0], gdn_conv_w[0], gdn_a_log[0], gdn_dt_bias[0],
                        gdn_norm[0], nblk=1, spb=GDN_ROWS // ls, steps=1)
    hist = GDN_CONV_K - 1
    conv_p = proj_p.reshape(bp, lp, -1)[:, lp - hist:, :GDN_CONV_DIM]
    conv_s = proj_s.reshape(bs, ls, -1)[:, ls - hist:, :GDN_CONV_DIM]

    memo_p = _mem_attn_shared(proj_p, mq_col0, mem_kv[0], lp)
    memo_s = _mem_attn_decode(proj_s, mq_col0, cache_mem_k[0].reshape(bs, N_MEM, MEM_WIDTH),
                              cache_mem_v[0].reshape(bs, N_MEM, MEM_WIDTH), ls)
    xp = _mixer_ffn(xp, gdn_p, memo_p, w_out0, ffn_norm[0], w_gu0, w_dn0)
    xs = _mixer_ffn(xs, gdn_s, memo_s, w_out0, ffn_norm[0], w_gu0, w_dn0)

    w1 = w_in_swa[0].astype(BF16)
    mq_col1 = (SWA_Q_WIDTH + 2 * SWA_KV_WIDTH) // MEM_WIDTH
    w_out1 = w_out_swa[0].astype(BF16)
    slopes = 2.0 ** (-8.0 * jnp.arange(1, SWA_HEADS + 1, dtype=F32) / SWA_HEADS)
    sinks = swa_sinks[0].astype(F32)

    proj_p = _norm_matmul(xp, attn_norm[1], w1)
    proj_s = _norm_matmul(xs, attn_norm[1], w1)
    swa_p = _swa_prefill(proj_p, slopes, sinks, lp)
    cache_k = cache_swa_k[0].reshape(bs, SWA_WINDOW, SWA_KV_WIDTH)
    cache_v = cache_swa_v[0].reshape(bs, SWA_WINDOW, SWA_KV_WIDTH)
    swa_s = _swa_decode(proj_s, cache_k, cache_v, slopes, sinks, ls)

    k0, v0 = SWA_Q_WIDTH, SWA_Q_WIDTH + SWA_KV_WIDTH
    pp = proj_p.reshape(bp, lp, -1)
    ps = proj_s.reshape(bs, ls, -1)
    kv_shape = (SWA_KV_HEADS, SWA_HEAD_DIM)
    swk_p = pp[:, lp - SWA_WINDOW:, k0:k0 + SWA_KV_WIDTH].reshape(bp, SWA_WINDOW, *kv_shape)
    swv_p = pp[:, lp - SWA_WINDOW:, v0:v0 + SWA_KV_WIDTH].reshape(bp, SWA_WINDOW, *kv_shape)
    swk_s = jnp.concatenate([cache_k[:, ls:], ps[:, :, k0:k0 + SWA_KV_WIDTH]], axis=1).reshape(bs, SWA_WINDOW, *kv_shape)
    swv_s = jnp.concatenate([cache_v[:, ls:], ps[:, :, v0:v0 + SWA_KV_WIDTH]], axis=1).reshape(bs, SWA_WINDOW, *kv_shape)

    memo_p = _mem_attn_shared(proj_p, mq_col1, mem_kv[1], lp)
    memo_s = _mem_attn_decode(proj_s, mq_col1, cache_mem_k[1].reshape(bs, N_MEM, MEM_WIDTH),
                              cache_mem_v[1].reshape(bs, N_MEM, MEM_WIDTH), ls)
    t_all = tp + ts
    w_r = jnp.concatenate([w_router[0], jnp.zeros((d, LANES - N_EXPERTS), F32)], axis=1)
    x_all, route = _out_proj_route((xp, swa_p, memo_p), (xs, swa_s, memo_s), w_out1, ffn_norm[1], w_r)
    tm = MOE_TILE
    p_rows = -(-(2 * t_all + N_EXPERTS * (tm - 1)) // tm) * tm
    dest, src, tile_expert, n_tiles = _dispatch(route, tm, p_rows)
    rows_out = _moe_ffn(x_all, ffn_norm[1], src, tile_expert, n_tiles,
                        w_exp_gu[0].astype(BF16), w_exp_down[0].astype(BF16))
    y_p = _combine_norm(x_all, route, rows_out, dest, final_norm, 0, tp)
    y_s = _combine_norm(x_all, route, rows_out, dest, final_norm, tp, ts)

    return (
        y_p.reshape(bp, lp, d),
        y_s.reshape(bs, ls, d),
        conv_p[None],
        ssm_p[None],
        swk_p[None],
        swv_p[None],
        new_mem_k,
        new_mem_v,
        conv_s[None],
        ssm_s[None],
        swk_s[None],
        swv_s[None],
    )
```

```python
import functools

import jax
import jax.numpy as jnp
from jax import lax
from jax.experimental import pallas as pl
from jax.experimental.pallas import tpu as pltpu

F32 = jnp.float32
BF16 = jnp.bfloat16
HIGHEST = lax.Precision.HIGHEST

D_MODEL = 1024
RMS_EPS = 1e-6
L2_EPS = 1e-6
NEG_INF = -1e30

GDN_HEADS = 6
GDN_DK = 128
GDN_DV = 128
GDN_CONV_K = 4
GDN_QK_WIDTH = GDN_HEADS * GDN_DK
GDN_V_WIDTH = GDN_HEADS * GDN_DV
GDN_CONV_DIM = 2 * GDN_QK_WIDTH + GDN_V_WIDTH
GDN_ROWS = 64

SWA_HEADS = 12
SWA_KV_HEADS = 4
SWA_HEAD_DIM = 64
SWA_GROUP = SWA_HEADS // SWA_KV_HEADS
SWA_WINDOW = 128
SWA_Q_WIDTH = SWA_HEADS * SWA_HEAD_DIM
SWA_KV_WIDTH = SWA_KV_HEADS * SWA_HEAD_DIM

N_MEM = 256
MEM_HEADS = 4
MEM_HEAD_DIM = 64
MEM_WIDTH = MEM_HEADS * MEM_HEAD_DIM

FFN_DIM = 2816
N_EXPERTS = 8
EXPERT_DIM = 3584

LANES = 128
SUBLANES = 8
VMEM_LIMIT = 56 * 1024 * 1024

TOKEN_TILE = 512
MOE_TILE = 512
FFN_COLS = 1408
EXPERT_STEPS = 2
EXPERT_COLS = EXPERT_DIM // EXPERT_STEPS
SEQ_BATCH = 8
SWA_QUERY_BLOCKS = 1


def _params(sem):
    return pltpu.CompilerParams(dimension_semantics=sem, vmem_limit_bytes=VMEM_LIMIT)


def _rms(x, g):
    return x * lax.rsqrt(jnp.mean(x * x, axis=-1, keepdims=True) + RMS_EPS) * g


def _dot(a, b):
    return jnp.dot(a, b, preferred_element_type=F32)


def _div(x, d):
    assert d & (d - 1) == 0
    return lax.shift_right_logical(x, d.bit_length() - 1)


def _dot_nt(a, b):
    return lax.dot_general(a, b, (((1,), (1,)), ((), ())), preferred_element_type=F32)


def _norm_matmul_kernel(x_ref, g_ref, w_ref, o_ref):
    o_ref[...] = _dot(_rms(x_ref[...], g_ref[...]).astype(BF16), w_ref[...])


def _norm_matmul(x, g, w):
    t, k = x.shape
    n = w.shape[1]
    tm = min(TOKEN_TILE, t)
    return pl.pallas_call(
        _norm_matmul_kernel,
        grid=(t // tm,),
        in_specs=[
            pl.BlockSpec((tm, k), lambda i: (i, 0)),
            pl.BlockSpec((1, k), lambda i: (0, 0)),
            pl.BlockSpec((k, n), lambda i: (0, 0)),
        ],
        out_specs=pl.BlockSpec((tm, n), lambda i: (i, 0)),
        out_shape=jax.ShapeDtypeStruct((t, n), F32),
        compiler_params=_params(("parallel",)),
        name="norm_matmul",
    )(x, g.reshape(1, k), w)


def _out_proj_block(x_ref, a_ref, m_ref, w_ref, o_ref):
    na = a_ref.shape[1]
    y = _dot(a_ref[...].astype(BF16), w_ref[0:na, :])
    y = y + _dot(m_ref[...].astype(BF16), w_ref[na:, :])
    o_ref[...] = x_ref[...] + y


def _route(h, w_router):
    logits = jnp.dot(h, w_router, preferred_element_type=F32, precision=HIGHEST)
    lane = lax.broadcasted_iota(jnp.int32, logits.shape, 1).astype(F32)
    logits = jnp.where(lane < N_EXPERTS, logits, -jnp.inf)
    m1 = jnp.max(logits, axis=-1, keepdims=True)
    i1 = jnp.min(jnp.where(logits == m1, lane, float(LANES)), axis=-1, keepdims=True)
    rest = jnp.where(lane == i1, -jnp.inf, logits)
    m2 = jnp.max(rest, axis=-1, keepdims=True)
    i2 = jnp.min(jnp.where(rest == m2, lane, float(LANES)), axis=-1, keepdims=True)
    e2 = jnp.exp(m2 - m1)
    w1 = 1.0 / (1.0 + e2)
    w2 = e2 / (1.0 + e2)
    r = jnp.where(lane == 0, i1, 0.0)
    r = jnp.where(lane == 1, i2, r)
    r = jnp.where(lane == 2, w1, r)
    return jnp.where(lane == 3, w2, r)


def _out_proj_route_kernel(x0_ref, a0_ref, m0_ref, x1_ref, a1_ref, m1_ref, w_ref, g_ref, wr_ref,
                           o_ref, r_ref, *, n0):
    i = pl.program_id(0)

    @pl.when(i < n0)
    def _():
        _out_proj_block(x0_ref, a0_ref, m0_ref, w_ref, o_ref)

    @pl.when(i >= n0)
    def _():
        _out_proj_block(x1_ref, a1_ref, m1_ref, w_ref, o_ref)

    r_ref[...] = _route(_rms(o_ref[...], g_ref[...]), wr_ref[...])


def _out_proj_route(first, second, w, g, w_router_pad):
    (x0, a0, m0), (x1, a1, m1) = first, second
    d = x0.shape[1]
    tm = min(TOKEN_TILE, x0.shape[0], x1.shape[0])
    n0, n1 = x0.shape[0] // tm, x1.shape[0] // tm

    def group0(width):
        return pl.BlockSpec((tm, width), lambda i: (jnp.minimum(i, n0 - 1), 0))

    def group1(width):
        return pl.BlockSpec((tm, width), lambda i: (jnp.maximum(i - n0, 0), 0))

    return pl.pallas_call(
        functools.partial(_out_proj_route_kernel, n0=n0),
        grid=(n0 + n1,),
        in_specs=[
            group0(d), group0(a0.shape[1]), group0(m0.shape[1]),
            group1(d), group1(a1.shape[1]), group1(m1.shape[1]),
            pl.BlockSpec(w.shape, lambda i: (0, 0)),
            pl.BlockSpec((1, d), lambda i: (0, 0)),
            pl.BlockSpec((d, LANES), lambda i: (0, 0)),
        ],
        out_specs=[
            pl.BlockSpec((tm, d), lambda i: (i, 0)),
            pl.BlockSpec((tm, LANES), lambda i: (i, 0)),
        ],
        out_shape=[
            jax.ShapeDtypeStruct(((n0 + n1) * tm, d), F32),
            jax.ShapeDtypeStruct(((n0 + n1) * tm, LANES), F32),
        ],
        compiler_params=_params(("parallel",)),
        name="out_proj_route",
    )(x0, a0, m0, x1, a1, m1, w, g.reshape(1, d), w_router_pad)


def _mixer_ffn_kernel(x_ref, a_ref, m_ref, wo_ref, g_ref, wg_ref, wu_ref, wd_ref, o_ref, x1_ref, h_ref, acc_ref):
    f = pl.program_id(1)

    @pl.when(f == 0)
    def _():
        _out_proj_block(x_ref, a_ref, m_ref, wo_ref, x1_ref)
        h_ref[...] = _rms(x1_ref[...], g_ref[...]).astype(BF16)
        acc_ref[...] = jnp.zeros_like(acc_ref)

    h = h_ref[...]
    a = jax.nn.silu(_dot(h, wg_ref[...])) * _dot(h, wu_ref[...])
    acc_ref[...] += _dot(a.astype(BF16), wd_ref[...])

    @pl.when(f == pl.num_programs(1) - 1)
    def _():
        o_ref[...] = x1_ref[...] + acc_ref[...]


def _mixer_ffn(x, a, m, w_out, g, w_gu, w_down):
    t, d = x.shape
    tm = min(TOKEN_TILE, t)
    tf = FFN_COLS
    nf = FFN_DIM // tf
    return pl.pallas_call(
        _mixer_ffn_kernel,
        grid=(t // tm, nf),
        in_specs=[
            pl.BlockSpec((tm, d), lambda i, f: (i, 0)),
            pl.BlockSpec((tm, a.shape[1]), lambda i, f: (i, 0)),
            pl.BlockSpec((tm, m.shape[1]), lambda i, f: (i, 0)),
            pl.BlockSpec(w_out.shape, lambda i, f: (0, 0)),
            pl.BlockSpec((1, d), lambda i, f: (0, 0)),
            pl.BlockSpec((d, tf), lambda i, f: (0, f)),
            pl.BlockSpec((d, tf), lambda i, f: (0, nf + f)),
            pl.BlockSpec((tf, d), lambda i, f: (f, 0)),
        ],
        out_specs=pl.BlockSpec((tm, d), lambda i, f: (i, 0)),
        out_shape=jax.ShapeDtypeStruct((t, d), F32),
        scratch_shapes=[pltpu.VMEM((tm, d), F32), pltpu.VMEM((tm, d), BF16), pltpu.VMEM((tm, d), F32)],
        compiler_params=_params(("parallel", "arbitrary")),
        name="mixer_ffn",
    )(x, a, m, w_out, g.reshape(1, d), w_gu, w_gu, w_down)


def _start_row_gather(src_hbm, idx_ref, dst, sem):
    def body(r, carry):
        row = idx_ref[r]
        pltpu.make_async_copy(src_hbm.at[pl.ds(row, 1)], dst.at[pl.ds(r, 1)], sem).start()
        return carry

    lax.fori_loop(0, dst.shape[0], body, 0, unroll=8)


def _wait_row_gather(src_hbm, dst, sem):
    pltpu.make_async_copy(src_hbm.at[pl.ds(0, dst.shape[0])], dst, sem).wait()


def _moe_kernel(te_ref, nt_ref, dest_ref, x_hbm, g_ref, wg_ref, wu_ref, wd_ref, o_ref,
                xbuf, h_ref, acc_ref, src_ref, sem):
    i = pl.program_id(0)
    f = pl.program_id(1)
    tm = h_ref.shape[0]
    nt = nt_ref[0]
    slot = i % 2

    @pl.when((i == 0) & (f == 0))
    def _():
        def clear(p, carry):
            src_ref[p] = 0
            return carry

        def place(q, carry):
            src_ref[dest_ref[q]] = lax.shift_right_logical(q, 1)
            return carry

        lax.fori_loop(0, src_ref.shape[0], clear, 0, unroll=8)
        lax.fori_loop(0, dest_ref.shape[0], place, 0, unroll=8)
        _start_row_gather(x_hbm, src_ref, xbuf.at[0], sem.at[0])

    nf = pl.num_programs(1)
    share = tm // EXPERT_STEPS

    def column_step(prefetch):
        if prefetch:
            for r in range(share):
                row = src_ref[(i + 1) * tm + f * share + r]
                pltpu.make_async_copy(
                    x_hbm.at[pl.ds(row, 1)], xbuf.at[1 - slot, pl.ds(f * share + r, 1)], sem.at[1 - slot]
                ).start()
        h = h_ref[...]
        a = jax.nn.silu(_dot(h, wg_ref[...])) * _dot(h, wu_ref[...])
        acc_ref[...] += _dot(a.astype(BF16), wd_ref[...])

    @pl.when(i < nt)
    def _():
        @pl.when(f == 0)
        def _():
            _wait_row_gather(x_hbm, xbuf.at[slot], sem.at[slot])
            h_ref[...] = _rms(xbuf[slot], g_ref[...]).astype(BF16)
            acc_ref[...] = jnp.zeros_like(acc_ref)

        @pl.when(i + 1 < nt)
        def _():
            column_step(True)

        @pl.when(i + 1 >= nt)
        def _():
            column_step(False)

        @pl.when(f == nf - 1)
        def _():
            o_ref[...] = acc_ref[...]

    @pl.when((i >= nt) & (f == 0))
    def _():
        o_ref[...] = jnp.zeros_like(o_ref)


def _moe_ffn(x, g, dest, p, tile_expert, n_tiles, w_gu, w_down):
    d = x.shape[1]
    tm = MOE_TILE
    tf = EXPERT_COLS
    nf = EXPERT_DIM // tf

    def col(i, f, nt):
        return jnp.where(i < nt[0], f, nf - 1)

    grid_spec = pltpu.PrefetchScalarGridSpec(
        num_scalar_prefetch=3,
        grid=(p // tm, nf),
        in_specs=[
            pl.BlockSpec(memory_space=pl.ANY),
            pl.BlockSpec((1, d), lambda i, f, te, nt, src: (0, 0)),
            pl.BlockSpec((None, d, tf), lambda i, f, te, nt, src: (te[i], 0, col(i, f, nt))),
            pl.BlockSpec((None, d, tf), lambda i, f, te, nt, src: (te[i], 0, nf + col(i, f, nt))),
            pl.BlockSpec((None, tf, d), lambda i, f, te, nt, src: (te[i], col(i, f, nt), 0)),
        ],
        out_specs=pl.BlockSpec((tm, d), lambda i, f, te, nt, src: (i, 0)),
        scratch_shapes=[
            pltpu.VMEM((2, tm, d), F32),
            pltpu.VMEM((tm, d), BF16),
            pltpu.VMEM((tm, d), F32),
            pltpu.SMEM((p,), jnp.int32),
            pltpu.SemaphoreType.DMA((2,)),
        ],
    )
    return pl.pallas_call(
        _moe_kernel,
        grid_spec=grid_spec,
        out_shape=jax.ShapeDtypeStruct((p, d), F32),
        compiler_params=_params(("arbitrary", "arbitrary")),
        name="moe_ffn",
    )(tile_expert, n_tiles, dest, x, g.reshape(1, d), w_gu, w_gu, w_down)


def _combine_norm_kernel(dest_ref, x_ref, r_ref, y_hbm, g_ref, o_ref, buf, sem, *, tok0):
    i = pl.program_id(0)
    tm = x_ref.shape[0]
    slot = i % 2

    def fetch(tile, s):
        base = 2 * (tok0 + tile * tm)

        def body(r8, carry):
            for j in range(SUBLANES):
                for k in range(2):
                    row = dest_ref[base + 2 * (r8 * SUBLANES + j) + k]
                    pltpu.make_async_copy(
                        y_hbm.at[pl.ds(row, 1)], buf.at[s, k, r8, pl.ds(j, 1)], sem.at[s, k]
                    ).start(priority=k)
            return carry

        lax.fori_loop(0, tm // SUBLANES, body, 0)

    @pl.when(i == 0)
    def _():
        fetch(0, 0)

    @pl.when(i + 1 < pl.num_programs(0))
    def _():
        fetch(i + 1, 1 - slot)

    for k in range(2):
        pltpu.make_async_copy(buf.at[slot, k], buf.at[slot, k], sem.at[slot, k]).wait()
    picks = [buf[slot, k].reshape(tm, buf.shape[-1]) for k in range(2)]
    moe = picks[0] * r_ref[:, 2:3] + picks[1] * r_ref[:, 3:4]
    o_ref[...] = _rms(x_ref[...] + moe, g_ref[...])


def _combine_norm(x, route, rows_out, dest, g, tok0, rows):
    d = x.shape[1]
    tm = min(TOKEN_TILE, rows)
    blk0 = tok0 // tm
    grid_spec = pltpu.PrefetchScalarGridSpec(
        num_scalar_prefetch=1,
        grid=(rows // tm,),
        in_specs=[
            pl.BlockSpec((tm, d), lambda i, dest: (i + blk0, 0)),
            pl.BlockSpec((tm, LANES), lambda i, dest: (i + blk0, 0)),
            pl.BlockSpec(memory_space=pl.ANY),
            pl.BlockSpec((1, d), lambda i, dest: (0, 0)),
        ],
        out_specs=pl.BlockSpec((tm, d), lambda i, dest: (i, 0)),
        scratch_shapes=[pltpu.VMEM((2, 2, tm // SUBLANES, SUBLANES, d), F32), pltpu.SemaphoreType.DMA((2, 2))],
    )
    return pl.pallas_call(
        functools.partial(_combine_norm_kernel, tok0=tok0),
        grid_spec=grid_spec,
        out_shape=jax.ShapeDtypeStruct((rows, d), F32),
        compiler_params=_params(("arbitrary",)),
        name="combine_norm",
    )(dest, x, route, rows_out, g.reshape(1, d))


def _head_softmax_pv(s, v):
    m = jnp.max(s, axis=-1, keepdims=True)
    p = jnp.exp(s - m)
    return _dot(p.astype(BF16), v) / jnp.sum(p, axis=-1, keepdims=True)


def _mem_attn_shared_kernel(q_ref, kv_ref, o_ref):
    q = q_ref[...]
    k = kv_ref[0, :, 0:MEM_WIDTH].astype(BF16)
    v = kv_ref[0, :, MEM_WIDTH:].astype(BF16)
    col = lax.broadcasted_iota(jnp.int32, (1, MEM_WIDTH), 1)
    acc = jnp.zeros(q.shape, F32)
    for h in range(MEM_HEADS):
        in_head = (col >= h * MEM_HEAD_DIM) & (col < (h + 1) * MEM_HEAD_DIM)
        qh = jnp.where(in_head, q, 0.0).astype(BF16)
        s = _dot_nt(qh, k) * MEM_HEAD_DIM ** -0.5
        acc = acc + jnp.where(in_head, _head_softmax_pv(s, v), 0.0)
    o_ref[...] = acc


def _mem_attn_shared(proj, q_col, mem_kv, seq_len):
    t = proj.shape[0]
    tq = min(TOKEN_TILE, seq_len)
    per_seq = seq_len // tq
    return pl.pallas_call(
        _mem_attn_shared_kernel,
        grid=(t // tq,),
        in_specs=[
            pl.BlockSpec((tq, MEM_WIDTH), lambda i: (i, q_col)),
            pl.BlockSpec((1, N_MEM, 2 * MEM_WIDTH), lambda i: (i // per_seq, 0, 0)),
        ],
        out_specs=pl.BlockSpec((tq, MEM_WIDTH), lambda i: (i, 0)),
        out_shape=jax.ShapeDtypeStruct((t, MEM_WIDTH), F32),
        compiler_params=_params(("parallel",)),
        name="mem_attn_shared",
    )(proj, mem_kv)


def _mem_attn_decode_kernel(q_ref, k_ref, v_ref, o_ref, *, seq_len):
    rows = MEM_HEADS * seq_len
    row_head = _div(lax.broadcasted_iota(jnp.int32, (rows, MEM_WIDTH), 0), seq_len)
    col_head = _div(lax.broadcasted_iota(jnp.int32, (rows, MEM_WIDTH), 1), MEM_HEAD_DIM)
    diag = row_head == col_head
    out_head = _div(lax.broadcasted_iota(jnp.int32, (seq_len, MEM_WIDTH), 1), MEM_HEAD_DIM)
    for b in range(k_ref.shape[0]):
        q = q_ref[b * seq_len:(b + 1) * seq_len, :]
        qd = jnp.where(diag, jnp.concatenate([q] * MEM_HEADS, axis=0), 0.0).astype(BF16)
        s = _dot_nt(qd, k_ref[b].astype(BF16)) * MEM_HEAD_DIM ** -0.5
        o = _head_softmax_pv(s, v_ref[b].astype(BF16))
        acc = jnp.zeros((seq_len, MEM_WIDTH), F32)
        for h in range(MEM_HEADS):
            acc = acc + jnp.where(out_head == h, o[h * seq_len:(h + 1) * seq_len], 0.0)
        o_ref[b * seq_len:(b + 1) * seq_len, :] = acc


def _mem_attn_decode(proj, q_col, mem_k, mem_v, seq_len):
    t = proj.shape[0]
    nb = mem_k.shape[0]
    bb = min(SEQ_BATCH, nb)
    return pl.pallas_call(
        functools.partial(_mem_attn_decode_kernel, seq_len=seq_len),
        grid=(nb // bb,),
        in_specs=[
            pl.BlockSpec((bb * seq_len, MEM_WIDTH), lambda i: (i, q_col)),
            pl.BlockSpec((bb, N_MEM, MEM_WIDTH), lambda i: (i, 0, 0)),
            pl.BlockSpec((bb, N_MEM, MEM_WIDTH), lambda i: (i, 0, 0)),
        ],
        out_specs=pl.BlockSpec((bb * seq_len, MEM_WIDTH), lambda i: (i, 0)),
        out_shape=jax.ShapeDtypeStruct((t, MEM_WIDTH), F32),
        compiler_params=_params(("parallel",)),
        name="mem_attn_decode",
    )(proj, mem_k, mem_v)


def _sink_softmax_pv(s, sink, v):
    m = jnp.maximum(jnp.max(s, axis=-1, keepdims=True), sink)
    p = jnp.exp(s - m)
    denom = jnp.sum(p, axis=-1, keepdims=True) + jnp.exp(sink - m)
    return _dot(p.astype(BF16), v) / denom


def _swa_prefill_kernel(slope_ref, sink_ref, q_ref, kp_ref, kc_ref, vp_ref, vc_ref, o_ref, *, steps_per_seq):
    w = SWA_WINDOW
    first = (pl.program_id(0) % steps_per_seq) == 0
    k_all = jnp.concatenate([kp_ref[...], kc_ref[...]], axis=0)
    v_all = jnp.concatenate([vp_ref[...], vc_ref[...]], axis=0)
    qi = lax.broadcasted_iota(jnp.int32, (w, 2 * w), 0)
    kj = lax.broadcasted_iota(jnp.int32, (w, 2 * w), 1)
    dist = qi + w - kj
    in_window = (dist >= 0) & (dist < w)
    distf = dist.astype(F32)
    for j in range(q_ref.shape[0] // w):
        q = q_ref[j * w:(j + 1) * w, :]
        k = k_all[j * w:(j + 2) * w]
        v = v_all[j * w:(j + 2) * w]
        valid = in_window & ((kj >= w) | jnp.logical_not(first)) if j == 0 else in_window
        outs = []
        for kh in range(SWA_KV_HEADS):
            kk = k[:, kh * SWA_HEAD_DIM:(kh + 1) * SWA_HEAD_DIM].astype(BF16)
            vv = v[:, kh * SWA_HEAD_DIM:(kh + 1) * SWA_HEAD_DIM].astype(BF16)
            heads = [kh * SWA_GROUP + g for g in range(SWA_GROUP)]
            qg = jnp.concatenate([q[:, h * SWA_HEAD_DIM:(h + 1) * SWA_HEAD_DIM] for h in heads], axis=0).astype(BF16)
            s3 = _dot_nt(qg, kk) * SWA_HEAD_DIM ** -0.5
            for g, h in enumerate(heads):
                s = s3[g * w:(g + 1) * w] - slope_ref[h] * distf
                s = jnp.where(valid, s, NEG_INF)
                outs.append(_sink_softmax_pv(s, sink_ref[h], vv))
        o_ref[j * w:(j + 1) * w, :] = jnp.concatenate(outs, axis=1)


def _swa_prefill(proj, slopes, sinks, seq_len):
    t = proj.shape[0]
    w = SWA_WINDOW
    nq = SWA_QUERY_BLOCKS
    per_seq = seq_len // (nq * w)
    kcol = SWA_Q_WIDTH // SWA_KV_WIDTH
    vcol = kcol + 1

    def prev(i):
        return jnp.where(i % per_seq == 0, nq * i, nq * i - 1)

    smem = pl.BlockSpec(memory_space=pltpu.SMEM)
    return pl.pallas_call(
        functools.partial(_swa_prefill_kernel, steps_per_seq=per_seq),
        grid=(t // (nq * w),),
        in_specs=[
            smem,
            smem,
            pl.BlockSpec((nq * w, SWA_Q_WIDTH), lambda i: (i, 0)),
            pl.BlockSpec((w, SWA_KV_WIDTH), lambda i: (prev(i), kcol)),
            pl.BlockSpec((nq * w, SWA_KV_WIDTH), lambda i: (i, kcol)),
            pl.BlockSpec((w, SWA_KV_WIDTH), lambda i: (prev(i), vcol)),
            pl.BlockSpec((nq * w, SWA_KV_WIDTH), lambda i: (i, vcol)),
        ],
        out_specs=pl.BlockSpec((nq * w, SWA_Q_WIDTH), lambda i: (i, 0)),
        out_shape=jax.ShapeDtypeStruct((t, SWA_Q_WIDTH), F32),
        compiler_params=_params(("parallel",)),
        name="swa_prefill",
    )(slopes, sinks, proj, proj, proj, proj, proj)


def _swa_decode_kernel(q_ref, kn_ref, vn_ref, kc_ref, vc_ref, rep_ref, rep_t_ref, slope_ref, sink_ref, o_ref, *,
                       seq_len):
    w = SWA_WINDOW
    rows = SWA_HEADS * seq_len
    span = 2 * w
    row = lax.broadcasted_iota(jnp.int32, (rows, SWA_Q_WIDTH), 0)
    col = lax.broadcasted_iota(jnp.int32, (rows, SWA_Q_WIDTH), 1)
    diag = _div(row, seq_len) == _div(col, SWA_HEAD_DIM)
    ql = lax.broadcasted_iota(jnp.int32, (rows, span), 0) & (seq_len - 1)
    kj = lax.broadcasted_iota(jnp.int32, (rows, span), 1)
    dist = w + ql - kj
    valid = (dist >= 0) & (dist < w)
    bias = slope_ref[...] * dist.astype(F32)
    sink = sink_ref[...]
    out_head = _div(lax.broadcasted_iota(jnp.int32, (seq_len, SWA_Q_WIDTH), 1), SWA_HEAD_DIM)
    tail = jnp.zeros((w - seq_len, SWA_KV_WIDTH), F32)
    nb = kc_ref.shape[0]
    seqs = [slice(b * seq_len, (b + 1) * seq_len) for b in range(nb)]
    qd = [jnp.where(diag, jnp.concatenate([q_ref[rs, :]] * SWA_HEADS, axis=0), 0.0) for rs in seqs]
    qk = _dot(jnp.concatenate(qd, axis=0).astype(BF16), rep_t_ref[...]).astype(BF16)
    outs = []
    for b, rs in enumerate(seqs):
        k = jnp.concatenate([kc_ref[b], kn_ref[rs, :], tail], axis=0).astype(BF16)
        v = jnp.concatenate([vc_ref[b], vn_ref[rs, :], tail], axis=0).astype(BF16)
        s = _dot_nt(qk[b * rows:(b + 1) * rows], k) * SWA_HEAD_DIM ** -0.5 - bias
        s = jnp.where(valid, s, NEG_INF)
        outs.append(_sink_softmax_pv(s, sink, v).astype(BF16))
    o = _dot(jnp.concatenate(outs, axis=0), rep_ref[...])
    for b, rs in enumerate(seqs):
        acc = jnp.zeros((seq_len, SWA_Q_WIDTH), F32)
        for h in range(SWA_HEADS):
            acc = acc + jnp.where(out_head == h, o[b * rows + h * seq_len:b * rows + (h + 1) * seq_len], 0.0)
        o_ref[rs, :] = acc


def _swa_decode(proj, cache_k, cache_v, slopes, sinks, seq_len):
    t = proj.shape[0]
    nb = cache_k.shape[0]
    bb = min(SEQ_BATCH, nb)
    w = SWA_WINDOW
    kcol = SWA_Q_WIDTH // SWA_KV_WIDTH
    src = (jnp.arange(SWA_Q_WIDTH) // SWA_HEAD_DIM // SWA_GROUP) * SWA_HEAD_DIM + jnp.arange(SWA_Q_WIDTH) % SWA_HEAD_DIM
    rep = (jnp.arange(SWA_KV_WIDTH)[:, None] == src[None, :]).astype(BF16)
    slope_rows = jnp.repeat(slopes, seq_len).reshape(-1, 1)
    sink_rows = jnp.repeat(sinks, seq_len).reshape(-1, 1)
    rows = SWA_HEADS * seq_len
    return pl.pallas_call(
        functools.partial(_swa_decode_kernel, seq_len=seq_len),
        grid=(nb // bb,),
        in_specs=[
            pl.BlockSpec((bb * seq_len, SWA_Q_WIDTH), lambda i: (i, 0)),
            pl.BlockSpec((bb * seq_len, SWA_KV_WIDTH), lambda i: (i, kcol)),
            pl.BlockSpec((bb * seq_len, SWA_KV_WIDTH), lambda i: (i, kcol + 1)),
            pl.BlockSpec((bb, w, SWA_KV_WIDTH), lambda i: (i, 0, 0)),
            pl.BlockSpec((bb, w, SWA_KV_WIDTH), lambda i: (i, 0, 0)),
            pl.BlockSpec((SWA_KV_WIDTH, SWA_Q_WIDTH), lambda i: (0, 0)),
            pl.BlockSpec((SWA_Q_WIDTH, SWA_KV_WIDTH), lambda i: (0, 0)),
            pl.BlockSpec((rows, 1), lambda i: (0, 0)),
            pl.BlockSpec((rows, 1), lambda i: (0, 0)),
        ],
        out_specs=pl.BlockSpec((bb * seq_len, SWA_Q_WIDTH), lambda i: (i, 0)),
        out_shape=jax.ShapeDtypeStruct((t, SWA_Q_WIDTH), F32),
        compiler_params=_params(("parallel",)),
        name="swa_decode",
    )(proj, proj, proj, cache_k, cache_v, rep, rep.T, slope_rows, sink_rows)


def _gdn_kernel(*refs, nblk, spb):
    r = GDN_ROWS
    c = r // spb
    qkv_refs = refs[0:nblk]
    z_refs = refs[nblk:2 * nblk]
    ba_refs = refs[2 * nblk:3 * nblk]
    conv0_ref, s0_ref, cw_ref, alog_ref, dtb_ref, gn_ref, o_ref, sfin_ref, s_scr, fbuf = refs[3 * nblk:]
    step = pl.program_id(1)
    nseq = nblk * spb
    tail = fbuf.shape[1]
    hist = GDN_CONV_K - 1

    @pl.when(step == 0)
    def _():
        s_scr[...] = s0_ref[...]
        fbuf[...] = jnp.zeros_like(fbuf)
        for s in range(nseq):
            fbuf[s, tail - hist:tail, :] = conv0_ref[s]

    ri = lax.broadcasted_iota(jnp.int32, (r, r), 0)
    ci = lax.broadcasted_iota(jnp.int32, (r, r), 1)
    same = _div(ri, c) == _div(ci, c)
    tri = same & (ri >= ci)
    strict = same & (ri > ci)
    eye = (ri == ci).astype(F32)
    cum_mat = jnp.concatenate([tri.astype(F32), same.astype(F32)], axis=0)
    row_seq = _div(lax.broadcasted_iota(jnp.int32, (r, 1), 0), c)
    row_tail = lax.broadcasted_iota(jnp.int32, (tail, 1), 0)
    zeros_rr = jnp.zeros((r, LANES), F32)
    cw = cw_ref[...]
    neg_a = -jnp.exp(alog_ref[...])
    gn = gn_ref[...]

    blocks = []
    for n in range(nblk):
        u = qkv_refs[n][...]
        pieces = []
        for s in range(spb):
            idx = n * spb + s
            us = u[s * c:(s + 1) * c]
            prev = fbuf[idx]
            acc = us * cw[hist:hist + 1]
            for back in range(1, GDN_CONV_K):
                moved = pltpu.roll(us, back, 0)
                head = jnp.where(row_tail < back, pltpu.roll(prev, back, 0), moved[0:tail])
                moved = head if c == tail else jnp.concatenate([head, moved[tail:]], axis=0)
                acc = acc + moved * cw[hist - back:hist - back + 1]
            fbuf[idx] = us[c - tail:c]
            pieces.append(acc)
        conv = pieces[0] if spb == 1 else jnp.concatenate(pieces, axis=0)
        qkv = jax.nn.silu(conv)

        ba = ba_refs[n][...]
        beta_all = jax.nn.sigmoid(ba)
        xg = ba + dtb_ref[...]
        g_all = neg_a * (jnp.maximum(xg, 0.0) + jnp.log(1.0 + jnp.exp(-jnp.abs(xg))))
        gsum = jnp.dot(cum_mat, g_all, preferred_element_type=F32, precision=HIGHEST)
        gcum = gsum[0:r]
        gtot = gsum[r:2 * r]
        gcum_t = jnp.transpose(jnp.concatenate([gcum, zeros_rr], axis=0))

        blocks.append((qkv, beta_all, gcum, gtot, gcum_t))

    items = [(n, h) for n in range(nblk) for h in range(GDN_HEADS)]
    bf = lambda x: x.astype(BF16)
    k_n, kb_n, q_n, qk_dec, vb_n, kdec_t, decay_n, glast_n = [], [], [], [], [], [], [], []
    for n, h in items:
        qkv, beta_all, gcum, gtot, gcum_t = blocks[n]
        lo = h * GDN_DK
        q = qkv[:, lo:lo + GDN_DK]
        k = qkv[:, GDN_QK_WIDTH + lo:GDN_QK_WIDTH + lo + GDN_DK]
        v = qkv[:, 2 * GDN_QK_WIDTH + lo:2 * GDN_QK_WIDTH + lo + GDN_DV]
        q = q * lax.rsqrt(jnp.sum(q * q, axis=-1, keepdims=True) + L2_EPS) * GDN_DK ** -0.5
        k = k * lax.rsqrt(jnp.sum(k * k, axis=-1, keepdims=True) + L2_EPS)
        beta = beta_all[:, h:h + 1]
        gcol = gcum[:, GDN_HEADS + h:GDN_HEADS + h + 1]
        grow = gcum_t[GDN_HEADS + h:GDN_HEADS + h + 1, 0:r]
        glast = gtot[:, GDN_HEADS + h:GDN_HEADS + h + 1]
        eg = jnp.exp(gcol)
        kb = k * beta
        decay_n.append(jnp.where(tri, jnp.exp(jnp.where(tri, gcol - grow, 0.0)), 0.0))
        k_n.append(bf(k))
        kb_n.append(kb)
        q_n.append(q)
        vb_n.append(v * beta)
        qk_dec.append(bf(jnp.concatenate([kb * eg, q * eg], axis=0)))
        kdec = jnp.concatenate([k * jnp.exp(glast - gcol), zeros_rr], axis=0)
        kdec_t.append(bf(jnp.transpose(kdec)[:, 0:r]))
        glast_n.append(glast)

    kq = [_dot_nt(bf(jnp.concatenate([kb_n[i], q_n[i]], axis=0)), k_n[i]) for i in range(len(items))]
    power = [jnp.where(strict, kq[i][0:r] * decay_n[i], 0.0) for i in range(len(items))]
    qk = [bf(kq[i][r:2 * r] * decay_n[i]) for i in range(len(items))]
    inv = [eye - p for p in power]
    span = 2
    while span < c:
        power = [_dot(bf(p), bf(p)) for p in power]
        inv = [_dot(bf(a), bf(eye + p)) for a, p in zip(inv, power)]
        span *= 2
    inv = [bf(a) for a in inv]
    tq = [jnp.concatenate([a, bf(_dot(b, a))], axis=0) for a, b in zip(inv, qk)]

    if spb == 1:
        ks_qs = [_dot(qk_dec[i], bf(s_scr[n, h])) for i, (n, h) in enumerate(items)]
        resid = [vb_n[i] - ks_qs[i][0:r] for i in range(len(items))]
        qs = [x[r:2 * r] for x in ks_qs]
    else:
        resid, qs = [], []
        for i, (n, h) in enumerate(items):
            both = []
            for s in range(spb):
                rows = jnp.concatenate([qk_dec[i][s * c:(s + 1) * c], qk_dec[i][r + s * c:r + (s + 1) * c]], axis=0)
                both.append(_dot(rows, bf(s_scr[n * spb + s, h])))
            resid.append(vb_n[i] - jnp.concatenate([x[0:c] for x in both], axis=0))
            qs.append(jnp.concatenate([x[c:2 * c] for x in both], axis=0))

    vo = [_dot(tq[i], bf(resid[i])) for i in range(len(items))]
    for i, (n, h) in enumerate(items):
        v_new = vo[i][0:r]
        for s in range(spb):
            idx = n * spb + s
            vs = v_new if spb == 1 else jnp.where(row_seq == s, v_new, 0.0)
            carry = jnp.exp(glast_n[i][s * c:s * c + 1, :])
            s_scr[idx, h] = s_scr[idx, h] * carry + _dot(kdec_t[i], bf(vs))
    for i, (n, h) in enumerate(items):
        on = _rms(qs[i] + vo[i][r:2 * r], gn)
        zh = z_refs[n][:, h * GDN_DV:(h + 1) * GDN_DV]
        o_ref[n, :, h * GDN_DV:(h + 1) * GDN_DV] = on * jax.nn.silu(zh)

    @pl.when(step == pl.num_programs(1) - 1)
    def _():
        sfin_ref[...] = s_scr[...]


def _gdn(proj, conv_buf, s0, conv_w, a_log, dt_bias, norm_g, *, nblk, spb, steps):
    t = proj.shape[0]
    r = GDN_ROWS
    groups = t // (r * nblk * steps)
    nseq = nblk * spb
    c = r // spb
    z_col = GDN_CONV_DIM // GDN_V_WIDTH
    ba_col = (GDN_CONV_DIM + GDN_V_WIDTH + MEM_WIDTH) // LANES

    def rows(n, col):
        return lambda g, l: ((g * nblk + n) * steps + l, col)

    lane6 = jnp.zeros((1, LANES), F32)
    alog = lane6.at[0, GDN_HEADS:2 * GDN_HEADS].set(a_log)
    dtb = lane6.at[0, GDN_HEADS:2 * GDN_HEADS].set(dt_bias)
    const = lambda g, l: (0, 0)
    in_specs = (
        [pl.BlockSpec((r, GDN_CONV_DIM), rows(n, 0)) for n in range(nblk)]
        + [pl.BlockSpec((r, GDN_V_WIDTH), rows(n, z_col)) for n in range(nblk)]
        + [pl.BlockSpec((r, LANES), rows(n, ba_col)) for n in range(nblk)]
        + [
            pl.BlockSpec((nseq, GDN_CONV_K - 1, GDN_CONV_DIM), lambda g, l: (g, 0, 0)),
            pl.BlockSpec((nseq, GDN_HEADS, GDN_DK, GDN_DV), lambda g, l: (g, 0, 0, 0)),
            pl.BlockSpec((GDN_CONV_K, GDN_CONV_DIM), const),
            pl.BlockSpec((1, LANES), const),
            pl.BlockSpec((1, LANES), const),
            pl.BlockSpec((1, GDN_DV), const),
        ]
    )
    out, s_fin = pl.pallas_call(
        functools.partial(_gdn_kernel, nblk=nblk, spb=spb),
        grid=(groups, steps),
        in_specs=in_specs,
        out_specs=[
            pl.BlockSpec((nblk, r, GDN_V_WIDTH), lambda g, l: (0, g * steps + l, 0)),
            pl.BlockSpec((nseq, GDN_HEADS, GDN_DK, GDN_DV), lambda g, l: (g, 0, 0, 0)),
        ],
        out_shape=[
            jax.ShapeDtypeStruct((nblk, groups * steps * r, GDN_V_WIDTH), F32),
            jax.ShapeDtypeStruct((groups * nseq, GDN_HEADS, GDN_DK, GDN_DV), F32),
        ],
        scratch_shapes=[
            pltpu.VMEM((nseq, GDN_HEADS, GDN_DK, GDN_DV), F32),
            pltpu.VMEM((nseq, 8, GDN_CONV_DIM), F32),
        ],
        compiler_params=_params(("parallel", "arbitrary")),
        name="gdn",
    )(*([proj] * (3 * nblk)), conv_buf, s0, conv_w, alog, dtb, norm_g.reshape(1, GDN_DV))
    return out.reshape(t, GDN_V_WIDTH), s_fin


def _dispatch(route, tm, p):
    t = route.shape[0]
    experts = route[:, 0:2].astype(jnp.int32).reshape(-1)
    onehot = (experts[:, None] == jnp.arange(N_EXPERTS)[None, :]).astype(jnp.int32)
    csum = jnp.cumsum(onehot, axis=0)
    rank = jnp.take_along_axis(csum, experts[:, None], axis=1)[:, 0] - 1
    tiles = (csum[-1] + tm - 1) // tm
    tile_end = jnp.cumsum(tiles)
    start = (tile_end - tiles) * tm
    dest = start[experts] + rank
    n_tiles = tile_end[-1:].astype(jnp.int32)
    tile_expert = jnp.sum(tile_end[None, :] <= jnp.arange(p // tm)[:, None], axis=1)
    tile_expert = jnp.minimum(tile_expert, N_EXPERTS - 1).astype(jnp.int32)
    return dest.astype(jnp.int32), tile_expert, n_tiles


def kernel(x_prompt, x_sample, state_gdn_conv, state_gdn_ssm, cache_swa_k, cache_swa_v, cache_mem_k, cache_mem_v, mem_prompt, attn_norm, ffn_norm, mem_norm, final_norm, w_in_gdn, gdn_conv_w, gdn_a_log, gdn_dt_bias, gdn_norm, w_out_gdn, w_in_swa, swa_sinks, w_out_swa, w_mem_kv, w_ffn_gu, w_ffn_down, w_router, w_exp_gu, w_exp_down):
    bp, lp, d = x_prompt.shape
    bs, ls, _ = x_sample.shape
    tp, ts = bp * lp, bs * ls
    xp = x_prompt.reshape(tp, d)
    xs = x_sample.reshape(ts, d)

    mem = mem_prompt.reshape(bp * N_MEM, d)
    mem_kv = [
        _norm_matmul(mem, mem_norm[i], w_mem_kv[i].astype(BF16)).reshape(bp, N_MEM, 2 * MEM_WIDTH)
        for i in range(2)
    ]
    new_mem_k = jnp.stack([kv[..., :MEM_WIDTH].reshape(bp, N_MEM, MEM_HEADS, MEM_HEAD_DIM) for kv in mem_kv])
    new_mem_v = jnp.stack([kv[..., MEM_WIDTH:].reshape(bp, N_MEM, MEM_HEADS, MEM_HEAD_DIM) for kv in mem_kv])

    w_in = w_in_gdn[0]
    o_z = GDN_CONV_DIM + GDN_V_WIDTH
    o_mem = o_z + 2 * GDN_HEADS
    w0 = jnp.concatenate(
        [w_in[:, :o_z], w_in[:, o_mem:], w_in[:, o_z:o_mem], jnp.zeros((d, LANES - 2 * GDN_HEADS), F32)], axis=1
    ).astype(BF16)
    mq_col0 = o_z // MEM_WIDTH
    w_out0 = w_out_gdn[0].astype(BF16)
    w_gu0 = w_ffn_gu[0].astype(BF16)
    w_dn0 = w_ffn_down[0].astype(BF16)

    proj_p = _norm_matmul(xp, attn_norm[0], w0)
    proj_s = _norm_matmul(xs, attn_norm[0], w0)

    zero_conv = jnp.zeros((bp, GDN_CONV_K - 1, GDN_CONV_DIM), F32)
    zero_state = jnp.zeros((bp, GDN_HEADS, GDN_DK, GDN_DV), F32)
    gdn_p, ssm_p = _gdn(proj_p, zero_conv, zero_state, gdn_conv_w[0], gdn_a_log[0], gdn_dt_bias[0], gdn_norm[0],
                        nblk=bp, spb=1, steps=lp // GDN_ROWS)
    gdn_s, ssm_s = _gdn(proj_s, state_gdn_conv[0], state_gdn_ssm[0], gdn_conv_w[0], gdn_a_log[0], gdn_dt_bias[0],
                        gdn_norm[0], nblk=1, spb=GDN_ROWS // ls, steps=1)
    hist = GDN_CONV_K - 1
    conv_p = proj_p.reshape(bp, lp, -1)[:, lp - hist:, :GDN_CONV_DIM]
    conv_s = proj_s.reshape(bs, ls, -1)[:, ls - hist:, :GDN_CONV_DIM]

    memo_p = _mem_attn_shared(proj_p, mq_col0, mem_kv[0], lp)
    memo_s = _mem_attn_decode(proj_s, mq_col0, cache_mem_k[0].reshape(bs, N_MEM, MEM_WIDTH),
                              cache_mem_v[0].reshape(bs, N_MEM, MEM_WIDTH), ls)
    xp = _mixer_ffn(xp, gdn_p, memo_p, w_out0, ffn_norm[0], w_gu0, w_dn0)
    xs = _mixer_ffn(xs, gdn_s, memo_s, w_out0, ffn_norm[0], w_gu0, w_dn0)

    w1 = w_in_swa[0].astype(BF16)
    mq_col1 = (SWA_Q_WIDTH + 2 * SWA_KV_WIDTH) // MEM_WIDTH
    w_out1 = w_out_swa[0].astype(BF16)
    slopes = 2.0 ** (-8.0 * jnp.arange(1, SWA_HEADS + 1, dtype=F32) / SWA_HEADS)
    sinks = swa_sinks[0].astype(F32)

    proj_p = _norm_matmul(xp, attn_norm[1], w1)
    proj_s = _norm_matmul(xs, attn_norm[1], w1)
    swa_p = _swa_prefill(proj_p, slopes, sinks, lp)
    cache_k = cache_swa_k[0].reshape(bs, SWA_WINDOW, SWA_KV_WIDTH)
    cache_v = cache_swa_v[0].reshape(bs, SWA_WINDOW, SWA_KV_WIDTH)
    swa_s = _swa_decode(proj_s, cache_k, cache_v, slopes, sinks, ls)

    k0, v0 = SWA_Q_WIDTH, SWA_Q_WIDTH + SWA_KV_WIDTH
    pp = proj_p.reshape(bp, lp, -1)
    ps = proj_s.reshape(bs, ls, -1)
    kv_shape = (SWA_KV_HEADS, SWA_HEAD_DIM)
    swk_p = pp[:, lp - SWA_WINDOW:, k0:k0 + SWA_KV_WIDTH].reshape(bp, SWA_WINDOW, *kv_shape)
    swv_p = pp[:, lp - SWA_WINDOW:, v0:v0 + SWA_KV_WIDTH].reshape(bp, SWA_WINDOW, *kv_shape)
    swk_s = jnp.concatenate([cache_k[:, ls:], ps[:, :, k0:k0 + SWA_KV_WIDTH]], axis=1).reshape(bs, SWA_WINDOW, *kv_shape)
    swv_s = jnp.concatenate([cache_v[:, ls:], ps[:, :, v0:v0 + SWA_KV_WIDTH]], axis=1).reshape(bs, SWA_WINDOW, *kv_shape)

    memo_p = _mem_attn_shared(proj_p, mq_col1, mem_kv[1], lp)
    memo_s = _mem_attn_decode(proj_s, mq_col1, cache_mem_k[1].reshape(bs, N_MEM, MEM_WIDTH),
                              cache_mem_v[1].reshape(bs, N_MEM, MEM_WIDTH), ls)
    t_all = tp + ts
    w_r = jnp.concatenate([w_router[0], jnp.zeros((d, LANES - N_EXPERTS), F32)], axis=1)
    x_all, route = _out_proj_route((xp, swa_p, memo_p), (xs, swa_s, memo_s), w_out1, ffn_norm[1], w_r)
    tm = MOE_TILE
    p_rows = -(-(2 * t_all + N_EXPERTS * (tm - 1)) // tm) * tm
    dest, tile_expert, n_tiles = _dispatch(route, tm, p_rows)
    rows_out = _moe_ffn(x_all, ffn_norm[1], dest, p_rows, tile_expert, n_tiles,
                        w_exp_gu[0].astype(BF16), w_exp_down[0].astype(BF16))
    y_p = _combine_norm(x_all, route, rows_out, dest, final_norm, 0, tp)
    y_s = _combine_norm(x_all, route, rows_out, dest, final_norm, tp, ts)

    return (
        y_p.reshape(bp, lp, d),
        y_s.reshape(bs, ls, d),
        conv_p[None],
        ssm_p[None],
        swk_p[None],
        swv_p[None],
        new_mem_k,
        new_mem_v,
        conv_s[None],
        ssm_s[None],
        swk_s[None],
        swv_s[None],
    )
```
